```python
import math
import jax
import jax.numpy as jnp
from jax import lax
import numpy as np

D_MODEL = 1024
BATCH = 4
SEQ = 4096
DEPTH = 2
DEC_BATCH = 128
DEC_SEQ = 8
PAST_LEN = 2048
PAGE_SIZE = 128

BRANCH_WIDTH = D_MODEL // 2
N_BRANCH = 3
S5_WIDTH = BRANCH_WIDTH
S5_GROUP = 16
S5_GROUPS = S5_WIDTH // S5_GROUP
S5_STATE = 64
FOX_HEAD_DIM = 64
FOX_HEADS = BRANCH_WIDTH // FOX_HEAD_DIM
FOX_WIDTH = FOX_HEADS * FOX_HEAD_DIM
Q_BLOCK = 128
SSD_WIDTH = BRANCH_WIDTH
SSD_HEAD_DIM = 64
SSD_HEADS = SSD_WIDTH // SSD_HEAD_DIM
SSD_GROUPS = 2
SSD_STATE = 128
SSD_CONV = 4
SSD_CHUNK = 128
SSD_CONV_DIM = SSD_WIDTH + 2 * SSD_GROUPS * SSD_STATE
D_IN = N_BRANCH * D_MODEL + S5_WIDTH + 3 * FOX_WIDTH + FOX_HEADS + SSD_WIDTH + SSD_CONV_DIM + SSD_HEADS
D_FF = ((8 * D_MODEL // 3 + 127) // 128) * 128
N_EXPERTS = 8
TOP_K = 2
D_FF_EXPERT = D_MODEL
N_DENSE = (DEPTH + 1) // 2
N_MOE = DEPTH // 2
ALPHA = (2 * DEPTH) ** 0.25
BETA = (8 * DEPTH) ** -0.25
LN_EPS = 1e-5
RMS_EPS = 1e-5

kernel_name = 'hybrid_s5_fox_ssd_decoder_step'


def _in_split_points():
    sizes = (N_BRANCH * D_MODEL, S5_WIDTH, FOX_WIDTH, FOX_WIDTH, FOX_WIDTH, FOX_HEADS,
             SSD_WIDTH, SSD_CONV_DIM, SSD_HEADS)
    return np.cumsum(sizes)[:-1].tolist()


def layer_norm(x, g, b):
    xf = x.astype(jnp.float32)
    mu = jnp.mean(xf, axis=-1, keepdims=True)
    var = jnp.mean(jnp.square(xf - mu), axis=-1, keepdims=True)
    return ((xf - mu) * lax.rsqrt(var + LN_EPS) * g.astype(jnp.float32) + b.astype(jnp.float32)).astype(x.dtype)


def swiglu(h, w_gate, w_up, w_down):
    return (jax.nn.silu(h @ w_gate) * (h @ w_up)) @ w_down


def moe_swiglu(h, w_router, b_router, w_gate, w_up, w_down):
    logits = (h @ w_router).astype(jnp.float32) + b_router.astype(jnp.float32)
    top_val, top_idx = lax.top_k(logits, TOP_K)
    top_w = jax.nn.softmax(top_val, axis=-1)
    gate = jnp.einsum('blk,blke->ble', top_w, jax.nn.one_hot(top_idx, N_EXPERTS, dtype=jnp.float32))
    out = jnp.zeros(h.shape, jnp.float32)
    for e in range(N_EXPERTS):
        out = out + gate[..., e:e + 1] * swiglu(h, w_gate[e], w_up[e], w_down[e]).astype(jnp.float32)
    return out.astype(h.dtype)


def s5_discretise(lam_re, lam_im, log_dt, b_re, b_im):
    dt = jnp.exp(log_dt.astype(jnp.float32))[:, None]
    lr = lam_re.astype(jnp.float32)
    li = lam_im.astype(jnp.float32)
    mag = jnp.exp(lr * dt)
    ab_re = mag * jnp.cos(li * dt)
    ab_im = mag * jnp.sin(li * dt)
    nr = ab_re - 1.0
    den = lr * lr + li * li
    q_re = (nr * lr + ab_im * li) / den
    q_im = (ab_im * lr - nr * li) / den
    bb_re = q_re[..., None] * b_re - q_im[..., None] * b_im
    bb_im = q_re[..., None] * b_im + q_im[..., None] * b_re
    return ab_re, ab_im, bb_re, bb_im


def s5_branch(u, h0_re, h0_im, lam_re, lam_im, log_dt, b_re, b_im, c_re, c_im, d_skip, w_glu, b_glu):
    bsz, seq, _ = u.shape
    ug = u.astype(jnp.float32).reshape(bsz, seq, S5_GROUPS, S5_GROUP)
    ab_re, ab_im, bb_re, bb_im = s5_discretise(lam_re, lam_im, log_dt, b_re, b_im)
    bu_re = jnp.einsum('gnc,blgc->blgn', bb_re, ug)
    bu_im = jnp.einsum('gnc,blgc->blgn', bb_im, ug)
    a_re = jnp.broadcast_to(ab_re, bu_re.shape)
    a_im = jnp.broadcast_to(ab_im, bu_im.shape)

    def combine(e1, e2):
        a1r, a1i, b1r, b1i = e1
        a2r, a2i, b2r, b2i = e2
        return (a2r * a1r - a2i * a1i, a2r * a1i + a2i * a1r,
                a2r * b1r - a2i * b1i + b2r, a2r * b1i + a2i * b1r + b2i)

    pr, pi, xr, xi = lax.associative_scan(combine, (a_re, a_im, bu_re, bu_im), axis=1)
    h0r = h0_re.astype(jnp.float32)[:, None]
    h0i = h0_im.astype(jnp.float32)[:, None]
    xr = xr + pr * h0r - pi * h0i
    xi = xi + pr * h0i + pi * h0r
    y = jnp.einsum('gcn,blgn->blgc', c_re, xr) - jnp.einsum('gcn,blgn->blgc', c_im, xi)
    y = y + d_skip.astype(jnp.float32).reshape(S5_GROUPS, S5_GROUP) * ug
    y = jax.nn.gelu(y.reshape(bsz, seq, S5_WIDTH))
    y = y * jax.nn.sigmoid(y @ w_glu + b_glu)
    return y.astype(u.dtype), xr[:, -1], xi[:, -1]


def fox_prompt(q, k, v, logf):
    bsz, seq, nh, dh = q.shape
    nb = seq // Q_BLOCK
    scale = dh ** -0.5
    c = jnp.cumsum(logf, axis=1)
    c_keys = c.transpose(0, 2, 1)
    qb = q.reshape(bsz, nb, Q_BLOCK, nh, dh).transpose(1, 0, 2, 3, 4)
    cb = c.reshape(bsz, nb, Q_BLOCK, nh).transpose(1, 0, 3, 2)
    kpos = jnp.arange(seq)

    def block(args):
        i, qi, ci = args
        s = jnp.einsum('bqhd,bkhd->bhqk', qi, k).astype(jnp.float32) * scale
        s = s + ci[..., None] - c_keys[:, :, None, :]
        qpos = i * Q_BLOCK + jnp.arange(Q_BLOCK)
        s = jnp.where(kpos[None, :] <= qpos[:, None], s, -jnp.inf)
        p = jax.nn.softmax(s, axis=-1).astype(v.dtype)
        return jnp.einsum('bhqk,bkhd->bqhd', p, v)

    o = lax.map(block, (jnp.arange(nb), qb, cb))
    return o.transpose(1, 0, 2, 3, 4).reshape(bsz, seq, nh * dh)


def fox_sample(q, k, v, logf, k_past, v_past, logf_past):
    bsz, tq, nh, dh = q.shape
    n_past = k_past.shape[1]
    scale = dh ** -0.5
    cp = jnp.cumsum(logf_past.astype(jnp.float32), axis=1)
    decay_past = (cp[:, -1:] - cp).transpose(0, 2, 1)
    cn = jnp.cumsum(logf, axis=1).transpose(0, 2, 1)
    s_past = (jnp.einsum('bqhd,bkhd->bhqk', q, k_past).astype(jnp.float32) * scale
              + decay_past[:, :, None, :] + cn[..., None])
    s_new = (jnp.einsum('bqhd,bkhd->bhqk', q, k).astype(jnp.float32) * scale
             + cn[..., :, None] - cn[..., None, :])
    s_new = jnp.where(jnp.tril(jnp.ones((tq, tq), bool)), s_new, -jnp.inf)
    p = jax.nn.softmax(jnp.concatenate([s_past, s_new], axis=-1), axis=-1).astype(v.dtype)
    o = (jnp.einsum('bhqk,bkhd->bqhd', p[..., :n_past], v_past)
         + jnp.einsum('bhqk,bkhd->bqhd', p[..., n_past:], v))
    return o.reshape(bsz, tq, nh * dh)


def causal_conv(xbc, conv0, w, b):
    seq = xbc.shape[1]
    xp = jnp.concatenate([conv0.astype(xbc.dtype), xbc], axis=1)
    acc = xp[:, 0:seq] * w[0]
    for j in range(1, SSD_CONV):
        acc = acc + xp[:, j:j + seq] * w[j]
    return jax.nn.silu(acc + b), xp[:, seq:]


def segsum(a):
    t = a.shape[-1]
    x = jnp.broadcast_to(a[..., :, None], a.shape + (t,))
    x = jnp.where(jnp.tril(jnp.ones((t, t), bool), -1), x, 0.0)
    ss = jnp.cumsum(x, axis=-2)
    return jnp.where(jnp.tril(jnp.ones((t, t), bool), 0), ss, -jnp.inf)


def ssd_scan(x, dt, a, bm, cm, h0):
    bsz, seq, nh, hp = x.shape
    q = math.gcd(seq, SSD_CHUNK)
    nc = seq // q
    rep = nh // SSD_GROUPS
    bh = jnp.repeat(bm, rep, axis=2).reshape(bsz, nc, q, nh, SSD_STATE)
    ch = jnp.repeat(cm, rep, axis=2).reshape(bsz, nc, q, nh, SSD_STATE)
    xd = (x * dt[..., None]).reshape(bsz, nc, q, nh, hp)
    adt = (a * dt).reshape(bsz, nc, q, nh).transpose(0, 3, 1, 2)
    a_cum = jnp.cumsum(adt, axis=-1)
    lmat = jnp.exp(segsum(adt))
    y_diag = jnp.einsum('bclhn,bcshn,bhcls,bcshp->bclhp', ch, bh, lmat, xd)
    decay_states = jnp.exp(a_cum[..., -1:] - a_cum)
    states = jnp.einsum('bclhn,bhcl,bclhp->bchpn', bh, decay_states, xd)
    states = jnp.concatenate([h0[:, None], states], axis=1)
    decay_chunk = jnp.exp(segsum(jnp.pad(a_cum[..., -1], ((0, 0), (0, 0), (1, 0)))))
    new_states = jnp.einsum('bhzc,bchpn->bzhpn', decay_chunk, states)
    states, final = new_states[:, :-1], new_states[:, -1]
    y_off = jnp.einsum('bclhn,bchpn,bhcl->bclhp', ch, states, jnp.exp(a_cum))
    return (y_diag + y_off).reshape(bsz, seq, nh, hp), final


def ssd_branch(z, xbc, dt_raw, conv0, h0, conv_w, conv_b, dt_bias, a_log, d_skip, norm_w):
    bsz, seq, _ = z.shape
    xbc_c, conv_new = causal_conv(xbc, conv0, conv_w, conv_b)
    xbc_c = xbc_c.astype(jnp.float32)
    xs, bm, cm = jnp.split(xbc_c, [SSD_WIDTH, SSD_WIDTH + SSD_GROUPS * SSD_STATE], axis=-1)
    xs = xs.reshape(bsz, seq, SSD_HEADS, SSD_HEAD_DIM)
    bm = bm.reshape(bsz, seq, SSD_GROUPS, SSD_STATE)
    cm = cm.reshape(bsz, seq, SSD_GROUPS, SSD_STATE)
    dt = jax.nn.softplus(dt_raw.astype(jnp.float32) + dt_bias.astype(jnp.float32))
    a = -jnp.exp(a_log.astype(jnp.float32))
    y, h_new = ssd_scan(xs, dt, a, bm, cm, h0.astype(jnp.float32))
    y = y + d_skip.astype(jnp.float32)[:, None] * xs
    y = y.reshape(bsz, seq, SSD_WIDTH) * jax.nn.silu(z.astype(jnp.float32))
    y = y * lax.rsqrt(jnp.mean(jnp.square(y), axis=-1, keepdims=True) + RMS_EPS) * norm_w.astype(jnp.float32)
    return y.astype(z.dtype), conv_new, h_new


def gather_pages(cache_l, page_table):
    rows = cache_l[page_table]
    return rows.reshape((page_table.shape[0], -1) + cache_l.shape[2:])


def token_mixer(h, l, p, s5_re0, s5_im0, conv0, ssd0, past):
    bsz, seq, _ = h.shape
    gates, u, q, k, v, fg, z, xbc, dt_raw = jnp.split(h @ p['w_in'][l], _in_split_points(), axis=-1)
    gates = jax.nn.sigmoid(gates.reshape(bsz, seq, N_BRANCH, D_MODEL) + p['b_gate'][l])
    y_s5, s5_re, s5_im = s5_branch(u, s5_re0, s5_im0, p['s5_lam_re'][l], p['s5_lam_im'][l],
                                   p['s5_log_dt'][l], p['s5_b_re'][l], p['s5_b_im'][l],
                                   p['s5_c_re'][l], p['s5_c_im'][l], p['s5_d'][l],
                                   p['s5_w_glu'][l], p['s5_b_glu'][l])
    q = q.reshape(bsz, seq, FOX_HEADS, FOX_HEAD_DIM)
    k = k.reshape(bsz, seq, FOX_HEADS, FOX_HEAD_DIM)
    v = v.reshape(bsz, seq, FOX_HEADS, FOX_HEAD_DIM)
    logf = jax.nn.log_sigmoid(fg.astype(jnp.float32) + p['b_fgate'][l].astype(jnp.float32))
    if past is None:
        y_fox = fox_prompt(q, k, v, logf)
    else:
        y_fox = fox_sample(q, k, v, logf, past[0], past[1], past[2])
    y_ssd, conv_new, ssd_new = ssd_branch(z, xbc, dt_raw, conv0, ssd0, p['ssd_conv_w'][l],
                                          p['ssd_conv_b'][l], p['ssd_dt_bias'][l], p['ssd_a_log'][l],
                                          p['ssd_d'][l], p['ssd_norm_w'][l])
    merged = (gates[:, :, 0] * (y_s5 @ p['w_branch_s5'][l])
              + gates[:, :, 1] * (y_fox @ p['w_branch_fox'][l])
              + gates[:, :, 2] * (y_ssd @ p['w_branch_ssd'][l]))
    out = merged @ p['w_o'][l]
    return out, (k, v, logf, s5_re, s5_im, conv_new, ssd_new)


def run_trunk(x, p, s5_re0, s5_im0, conv0, ssd0, paged):
    new = [[] for _ in range(7)]
    for l in range(DEPTH):
        past = None
        if paged is not None:
            past = (gather_pages(paged[0][l], paged[3]), gather_pages(paged[1][l], paged[3]),
                    gather_pages(paged[2][l], paged[3]))
        m, st = token_mixer(x, l, p, s5_re0[l], s5_im0[l], conv0[l], ssd0[l], past)
        for lst, s in zip(new, st):
            lst.append(s)
        x = layer_norm(ALPHA * x + m, p['ln1_g'][l], p['ln1_b'][l])
        if l % 2 == 0:
            f = swiglu(x, p['ffn_w_gate'][l // 2], p['ffn_w_up'][l // 2], p['ffn_w_down'][l // 2])
        else:
            f = moe_swiglu(x, p['moe_w_router'][l // 2], p['moe_b_router'][l // 2],
                           p['moe_w_gate'][l // 2], p['moe_w_up'][l // 2], p['moe_w_down'][l // 2])
        x = layer_norm(ALPHA * x + f, p['ln2_g'][l], p['ln2_b'][l])
    return x, [jnp.stack(s) for s in new]


def setup_inputs(seed: int = 0) -> dict:
    key = jax.random.key(seed)
    keys = jax.random.split(key, 64)
    counter = [0]

    def nk():
        counter[0] += 1
        return keys[counter[0] - 1]

    def nrm(shape, scale):
        return jax.random.normal(nk(), shape, jnp.float32) * scale

    def unif(shape, lo, hi):
        return jax.random.uniform(nk(), shape, jnp.float32, lo, hi)

    n_pages = PAST_LEN // PAGE_SIZE
    n_used = DEC_BATCH * n_pages
    n_pool = n_used + max(1, n_used // 4)
    page_table = jax.random.permutation(nk(), n_pool)[:n_used].reshape(DEC_BATCH, n_pages).astype(jnp.int32)

    x_prompt = nrm((BATCH, SEQ, D_MODEL), 1.0)
    x_sample = nrm((DEC_BATCH, DEC_SEQ, D_MODEL), 1.0)
    cache_k = nrm((DEPTH, n_pool, PAGE_SIZE, FOX_HEADS, FOX_HEAD_DIM), 1.0)
    cache_v = nrm((DEPTH, n_pool, PAGE_SIZE, FOX_HEADS, FOX_HEAD_DIM), 1.0)
    cache_logf = jax.nn.log_sigmoid(nrm((DEPTH, n_pool, PAGE_SIZE, FOX_HEADS), 1.0) + 2.5)
    state_s5_re = nrm((DEPTH, DEC_BATCH, S5_GROUPS, S5_STATE), 0.5)
    state_s5_im = nrm((DEPTH, DEC_BATCH, S5_GROUPS, S5_STATE), 0.5)
    state_conv = nrm((DEPTH, DEC_BATCH, SSD_CONV - 1, SSD_CONV_DIM), 1.0)
    state_ssd = nrm((DEPTH, DEC_BATCH, SSD_HEADS, SSD_HEAD_DIM, SSD_STATE), 0.1)

    w_in = nrm((DEPTH, D_MODEL, D_IN), D_MODEL ** -0.5)
    b_gate = nrm((DEPTH, N_BRANCH, D_MODEL), 0.01)
    b_fgate = unif((DEPTH, FOX_HEADS), 1.0, 4.0)

    s5_lam_re = -0.5 + nrm((DEPTH, S5_GROUPS, S5_STATE), 0.01)
    s5_lam_im = jnp.pi * jnp.arange(S5_STATE, dtype=jnp.float32) + nrm((DEPTH, S5_GROUPS, S5_STATE), 0.01)
    s5_log_dt = unif((DEPTH, S5_GROUPS), math.log(1e-3), math.log(1e-1))
    s5_b_re = nrm((DEPTH, S5_GROUPS, S5_STATE, S5_GROUP), (2 * S5_GROUP) ** -0.5)
    s5_b_im = nrm((DEPTH, S5_GROUPS, S5_STATE, S5_GROUP), (2 * S5_GROUP) ** -0.5)
    s5_c_re = nrm((DEPTH, S5_GROUPS, S5_GROUP, S5_STATE), (2 * S5_STATE) ** -0.5)
    s5_c_im = nrm((DEPTH, S5_GROUPS, S5_GROUP, S5_STATE), (2 * S5_STATE) ** -0.5)
    s5_d = nrm((DEPTH, S5_WIDTH), 1.0)
    s5_w_glu = nrm((DEPTH, S5_WIDTH, S5_WIDTH), S5_WIDTH ** -0.5)
    s5_b_glu = nrm((DEPTH, S5_WIDTH), 0.01)

    ssd_conv_w = nrm((DEPTH, SSD_CONV, SSD_CONV_DIM), SSD_CONV ** -0.5)
    ssd_conv_b = nrm((DEPTH, SSD_CONV_DIM), 0.01)
    dt0 = jnp.exp(unif((DEPTH, SSD_HEADS), math.log(1e-3), math.log(1e-1)))
    ssd_dt_bias = dt0 + jnp.log(-jnp.expm1(-dt0))
    ssd_a_log = jnp.log(unif((DEPTH, SSD_HEADS), 1.0, 16.0))
    ssd_d = 1.0 + nrm((DEPTH, SSD_HEADS), 0.1)
    ssd_norm_w = 1.0 + nrm((DEPTH, SSD_WIDTH), 0.1)

    w_branch_s5 = nrm((DEPTH, S5_WIDTH, D_MODEL), S5_WIDTH ** -0.5)
    w_branch_fox = nrm((DEPTH, FOX_WIDTH, D_MODEL), FOX_WIDTH ** -0.5)
    w_branch_ssd = nrm((DEPTH, SSD_WIDTH, D_MODEL), SSD_WIDTH ** -0.5)
    w_o = nrm((DEPTH, D_MODEL, D_MODEL), BETA * D_MODEL ** -0.5)

    ln1_g = 1.0 + nrm((DEPTH, D_MODEL), 0.1)
    ln1_b = nrm((DEPTH, D_MODEL), 0.01)
    ln2_g = 1.0 + nrm((DEPTH, D_MODEL), 0.1)
    ln2_b = nrm((DEPTH, D_MODEL), 0.01)

    ffn_w_gate = nrm((N_DENSE, D_MODEL, D_FF), D_MODEL ** -0.5)
    ffn_w_up = nrm((N_DENSE, D_MODEL, D_FF), D_MODEL ** -0.5)
    ffn_w_down = nrm((N_DENSE, D_FF, D_MODEL), BETA * D_FF ** -0.5)
    moe_w_router = nrm((N_MOE, D_MODEL, N_EXPERTS), D_MODEL ** -0.5)
    moe_b_router = nrm((N_MOE, N_EXPERTS), 0.01)
    moe_w_gate = nrm((N_MOE, N_EXPERTS, D_MODEL, D_FF_EXPERT), D_MODEL ** -0.5)
    moe_w_up = nrm((N_MOE, N_EXPERTS, D_MODEL, D_FF_EXPERT), D_MODEL ** -0.5)
    moe_w_down = nrm((N_MOE, N_EXPERTS, D_FF_EXPERT, D_MODEL), BETA * D_FF_EXPERT ** -0.5)

    return {'x_prompt': x_prompt, 'x_sample': x_sample, 'cache_k': cache_k, 'cache_v': cache_v,
            'cache_logf': cache_logf, 'page_table': page_table, 'state_s5_re': state_s5_re,
            'state_s5_im': state_s5_im, 'state_conv': state_conv, 'state_ssd': state_ssd,
            'w_in': w_in, 'b_gate': b_gate, 'b_fgate': b_fgate,
            's5_lam_re': s5_lam_re, 's5_lam_im': s5_lam_im, 's5_log_dt': s5_log_dt,
            's5_b_re': s5_b_re, 's5_b_im': s5_b_im, 's5_c_re': s5_c_re, 's5_c_im': s5_c_im,
            's5_d': s5_d, 's5_w_glu': s5_w_glu, 's5_b_glu': s5_b_glu,
            'ssd_conv_w': ssd_conv_w, 'ssd_conv_b': ssd_conv_b, 'ssd_dt_bias': ssd_dt_bias,
            'ssd_a_log': ssd_a_log, 'ssd_d': ssd_d, 'ssd_norm_w': ssd_norm_w,
            'w_branch_s5': w_branch_s5, 'w_branch_fox': w_branch_fox, 'w_branch_ssd': w_branch_ssd,
            'w_o': w_o, 'ln1_g': ln1_g, 'ln1_b': ln1_b, 'ln2_g': ln2_g, 'ln2_b': ln2_b,
            'ffn_w_gate': ffn_w_gate, 'ffn_w_up': ffn_w_up, 'ffn_w_down': ffn_w_down,
            'moe_w_router': moe_w_router, 'moe_b_router': moe_b_router, 'moe_w_gate': moe_w_gate,
            'moe_w_up': moe_w_up, 'moe_w_down': moe_w_down}


def reference(x_prompt, x_sample, cache_k, cache_v, cache_logf, page_table, state_s5_re, state_s5_im,
              state_conv, state_ssd, w_in, b_gate, b_fgate, s5_lam_re, s5_lam_im, s5_log_dt, s5_b_re,
              s5_b_im, s5_c_re, s5_c_im, s5_d, s5_w_glu, s5_b_glu, ssd_conv_w, ssd_conv_b, ssd_dt_bias,
              ssd_a_log, ssd_d, ssd_norm_w, w_branch_s5, w_branch_fox, w_branch_ssd, w_o, ln1_g, ln1_b,
              ln2_g, ln2_b, ffn_w_gate, ffn_w_up, ffn_w_down, moe_w_router, moe_b_router, moe_w_gate,
              moe_w_up, moe_w_down):
    p = {'w_in': w_in, 'b_gate': b_gate, 'b_fgate': b_fgate, 's5_lam_re': s5_lam_re,
         's5_lam_im': s5_lam_im, 's5_log_dt': s5_log_dt, 's5_b_re': s5_b_re, 's5_b_im': s5_b_im,
         's5_c_re': s5_c_re, 's5_c_im': s5_c_im, 's5_d': s5_d, 's5_w_glu': s5_w_glu,
         's5_b_glu': s5_b_glu, 'ssd_conv_w': ssd_conv_w, 'ssd_conv_b': ssd_conv_b,
         'ssd_dt_bias': ssd_dt_bias, 'ssd_a_log': ssd_a_log, 'ssd_d': ssd_d, 'ssd_norm_w': ssd_norm_w,
         'w_branch_s5': w_branch_s5, 'w_branch_fox': w_branch_fox, 'w_branch_ssd': w_branch_ssd,
         'w_o': w_o, 'ln1_g': ln1_g, 'ln1_b': ln1_b, 'ln2_g': ln2_g, 'ln2_b': ln2_b,
         'ffn_w_gate': ffn_w_gate, 'ffn_w_up': ffn_w_up, 'ffn_w_down': ffn_w_down,
         'moe_w_router': moe_w_router, 'moe_b_router': moe_b_router, 'moe_w_gate': moe_w_gate,
         'moe_w_up': moe_w_up, 'moe_w_down': moe_w_down}
    bsz = x_prompt.shape[0]
    z_s5 = jnp.zeros((DEPTH, bsz, S5_GROUPS, S5_STATE), jnp.float32)
    z_conv = jnp.zeros((DEPTH, bsz, SSD_CONV - 1, SSD_CONV_DIM), x_prompt.dtype)
    z_ssd = jnp.zeros((DEPTH, bsz, SSD_HEADS, SSD_HEAD_DIM, SSD_STATE), jnp.float32)
    y_prompt, st_p = run_trunk(x_prompt, p, z_s5, z_s5, z_conv, z_ssd, None)
    y_sample, st_s = run_trunk(x_sample, p, state_s5_re, state_s5_im, state_conv, state_ssd,
                               (cache_k, cache_v, cache_logf, page_table))
    k_p, v_p, lf_p, s5r_p, s5i_p, conv_p, ssd_p = st_p
    k_s, v_s, lf_s, s5r_s, s5i_s, conv_s, ssd_s = st_s
    return (y_prompt, y_sample, k_p, v_p, lf_p, s5r_p, s5i_p, conv_p, ssd_p,
            k_s, v_s, lf_s, s5r_s, s5i_s, conv_s, ssd_s)
```

```python
import functools
import math

import jax
import jax.numpy as jnp
import numpy as np
from jax import lax
from jax.experimental import pallas as pl
from jax.experimental.pallas import tpu as pltpu

F32 = jnp.float32
BF16 = jnp.bfloat16
HIGHEST = lax.Precision.HIGHEST

D_MODEL = 1024
N_BRANCH = 3
WIDTH = 512
S5_GROUPS = 32
S5_GROUP = 16
S5_STATE = 64
S5_LANES = S5_GROUPS * S5_STATE
HEADS = 8
HEAD_DIM = 64
SSD_STATE = 128
SSD_GROUPS = 2
SSD_CONV = 4
SSD_CHUNK = 128
CONV_DIM = 1024
D_FF = 2816
N_EXPERTS = 8
PAGE = 128
ALPHA = 4.0 ** 0.25
LN_EPS = 1e-5
RMS_EPS = 1e-5
NEG = -1e30
LANES = 128
VMEM_LIMIT = 56 * 1024 * 1024

NT_DIMS = (((1,), (1,)), ((), ()))


def _params(*sem):
    return pltpu.CompilerParams(dimension_semantics=sem, vmem_limit_bytes=VMEM_LIMIT)


def _const_spec(shape):
    nd = len(shape)
    return pl.BlockSpec(shape, lambda *_: (0,) * nd, pipeline_mode=pl.Buffered(1))


def _layer_norm(x, g, b):
    mu = jnp.mean(x, axis=-1, keepdims=True)
    xc = x - mu
    var = jnp.mean(xc * xc, axis=-1, keepdims=True)
    return xc * lax.rsqrt(var + LN_EPS) * g + b


def _sigmoid(x):
    return 1.0 / (1.0 + jnp.exp(-x))


def _silu(x):
    return x * _sigmoid(x)


SEG_U, SEG_Q, SEG_K, SEG_V, SEG_Z, SEG_X, SEG_END = 0, 512, 1024, 1536, 2048, 2560, 3584


def _in_proj_kernel(x_ref, wm_ref, ws_ref, bs_ref, u_ref, ub_ref, qb_ref, k_ref, v_ref, kb_ref,
                    vb_ref, z_ref, xbc_ref, small_ref, smallt_ref, logf_ref):
    xb = x_ref[...].astype(BF16)

    def seg(a, b):
        return jnp.dot(xb, wm_ref[:, a:b], preferred_element_type=F32)

    u = seg(SEG_U, SEG_Q)
    u_ref[...] = u
    ub_ref[...] = u.astype(BF16)
    qb_ref[...] = (seg(SEG_Q, SEG_K) * (HEAD_DIM ** -0.5)).astype(BF16)
    k = seg(SEG_K, SEG_V)
    k_ref[...] = k
    kb_ref[...] = k.astype(BF16)
    v = seg(SEG_V, SEG_Z)
    v_ref[...] = v
    vb_ref[...] = v.astype(BF16)
    z_ref[...] = seg(SEG_Z, SEG_X)
    xbc_ref[...] = seg(SEG_X, SEG_END)
    s = jnp.dot(xb, ws_ref[...], preferred_element_type=F32) + bs_ref[...]
    t = jnp.log1p(jnp.exp(-jnp.abs(s)))
    lane = lax.broadcasted_iota(jnp.int32, s.shape, 1)
    sm = jnp.where(lane < HEADS, jnp.minimum(s, 0.0) - t,
                   jnp.where(lane < 2 * HEADS, jnp.maximum(s, 0.0) + t, 0.0))
    small_ref[...] = sm
    smallt_ref[...] = sm.T[:2 * HEADS, :]
    logf_ref[...] = sm[:, :HEADS]


def _in_proj(x2, wm, ws, bs, tm):
    t = x2.shape[0]
    row = lambda w: pl.BlockSpec((tm, w), lambda i: (i, 0))
    out_shape = (
        jax.ShapeDtypeStruct((t, WIDTH), F32),
        jax.ShapeDtypeStruct((t, WIDTH), BF16),
        jax.ShapeDtypeStruct((t, WIDTH), BF16),
        jax.ShapeDtypeStruct((t, WIDTH), F32),
        jax.ShapeDtypeStruct((t, WIDTH), F32),
        jax.ShapeDtypeStruct((t, WIDTH), BF16),
        jax.ShapeDtypeStruct((t, WIDTH), BF16),
        jax.ShapeDtypeStruct((t, WIDTH), F32),
        jax.ShapeDtypeStruct((t, CONV_DIM), F32),
        jax.ShapeDtypeStruct((t, LANES), F32),
        jax.ShapeDtypeStruct((2 * HEADS, t), F32),
        jax.ShapeDtypeStruct((t, HEADS), F32),
    )
    out_specs = (row(WIDTH), row(WIDTH), row(WIDTH), row(WIDTH), row(WIDTH), row(WIDTH), row(WIDTH),
                 row(WIDTH), row(CONV_DIM), row(LANES),
                 pl.BlockSpec((2 * HEADS, tm), lambda i: (0, i)), row(HEADS))
    return pl.pallas_call(
        _in_proj_kernel,
        grid=(t // tm,),
        in_specs=[row(D_MODEL), _const_spec(wm.shape), _const_spec(ws.shape), _const_spec(bs.shape)],
        out_specs=out_specs,
        out_shape=out_shape,
        compiler_params=_params("parallel"),
        name="in_proj",
    )(x2, wm, ws, bs)


def _cumsum_lanes(x):
    n = x.shape[-1]
    lane = lax.broadcasted_iota(jnp.int32, x.shape, x.ndim - 1)
    s = 1
    while s < n:
        x = x + jnp.where(lane >= s, pltpu.roll(x, s, axis=x.ndim - 1), 0.0)
        s *= 2
    return x


def _cumsum_kernel(x_ref, o_ref):
    o_ref[...] = _cumsum_lanes(x_ref[...])


def _cumsum_rows(x):
    return pl.pallas_call(
        _cumsum_kernel,
        out_shape=jax.ShapeDtypeStruct(x.shape, F32),
        name="logf_cumsum",
    )(x)


def _fox_prompt_kernel(it_ref, jt_ref, q_ref, k_ref, v_ref, c_ref, o_ref, m_sc, l_sc, acc_sc, *, tq, tk):
    t = pl.program_id(2)
    i = it_ref[t]
    j = jt_ref[t]

    @pl.when(j == 0)
    def _():
        m_sc[...] = jnp.full(m_sc.shape, NEG, F32)
        l_sc[...] = jnp.zeros(l_sc.shape, F32)
        acc_sc[...] = jnp.zeros(acc_sc.shape, F32)

    def step(masked):
        q2 = q_ref[0]
        k2 = k_ref[0]
        v2 = v_ref[0]
        lane_head = lax.broadcasted_iota(jnp.int32, q2.shape, 1) // HEAD_DIM
        if masked:
            qpos = i * tq + lax.broadcasted_iota(jnp.int32, (tq, tk), 0)
            kpos = j * tk + lax.broadcasted_iota(jnp.int32, (tq, tk), 1)
            causal = kpos <= qpos
        for hh in range(2):
            qh = jnp.where(lane_head == hh, q2, jnp.zeros_like(q2))
            s = lax.dot_general(qh, k2, NT_DIMS, preferred_element_type=F32)
            s = s - c_ref[0, 0, hh:hh + 1, :]
            if masked:
                s = jnp.where(causal, s, NEG)
            m_prev = m_sc[hh]
            m_new = jnp.maximum(m_prev, jnp.max(s, axis=-1, keepdims=True))
            alpha = jnp.exp(m_prev - m_new)
            p = jnp.exp(s - m_new)
            l_sc[hh] = alpha * l_sc[hh] + jnp.sum(p, axis=-1, keepdims=True)
            acc_sc[hh] = alpha * acc_sc[hh] + jnp.dot(p.astype(BF16), v2, preferred_element_type=F32)
            m_sc[hh] = m_new

    @pl.when(j < i)
    def _():
        step(False)

    @pl.when(j == i)
    def _():
        step(True)
        lane_head = lax.broadcasted_iota(jnp.int32, (tq, LANES), 1) // HEAD_DIM
        o = jnp.where(lane_head == 0, acc_sc[0] / l_sc[0], acc_sc[1] / l_sc[1])
        o_ref[0] = o.astype(o_ref.dtype)


def _fox_prompt(qb, kb, vb, c4, tq):
    bsz, seq, _ = qb.shape
    nq = seq // tq
    it = np.array([i for i in range(nq) for j in range(i + 1)], np.int32)
    jt = np.array([j for i in range(nq) for j in range(i + 1)], np.int32)
    grid_spec = pltpu.PrefetchScalarGridSpec(
        num_scalar_prefetch=2,
        grid=(bsz, HEADS // 2, len(it)),
        in_specs=[
            pl.BlockSpec((1, tq, LANES), lambda b, h, t, it, jt: (b, it[t], h)),
            pl.BlockSpec((1, tq, LANES), lambda b, h, t, it, jt: (b, jt[t], h)),
            pl.BlockSpec((1, tq, LANES), lambda b, h, t, it, jt: (b, jt[t], h)),
            pl.BlockSpec((1, 1, 2, tq), lambda b, h, t, it, jt: (b, h, 0, jt[t])),
        ],
        out_specs=pl.BlockSpec((1, tq, LANES), lambda b, h, t, it, jt: (b, it[t], h)),
        scratch_shapes=[pltpu.VMEM((2, tq, 1), F32), pltpu.VMEM((2, tq, 1), F32),
                        pltpu.VMEM((2, tq, LANES), F32)],
    )
    return pl.pallas_call(
        functools.partial(_fox_prompt_kernel, tq=tq, tk=tq),
        grid_spec=grid_spec,
        out_shape=jax.ShapeDtypeStruct((bsz, seq, WIDTH), BF16),
        compiler_params=_params("parallel", "parallel", "arbitrary"),
        name="fox_prompt",
    )(jnp.asarray(it), jnp.asarray(jt), qb, kb, vb, c4)


def _fox_sample_kernel(pt_ref, q_ref, kn_ref, vn_ref, lfn_ref, *refs, n_pages, tq):
    k_refs = refs[:n_pages]
    v_refs = refs[n_pages:2 * n_pages]
    lf_refs = refs[2 * n_pages:3 * n_pages]
    o_ref = refs[3 * n_pages]
    rows = tq * HEADS
    r_i = lax.broadcasted_iota(jnp.int32, (rows, tq), 0)
    t_i = lax.broadcasted_iota(jnp.int32, (rows, tq), 1)
    rep = (r_i // HEADS == t_i).astype(F32)
    q = q_ref[0].astype(F32)
    qe = jnp.dot(rep, q, precision=HIGHEST, preferred_element_type=F32)
    r_w = lax.broadcasted_iota(jnp.int32, (rows, WIDTH), 0)
    l_w = lax.broadcasted_iota(jnp.int32, (rows, WIDTH), 1)
    head_mask = (l_w // HEAD_DIM) == (r_w % HEADS)
    qbd = jnp.where(head_mask, qe, 0.0).astype(BF16)

    scores = [None] * n_pages
    carry = jnp.zeros((HEADS, 1), F32)
    for p in range(n_pages - 1, -1, -1):
        cs = _cumsum_lanes(lf_refs[p][0])
        tot = cs[:, PAGE - 1:PAGE]
        dec = (carry + tot) - cs
        carry = carry + tot
        s = lax.dot_general(qbd, k_refs[p][0].astype(BF16), NT_DIMS, preferred_element_type=F32)
        scores[p] = s + jnp.concatenate([dec] * tq, axis=0)
    pad = jnp.zeros((PAGE - tq, WIDTH), F32)
    kn = jnp.concatenate([kn_ref[0], pad], axis=0).astype(BF16)
    vn = jnp.concatenate([vn_ref[0], pad], axis=0).astype(BF16)
    m_i = lax.broadcasted_iota(jnp.int32, (PAGE, PAGE), 0)
    c_i = lax.broadcasted_iota(jnp.int32, (PAGE, PAGE), 1)
    cn = jnp.dot(lfn_ref[0], (m_i <= c_i).astype(F32), precision=HIGHEST,
                 preferred_element_type=F32)
    r_p = lax.broadcasted_iota(jnp.int32, (rows, PAGE), 0)
    c_p = lax.broadcasted_iota(jnp.int32, (rows, PAGE), 1)
    s_new = lax.dot_general(qbd, kn, NT_DIMS, preferred_element_type=F32)
    s_new = jnp.where(c_p <= r_p // HEADS, s_new - jnp.concatenate([cn] * tq, axis=0), NEG)

    m = s_new
    for p in range(n_pages):
        m = jnp.maximum(m, scores[p])
    m = jnp.max(m, axis=-1, keepdims=True)
    pn = jnp.exp(s_new - m)
    l = jnp.sum(pn, axis=-1, keepdims=True)
    o = jnp.dot(pn.astype(BF16), vn, preferred_element_type=F32)
    for p in range(n_pages):
        pp = jnp.exp(scores[p] - m)
        l = l + jnp.sum(pp, axis=-1, keepdims=True)
        o = o + jnp.dot(pp.astype(BF16), v_refs[p][0].astype(BF16), preferred_element_type=F32)
    om = jnp.where(head_mask, o / l, 0.0)
    rep_t = (lax.broadcasted_iota(jnp.int32, (tq, rows), 1) // HEADS
             == lax.broadcasted_iota(jnp.int32, (tq, rows), 0)).astype(F32)
    o_ref[0] = jnp.dot(rep_t, om, precision=HIGHEST, preferred_element_type=F32).astype(o_ref.dtype)


def _fox_sample(qb, k_new, v_new, lfn_t, cache_k_l, cache_v_l, cache_lf_t, page_table):
    dbsz, tq, _ = qb.shape
    n_pages = page_table.shape[1]
    pt = page_table.reshape(-1).astype(jnp.int32)

    def page_spec(p, shape):
        return pl.BlockSpec(shape, lambda b, pt: (pt[b * n_pages + p], 0, 0))

    seq_spec = lambda shape: pl.BlockSpec(shape, lambda b, pt: (b, 0, 0))
    in_specs = [seq_spec((1, tq, WIDTH)), seq_spec((1, tq, WIDTH)), seq_spec((1, tq, WIDTH)),
                seq_spec((1, HEADS, LANES))]
    in_specs += [page_spec(p, (1, PAGE, WIDTH)) for p in range(n_pages)]
    in_specs += [page_spec(p, (1, PAGE, WIDTH)) for p in range(n_pages)]
    in_specs += [page_spec(p, (1, HEADS, PAGE)) for p in range(n_pages)]
    grid_spec = pltpu.PrefetchScalarGridSpec(
        num_scalar_prefetch=1, grid=(dbsz,), in_specs=in_specs,
        out_specs=pl.BlockSpec((1, tq, WIDTH), lambda b, pt: (b, 0, 0)))
    return pl.pallas_call(
        functools.partial(_fox_sample_kernel, n_pages=n_pages, tq=tq),
        grid_spec=grid_spec,
        out_shape=jax.ShapeDtypeStruct((dbsz, tq, WIDTH), F32),
        compiler_params=_params("parallel"),
        name="fox_sample",
    )(pt, qb, k_new, v_new, lfn_t, *([cache_k_l] * n_pages), *([cache_v_l] * n_pages),
      *([cache_lf_t] * n_pages))


def _s5_tables(lam_re, lam_im, log_dt, b_re, b_im, c_re, c_im, q):
    hp = dict(precision=HIGHEST)
    dt = jnp.exp(log_dt)[:, None]
    mag = jnp.exp(lam_re * dt)
    ab_re = mag * jnp.cos(lam_im * dt)
    ab_im = mag * jnp.sin(lam_im * dt)
    nr = ab_re - 1.0
    den = lam_re * lam_re + lam_im * lam_im
    q_re = (nr * lam_re + ab_im * lam_im) / den
    q_im = (ab_im * lam_re - nr * lam_im) / den
    bb_re = q_re[..., None] * b_re - q_im[..., None] * b_im
    bb_im = q_re[..., None] * b_im + q_im[..., None] * b_re
    jj = jnp.arange(q + 1, dtype=F32)[:, None, None]
    pmag = jnp.exp(lam_re * dt * jj)
    p_re = pmag * jnp.cos(lam_im * dt * jj)
    p_im = pmag * jnp.sin(lam_im * dt * jj)
    ab_b_re = p_re[:q, :, :, None] * bb_re - p_im[:q, :, :, None] * bb_im
    ab_b_im = p_re[:q, :, :, None] * bb_im + p_im[:q, :, :, None] * bb_re
    kern = (jnp.einsum('gcn,jgnd->jgcd', c_re, ab_b_re, **hp)
            - jnp.einsum('gcn,jgnd->jgcd', c_im, ab_b_im, **hp))
    eye16 = jnp.eye(16, dtype=F32)
    kt = kern.transpose(0, 1, 3, 2).reshape(q, 2, 16, S5_GROUP, S5_GROUP)
    w_toep = jnp.einsum('jxgab,gh->jxgahb', kt, eye16).reshape(q, 2, 256, 256).astype(BF16)
    eye_g = jnp.eye(S5_GROUPS, dtype=F32)

    def in_map(bb):
        full = jnp.einsum('gnc,gh->gchn', bb, eye_g).reshape(WIDTH, S5_LANES)
        return jnp.stack([full[128 * (n // 2):128 * (n // 2) + 128, 256 * n:256 * n + 256]
                          for n in range(8)]).astype(BF16)

    def out_map(c):
        full = jnp.einsum('gcn,gh->gnhc', c, eye_g).reshape(S5_LANES, WIDTH)
        return jnp.stack([full[512 * m:512 * m + 512, 128 * m:128 * m + 128]
                          for m in range(4)]).astype(BF16)

    rows = ((q + 1 + 7) // 8) * 8
    padp = lambda p: jnp.pad(p.reshape(q + 1, S5_LANES), ((0, rows - q - 1), (0, 0)))
    return dict(w_toep=w_toep, wb_re=in_map(bb_re), wb_im=in_map(bb_im), wc_re=out_map(c_re),
                wc_im=out_map(-c_im), p_re=padp(p_re), p_im=padp(p_im))


def _s5_toeplitz_kernel(u_ref, w_ref, y_ref, *, q):
    for s_out in range(q):
        for half in range(2):
            acc = None
            for j in range(s_out + 1):
                a = (s_out - j) * WIDTH + half * 256
                d = jnp.dot(u_ref[:, a:a + 256], w_ref[j, half], preferred_element_type=F32)
                acc = d if acc is None else acc + d
            o = s_out * WIDTH + half * 256
            y_ref[:, o:o + 256] = acc


def _s5_toeplitz(u2b, w_toep, q, rt):
    nc = u2b.shape[0]
    return pl.pallas_call(
        functools.partial(_s5_toeplitz_kernel, q=q),
        grid=(nc // rt,),
        in_specs=[pl.BlockSpec((rt, q * WIDTH), lambda i: (i, 0)), _const_spec(w_toep.shape)],
        out_specs=pl.BlockSpec((rt, q * WIDTH), lambda i: (i, 0)),
        out_shape=jax.ShapeDtypeStruct((nc, q * WIDTH), F32),
        compiler_params=_params("parallel"),
        name="s5_toeplitz",
    )(u2b, w_toep)


def _s5_scan_kernel(u_ref, yi_ref, h0r_ref, h0i_ref, wbr_ref, wbi_ref, wcr_ref, wci_ref, pr_ref, pi_ref,
                    d_ref, wg_ref, bg_ref, y_ref, htr_ref, hti_ref,
                    bur_sc, bui_sc, sr_sc, si_sc, xcr_sc, xci_sc, hr_sc, hi_sc, *, q, nct, carry):
    tile = pl.program_id(1)
    u = u_ref[0]
    ub = u.astype(BF16)
    for n in range(8):
        a = 128 * (n // 2)
        br = jnp.dot(ub[:, a:a + 128], wbr_ref[n], preferred_element_type=F32)
        bi = jnp.dot(ub[:, a:a + 128], wbi_ref[n], preferred_element_type=F32)
        for half in range(2):
            bur_sc[2 * n + half] = br[:, LANES * half:LANES * (half + 1)]
            bui_sc[2 * n + half] = bi[:, LANES * half:LANES * (half + 1)]
    s_re_cols, s_im_cols = [], []
    for k in range(S5_LANES // LANES):
        lanes = slice(LANES * k, LANES * (k + 1))
        sr_k = jnp.zeros((nct, LANES), F32)
        si_k = jnp.zeros((nct, LANES), F32)
        for s in range(q):
            br = bur_sc[k, pl.ds(s, nct, stride=q), :]
            bi = bui_sc[k, pl.ds(s, nct, stride=q), :]
            pr = pr_ref[q - 1 - s:q - s, lanes]
            pi = pi_ref[q - 1 - s:q - s, lanes]
            sr_k = sr_k + pr * br - pi * bi
            si_k = si_k + pr * bi + pi * br
        s_re_cols.append(sr_k)
        s_im_cols.append(si_k)
    s_re = jnp.concatenate(s_re_cols, axis=-1)
    s_im = jnp.concatenate(s_im_cols, axis=-1)
    aq_r = pr_ref[q:q + 1, :]
    aq_i = pi_ref[q:q + 1, :]
    p1r = pr_ref[1:q + 1, :]
    p1i = pi_ref[1:q + 1, :]

    if carry:
        sr_sc[...] = s_re
        si_sc[...] = s_im

        @pl.when(tile == 0)
        def _():
            hr_sc[...] = h0r_ref[0]
            hi_sc[...] = h0i_ref[0]

        def body(c, h):
            hr, hi = h
            row0 = pl.multiple_of(c * q, q)
            xcr_sc[pl.ds(row0, q), :] = p1r * hr - p1i * hi
            xci_sc[pl.ds(row0, q), :] = p1r * hi + p1i * hr
            sr = sr_sc[pl.ds(c, 1), :]
            si = si_sc[pl.ds(c, 1), :]
            return aq_r * hr - aq_i * hi + sr, aq_r * hi + aq_i * hr + si

        hr, hi = lax.fori_loop(0, nct, body, (hr_sc[...], hi_sc[...]))
        hr_sc[...] = hr
        hi_sc[...] = hi

        @pl.when(tile == pl.num_programs(1) - 1)
        def _():
            htr_ref[0] = hr
            hti_ref[0] = hi
    else:
        def body(c, _):
            hr = h0r_ref[pl.ds(c, 1), :]
            hi = h0i_ref[pl.ds(c, 1), :]
            row0 = pl.multiple_of(c * q, q)
            xcr_sc[pl.ds(row0, q), :] = p1r * hr - p1i * hi
            xci_sc[pl.ds(row0, q), :] = p1r * hi + p1i * hr
            return 0

        lax.fori_loop(0, nct, body, 0)
        h0r = h0r_ref[...]
        h0i = h0i_ref[...]
        htr_ref[...] = aq_r * h0r - aq_i * h0i + s_re
        hti_ref[...] = aq_r * h0i + aq_i * h0r + s_im

    cols = []
    for m in range(4):
        xr = xcr_sc[:, 512 * m:512 * m + 512].astype(BF16)
        xi = xci_sc[:, 512 * m:512 * m + 512].astype(BF16)
        cols.append(jnp.dot(xr, wcr_ref[m], preferred_element_type=F32)
                    + jnp.dot(xi, wci_ref[m], preferred_element_type=F32))
    y = yi_ref[0] + jnp.concatenate(cols, axis=-1) + d_ref[...] * u
    y = jax.nn.gelu(y)
    g = jnp.dot(y.astype(BF16), wg_ref[...], preferred_element_type=F32) + bg_ref[...]
    y_ref[0] = (y * _sigmoid(g)).astype(y_ref.dtype)


def _s5_scan(u3, yi3, h0_re, h0_im, tab, d_row, w_glu, b_glu, q, tm, carry):
    bsz, seq, _ = u3.shape
    nct = tm // q
    n_tiles = seq // tm
    tok = pl.BlockSpec((1, tm, WIDTH), lambda b, t: (b, t, 0))
    if carry:
        h_spec = pl.BlockSpec((1, 1, S5_LANES), lambda b, t: (b, 0, 0))
        h_shape = jax.ShapeDtypeStruct((bsz, 1, S5_LANES), F32)
    else:
        h_spec = pl.BlockSpec((nct, S5_LANES), lambda b, t: (t, 0))
        h_shape = jax.ShapeDtypeStruct((seq // q, S5_LANES), F32)
    consts = [tab['wb_re'], tab['wb_im'], tab['wc_re'], tab['wc_im'], tab['p_re'], tab['p_im'],
              d_row, w_glu, b_glu]
    return pl.pallas_call(
        functools.partial(_s5_scan_kernel, q=q, nct=nct, carry=carry),
        grid=(bsz, n_tiles),
        in_specs=[tok, tok, h_spec, h_spec] + [_const_spec(c.shape) for c in consts],
        out_specs=(tok, h_spec, h_spec),
        out_shape=(jax.ShapeDtypeStruct((bsz, seq, WIDTH), BF16), h_shape, h_shape),
        scratch_shapes=[pltpu.VMEM((S5_LANES // LANES, tm, LANES), F32),
                        pltpu.VMEM((S5_LANES // LANES, tm, LANES), F32),
                        pltpu.VMEM((nct, S5_LANES), F32), pltpu.VMEM((nct, S5_LANES), F32),
                        pltpu.VMEM((tm, S5_LANES), F32), pltpu.VMEM((tm, S5_LANES), F32),
                        pltpu.VMEM((1, S5_LANES), F32), pltpu.VMEM((1, S5_LANES), F32)],
        compiler_params=_params("parallel", "arbitrary"),
        name="s5_scan",
    )(u3, yi3, h0_re, h0_im, *consts)


def _ssd_kernel(z_ref, xbc_ref, sm_ref, smt_ref, conv0_ref, h0_ref, cw_ref, cb_ref, alog_row_ref,
                alog_col_ref, e_ref, d_ref, nw_ref, y_ref, convt_ref, ht_ref,
                xp_sc, st_sc, zp_sc, *, nv):
    q = SSD_CHUNK
    c = pl.program_id(1)
    last = c == pl.num_programs(1) - 1

    @pl.when(c == 0)
    def _():
        st_sc[...] = h0_ref[0]
        xp_sc[...] = jnp.zeros(xp_sc.shape, F32)
        xp_sc[8 - (SSD_CONV - 1):8, :] = conv0_ref[0]
        if nv < q:
            zp_sc[...] = jnp.zeros(zp_sc.shape, F32)

    xp_sc[8:8 + nv, :] = xbc_ref[0]
    acc = None
    for j in range(SSD_CONV):
        o = 8 - (SSD_CONV - 1) + j
        term = xp_sc[o:o + q, :] * cw_ref[j:j + 1, :]
        acc = term if acc is None else acc + term
    xc = _silu(acc + cb_ref[...])
    tail = xp_sc[8 + nv - (SSD_CONV - 1):8 + nv, :]

    @pl.when(last)
    def _():
        convt_ref[0] = tail

    xp_sc[8 - (SSD_CONV - 1):8, :] = tail
    if nv < q:
        zp_sc[0:nv, :] = z_ref[0]
        z = zp_sc[...]
    else:
        z = z_ref[0]

    xs = xc[:, :WIDTH]
    bm = xc[:, WIDTH:WIDTH + SSD_GROUPS * SSD_STATE]
    cm = xc[:, WIDTH + SSD_GROUPS * SSD_STATE:]

    sm = sm_ref[0]
    smt = smt_ref[0]
    lane = lax.broadcasted_iota(jnp.int32, (1, LANES), 1)
    a_row = jnp.where((lane >= HEADS) & (lane < 2 * HEADS), -jnp.exp(alog_row_ref[...]), 0.0)
    rowi = lax.broadcasted_iota(jnp.int32, (2 * HEADS, 1), 0)
    a_col = jnp.where(rowi >= HEADS, -jnp.exp(alog_col_ref[...]), 0.0)
    adt = sm * a_row
    adt_t = smt * a_col
    r_i = lax.broadcasted_iota(jnp.int32, (q, q), 0)
    c_i = lax.broadcasted_iota(jnp.int32, (q, q), 1)
    tri = r_i >= c_i
    acum = jnp.dot(tri.astype(F32), adt, precision=HIGHEST, preferred_element_type=F32)
    acum_t = jnp.dot(adt_t, (r_i <= c_i).astype(F32), precision=HIGHEST, preferred_element_type=F32)
    atot = acum[q - 1:q, :]
    expand = lambda x: jnp.dot(x, e_ref[...], precision=HIGHEST, preferred_element_type=F32)
    dt_e = expand(sm)
    eac_e = expand(jnp.exp(acum))
    dec_e = expand(jnp.exp(atot - acum))
    xd = xs * dt_e
    xdd = xd * dec_e
    lane_half = lax.broadcasted_iota(jnp.int32, (q, LANES), 1) // HEAD_DIM

    y_cols = []
    for g in range(SSD_GROUPS):
        bg = bm[:, SSD_STATE * g:SSD_STATE * (g + 1)].astype(BF16)
        cg = cm[:, SSD_STATE * g:SSD_STATE * (g + 1)].astype(BF16)
        cb = lax.dot_general(cg, bg, NT_DIMS, preferred_element_type=F32)
        hpg = HEADS // SSD_GROUPS
        st_g = st_sc[hpg * HEAD_DIM * g:hpg * HEAD_DIM * (g + 1), :]
        y_off = lax.dot_general(cg, st_g.astype(BF16), NT_DIMS, preferred_element_type=F32)
        for pair in range(hpg // 2):
            ys = []
            for hh in range(2):
                h = hpg * g + 2 * pair + hh
                diff = acum[:, HEADS + h:HEADS + h + 1] - acum_t[HEADS + h:HEADS + h + 1, :]
                lmat = jnp.exp(jnp.where(tri, diff, NEG))
                w = (cb * lmat).astype(BF16)
                lo = LANES * (2 * g + pair)
                ys.append(jnp.dot(w, xd[:, lo:lo + LANES].astype(BF16), preferred_element_type=F32))
            y_cols.append(jnp.where(lane_half == 0, ys[0], ys[1]))
        y_cols[-2] = y_cols[-2] + y_off[:, :LANES] * eac_e[:, 256 * g:256 * g + LANES]
        y_cols[-1] = y_cols[-1] + y_off[:, LANES:] * eac_e[:, 256 * g + LANES:256 * (g + 1)]
        contrib = jnp.dot(xdd[:, 256 * g:256 * (g + 1)].T.astype(BF16), bg, preferred_element_type=F32)
        for hl in range(hpg):
            h = hpg * g + hl
            sl = slice(HEAD_DIM * h, HEAD_DIM * (h + 1))
            dec_h = jnp.exp(acum_t[HEADS + h:HEADS + h + 1, q - 1:q])
            st_sc[sl, :] = st_sc[sl, :] * dec_h + contrib[HEAD_DIM * hl:HEAD_DIM * (hl + 1), :]

    y = jnp.concatenate(y_cols, axis=-1) + d_ref[...] * xs
    y = y * _silu(z)
    y = y * lax.rsqrt(jnp.mean(y * y, axis=-1, keepdims=True) + RMS_EPS) * nw_ref[...]
    y_ref[0] = y[:nv].astype(y_ref.dtype)

    @pl.when(last)
    def _():
        ht_ref[0] = st_sc[...]


def _ssd(z3, xbc3, sm3, smt3, conv0, h0, cw, cb, alog_row, alog_col, e_mat, d_row, nw, nv):
    bsz, seq, _ = z3.shape
    nchunk = seq // nv
    q = SSD_CHUNK
    consts = [cw, cb, alog_row, alog_col, e_mat, d_row, nw]
    return pl.pallas_call(
        functools.partial(_ssd_kernel, nv=nv),
        grid=(bsz, nchunk),
        in_specs=[pl.BlockSpec((1, nv, WIDTH), lambda b, c: (b, c, 0)),
                  pl.BlockSpec((1, nv, CONV_DIM), lambda b, c: (b, c, 0)),
                  pl.BlockSpec((1, q, LANES), lambda b, c: (b, c, 0)),
                  pl.BlockSpec((1, 2 * HEADS, q), lambda b, c: (b, 0, c)),
                  pl.BlockSpec((1, SSD_CONV - 1, CONV_DIM), lambda b, c: (b, 0, 0)),
                  pl.BlockSpec((1, WIDTH, SSD_STATE), lambda b, c: (b, 0, 0))]
                 + [_const_spec(x.shape) for x in consts],
        out_specs=(pl.BlockSpec((1, nv, WIDTH), lambda b, c: (b, c, 0)),
                   pl.BlockSpec((1, SSD_CONV - 1, CONV_DIM), lambda b, c: (b, 0, 0)),
                   pl.BlockSpec((1, WIDTH, SSD_STATE), lambda b, c: (b, 0, 0))),
        out_shape=(jax.ShapeDtypeStruct((bsz, seq, WIDTH), BF16 if nv % 16 == 0 else F32),
                   jax.ShapeDtypeStruct((bsz, SSD_CONV - 1, CONV_DIM), F32),
                   jax.ShapeDtypeStruct((bsz, WIDTH, SSD_STATE), F32)),
        scratch_shapes=[pltpu.VMEM((8 + q, CONV_DIM), F32), pltpu.VMEM((WIDTH, SSD_STATE), F32),
                        pltpu.VMEM((q, WIDTH), F32)],
        compiler_params=_params("parallel", "arbitrary"),
        name="ssd",
    )(z3, xbc3, sm3, smt3, conv0, h0, *consts)


def _merge_kernel(x_ref, ys5_ref, yfox_ref, yssd_ref, wg_ref, bg_ref, ws5_ref, wfox_ref, wssd_ref,
                  wo_ref, g_ref, b_ref, o_ref):
    x = x_ref[...]
    xb = x.astype(BF16)
    merged = None
    for br, (y_ref, w_ref) in enumerate(((ys5_ref, ws5_ref), (yfox_ref, wfox_ref), (yssd_ref, wssd_ref))):
        lo = br * D_MODEL
        gate = _sigmoid(jnp.dot(xb, wg_ref[:, lo:lo + D_MODEL], preferred_element_type=F32)
                        + bg_ref[:, lo:lo + D_MODEL])
        term = gate * jnp.dot(y_ref[...].astype(BF16), w_ref[...], preferred_element_type=F32)
        merged = term if merged is None else merged + term
    out = jnp.dot(merged.astype(BF16), wo_ref[...], preferred_element_type=F32)
    o_ref[...] = _layer_norm(ALPHA * x + out, g_ref[...], b_ref[...])


def _merge(x2, ys5, yfox, yssd, wg, bg, ws5, wfox, wssd, wo, g, b, tm):
    t = x2.shape[0]
    row = lambda w: pl.BlockSpec((tm, w), lambda i: (i, 0))
    consts = [wg, bg, ws5, wfox, wssd, wo, g, b]
    return pl.pallas_call(
        _merge_kernel,
        grid=(t // tm,),
        in_specs=[row(D_MODEL), row(WIDTH), row(WIDTH), row(WIDTH)] + [_const_spec(c.shape) for c in consts],
        out_specs=row(D_MODEL),
        out_shape=jax.ShapeDtypeStruct((t, D_MODEL), F32),
        compiler_params=_params("parallel"),
        name="merge",
    )(x2, ys5, yfox, yssd, *consts)


FF_CHUNK = 256


def _ffn_kernel(x_ref, wg_ref, wu_ref, wd_ref, g_ref, b_ref, o_ref):
    x = x_ref[...]
    xb = x.astype(BF16)
    acc = None
    for c in range(0, D_FF, FF_CHUNK):
        gt = jnp.dot(xb, wg_ref[:, c:c + FF_CHUNK], preferred_element_type=F32)
        up = jnp.dot(xb, wu_ref[:, c:c + FF_CHUNK], preferred_element_type=F32)
        h = (_silu(gt) * up).astype(BF16)
        d = jnp.dot(h, wd_ref[c:c + FF_CHUNK, :], preferred_element_type=F32)
        acc = d if acc is None else acc + d
    o_ref[...] = _layer_norm(ALPHA * x + acc, g_ref[...], b_ref[...])


def _ffn(x2, wg, wu, wd, g, b, tm):
    t = x2.shape[0]
    row = pl.BlockSpec((tm, D_MODEL), lambda i: (i, 0))
    consts = [wg, wu, wd, g, b]
    return pl.pallas_call(
        _ffn_kernel,
        grid=(t // tm,),
        in_specs=[row] + [_const_spec(c.shape) for c in consts],
        out_specs=row,
        out_shape=jax.ShapeDtypeStruct((t, D_MODEL), F32),
        compiler_params=_params("parallel"),
        name="ffn",
    )(x2, *consts)


def _moe_kernel(x_ref, wr_ref, br_ref, wg_ref, wu_ref, wd_ref, g_ref, b_ref, o_ref, gate_sc, acc_sc):
    e = pl.program_id(1)
    x = x_ref[...]
    xb = x.astype(BF16)
    lane = lax.broadcasted_iota(jnp.int32, (x.shape[0], LANES), 1)

    @pl.when(e == 0)
    def _():
        logits = jnp.dot(xb, wr_ref[...], preferred_element_type=F32) + br_ref[...]
        logits = jnp.where(lane < N_EXPERTS, logits, NEG)
        m1 = jnp.max(logits, axis=-1, keepdims=True)
        i1 = jnp.min(jnp.where(logits == m1, lane, LANES), axis=-1, keepdims=True)
        rest = jnp.where(lane == i1, NEG, logits)
        m2 = jnp.max(rest, axis=-1, keepdims=True)
        i2 = jnp.min(jnp.where(rest == m2, lane, LANES), axis=-1, keepdims=True)
        e2 = jnp.exp(m2 - m1)
        w1 = 1.0 / (1.0 + e2)
        w2 = e2 / (1.0 + e2)
        gate_sc[...] = jnp.where(lane == i1, w1, jnp.where(lane == i2, w2, 0.0))
        acc_sc[...] = jnp.zeros(acc_sc.shape, F32)

    gate_e = jnp.sum(jnp.where(lane == e, gate_sc[...], 0.0), axis=-1, keepdims=True)
    gt = jnp.dot(xb, wg_ref[0], preferred_element_type=F32)
    up = jnp.dot(xb, wu_ref[0], preferred_element_type=F32)
    h = (_silu(gt) * up).astype(BF16)
    acc_sc[...] += gate_e * jnp.dot(h, wd_ref[0], preferred_element_type=F32)

    @pl.when(e == N_EXPERTS - 1)
    def _():
        o_ref[...] = _layer_norm(ALPHA * x + acc_sc[...], g_ref[...], b_ref[...])


def _moe(x2, wr, br, wg, wu, wd, g, b, tm):
    t = x2.shape[0]
    row = pl.BlockSpec((tm, D_MODEL), lambda i, e: (i, 0))
    wspec = pl.BlockSpec((1, D_MODEL, D_MODEL), lambda i, e: (e, 0, 0))
    return pl.pallas_call(
        _moe_kernel,
        grid=(t // tm, N_EXPERTS),
        in_specs=[row, _const_spec(wr.shape), _const_spec(br.shape), wspec, wspec, wspec,
                  _const_spec(g.shape), _const_spec(b.shape)],
        out_specs=row,
        out_shape=jax.ShapeDtypeStruct((t, D_MODEL), F32),
        scratch_shapes=[pltpu.VMEM((tm, LANES), F32), pltpu.VMEM((tm, D_MODEL), F32)],
        compiler_params=_params("parallel", "arbitrary"),
        name="moe",
    )(x2, wr, br, wg, wu, wd, g, b)


def _row(v, width=None):
    v = v.reshape(1, -1).astype(F32)
    if width is not None and v.shape[1] < width:
        v = jnp.pad(v, ((0, 0), (0, width - v.shape[1])))
    return v


def _layer_weights(p, l):
    w_in = p['w_in'][l]
    o = np.cumsum([N_BRANCH * D_MODEL, WIDTH, WIDTH, WIDTH, WIDTH, HEADS, WIDTH, CONV_DIM, HEADS])
    gates, u, q, k, v, fg, z, xbc, dtw = (w_in[:, a:b] for a, b in zip([0] + list(o[:-1]), o))
    w = {}
    w['w_main'] = jnp.concatenate([u, q, k, v, z, xbc], axis=1).astype(BF16)
    w['w_small'] = jnp.pad(jnp.concatenate([fg, dtw], axis=1), ((0, 0), (0, LANES - 2 * HEADS))).astype(BF16)
    w['b_small'] = _row(jnp.concatenate([p['b_fgate'][l], p['ssd_dt_bias'][l]]), LANES)
    w['w_gates'] = gates.astype(BF16)
    w['b_gates'] = _row(p['b_gate'][l])
    for name in ('w_branch_s5', 'w_branch_fox', 'w_branch_ssd', 'w_o'):
        w[name] = p[name][l].astype(BF16)
    for name in ('ln1_g', 'ln1_b', 'ln2_g', 'ln2_b', 's5_d', 's5_b_glu', 'ssd_conv_b', 'ssd_norm_w'):
        w[name] = _row(p[name][l])
    w['s5_w_glu'] = p['s5_w_glu'][l].astype(BF16)
    w['ssd_conv_w'] = p['ssd_conv_w'][l].astype(F32)
    w['ssd_alog_row'] = _row(jnp.concatenate([jnp.zeros((HEADS,), F32), p['ssd_a_log'][l]]), LANES)
    w['ssd_alog_col'] = jnp.concatenate([jnp.zeros((HEADS,), F32), p['ssd_a_log'][l]]).reshape(2 * HEADS, 1)
    w['ssd_d'] = _row(jnp.repeat(p['ssd_d'][l], HEAD_DIM))
    e = np.zeros((LANES, WIDTH), np.float32)
    for h in range(HEADS):
        e[HEADS + h, HEAD_DIM * h:HEAD_DIM * (h + 1)] = 1.0
    w['ssd_expand'] = jnp.asarray(e)
    if l % 2 == 0:
        for name in ('ffn_w_gate', 'ffn_w_up', 'ffn_w_down'):
            w[name] = p[name][l // 2].astype(BF16)
    else:
        w['moe_w_router'] = jnp.pad(p['moe_w_router'][l // 2], ((0, 0), (0, LANES - N_EXPERTS))).astype(BF16)
        w['moe_b_router'] = _row(p['moe_b_router'][l // 2], LANES)
        for name in ('moe_w_gate', 'moe_w_up', 'moe_w_down'):
            w[name] = p[name][l // 2].astype(BF16)
    return w


def _s5_q(seq):
    return min(16, seq)


def _run_trunk(x, p, weights, s5_tabs, s5_re0, s5_im0, conv0, ssd0, paged):
    bsz, seq, _ = x.shape
    t = bsz * seq
    tm_proj = min(256, t)
    tm_mix = min(512, t)
    x2 = x.reshape(t, D_MODEL)
    new = [[] for _ in range(7)]
    q5 = _s5_q(seq)
    for l in range(len(weights)):
        w = weights[l]
        u, ub, qb, k, v, kb, vb, z, xbc, small, small_t, logf = _in_proj(
            x2, w['w_main'], w['w_small'], w['b_small'], tm_proj)

        tab = s5_tabs[q5][l]
        yi = _s5_toeplitz(ub.reshape(t // q5, q5 * WIDTH), tab['w_toep'], q5, min(128, t // q5))
        d_row, w_glu, b_glu = w['s5_d'], w['s5_w_glu'], w['s5_b_glu']
        if paged is None:
            y_s5, s5_re, s5_im = _s5_scan(u.reshape(bsz, seq, WIDTH), yi.reshape(bsz, seq, WIDTH),
                                          s5_re0[l].reshape(bsz, 1, S5_LANES), s5_im0[l].reshape(bsz, 1, S5_LANES),
                                          tab, d_row, w_glu, b_glu, q5, min(256, seq), True)
        else:
            y_s5, s5_re, s5_im = _s5_scan(u.reshape(1, t, WIDTH), yi.reshape(1, t, WIDTH),
                                          s5_re0[l].reshape(bsz, S5_LANES), s5_im0[l].reshape(bsz, S5_LANES),
                                          tab, d_row, w_glu, b_glu, q5, min(256, t), False)
        y_s5 = y_s5.reshape(t, WIDTH)
        s5_re = s5_re.reshape(bsz, S5_GROUPS, S5_STATE)
        s5_im = s5_im.reshape(bsz, S5_GROUPS, S5_STATE)

        if paged is None:
            lf_t = small_t[:HEADS].reshape(HEADS * bsz, seq)
            c = _cumsum_rows(lf_t).reshape(HEADS // 2, 2, bsz, seq).transpose(2, 0, 1, 3)
            y_fox = _fox_prompt(qb.reshape(bsz, seq, WIDTH), kb.reshape(bsz, seq, WIDTH),
                                vb.reshape(bsz, seq, WIDTH), c, min(512, seq))
        else:
            cache_k, cache_v, cache_lf, page_table = paged
            n_pool = cache_k.shape[1]
            lfn_t = jnp.pad(logf.reshape(bsz, seq, HEADS).transpose(0, 2, 1),
                            ((0, 0), (0, 0), (0, LANES - seq)))
            y_fox = _fox_sample(qb.astype(F32).reshape(bsz, seq, WIDTH), k.reshape(bsz, seq, WIDTH),
                                v.reshape(bsz, seq, WIDTH), lfn_t,
                                cache_k[l].reshape(n_pool, PAGE, WIDTH), cache_v[l].reshape(n_pool, PAGE, WIDTH),
                                cache_lf[l].transpose(0, 2, 1), page_table)
        y_fox = y_fox.reshape(t, WIDTH)

        nv = math.gcd(seq, SSD_CHUNK)
        sm3 = small.reshape(bsz, seq, LANES)
        smt3 = small_t.reshape(2 * HEADS, bsz, seq).transpose(1, 0, 2)
        if nv < SSD_CHUNK:
            sm3 = jnp.pad(sm3, ((0, 0), (0, SSD_CHUNK - nv), (0, 0)))
            smt3 = jnp.pad(smt3, ((0, 0), (0, 0), (0, SSD_CHUNK - nv)))
        y_ssd, conv_new, ssd_new = _ssd(
            z.reshape(bsz, seq, WIDTH), xbc.reshape(bsz, seq, CONV_DIM), sm3, smt3, conv0[l],
            ssd0[l].reshape(bsz, WIDTH, SSD_STATE), w['ssd_conv_w'], w['ssd_conv_b'], w['ssd_alog_row'],
            w['ssd_alog_col'], w['ssd_expand'], w['ssd_d'], w['ssd_norm_w'], nv)
        y_ssd = y_ssd.reshape(t, WIDTH)
        ssd_new = ssd_new.reshape(bsz, HEADS, HEAD_DIM, SSD_STATE)

        x2 = _merge(x2, y_s5, y_fox, y_ssd, w['w_gates'], w['b_gates'], w['w_branch_s5'],
                    w['w_branch_fox'], w['w_branch_ssd'], w['w_o'], w['ln1_g'], w['ln1_b'], tm_mix)
        if l % 2 == 0:
            x2 = _ffn(x2, w['ffn_w_gate'], w['ffn_w_up'], w['ffn_w_down'], w['ln2_g'], w['ln2_b'], tm_mix)
        else:
            x2 = _moe(x2, w['moe_w_router'], w['moe_b_router'], w['moe_w_gate'], w['moe_w_up'],
                      w['moe_w_down'], w['ln2_g'], w['ln2_b'], tm_mix)

        st = (k.reshape(bsz, seq, HEADS, HEAD_DIM), v.reshape(bsz, seq, HEADS, HEAD_DIM),
              logf.reshape(bsz, seq, HEADS), s5_re, s5_im, conv_new, ssd_new)
        for lst, s in zip(new, st):
            lst.append(s)
    return x2.reshape(bsz, seq, D_MODEL), [jnp.stack(s) for s in new]


def kernel(x_prompt, x_sample, cache_k, cache_v, cache_logf, page_table, state_s5_re, state_s5_im,
           state_conv, state_ssd, w_in, b_gate, b_fgate, s5_lam_re, s5_lam_im, s5_log_dt, s5_b_re,
           s5_b_im, s5_c_re, s5_c_im, s5_d, s5_w_glu, s5_b_glu, ssd_conv_w, ssd_conv_b, ssd_dt_bias,
           ssd_a_log, ssd_d, ssd_norm_w, w_branch_s5, w_branch_fox, w_branch_ssd, w_o, ln1_g, ln1_b,
           ln2_g, ln2_b, ffn_w_gate, ffn_w_up, ffn_w_down, moe_w_router, moe_b_router, moe_w_gate,
           moe_w_up, moe_w_down):
    p = dict(w_in=w_in, b_gate=b_gate, b_fgate=b_fgate, s5_d=s5_d, s5_w_glu=s5_w_glu, s5_b_glu=s5_b_glu,
             ssd_conv_w=ssd_conv_w, ssd_conv_b=ssd_conv_b, ssd_dt_bias=ssd_dt_bias, ssd_a_log=ssd_a_log,
             ssd_d=ssd_d, ssd_norm_w=ssd_norm_w, w_branch_s5=w_branch_s5, w_branch_fox=w_branch_fox,
             w_branch_ssd=w_branch_ssd, w_o=w_o, ln1_g=ln1_g, ln1_b=ln1_b, ln2_g=ln2_g, ln2_b=ln2_b,
             ffn_w_gate=ffn_w_gate, ffn_w_up=ffn_w_up, ffn_w_down=ffn_w_down, moe_w_router=moe_w_router,
             moe_b_router=moe_b_router, moe_w_gate=moe_w_gate, moe_w_up=moe_w_up, moe_w_down=moe_w_down)
    depth = w_in.shape[0]
    weights = [_layer_weights(p, l) for l in range(depth)]
    qs = {_s5_q(x_prompt.shape[1]), _s5_q(x_sample.shape[1])}
    s5_tabs = {q: [_s5_tables(s5_lam_re[l].astype(F32), s5_lam_im[l].astype(F32), s5_log_dt[l].astype(F32),
                              s5_b_re[l].astype(F32), s5_b_im[l].astype(F32), s5_c_re[l].astype(F32),
                              s5_c_im[l].astype(F32), q) for l in range(depth)] for q in qs}
    bsz = x_prompt.shape[0]
    z_s5 = jnp.zeros((depth, bsz, S5_GROUPS, S5_STATE), F32)
    z_conv = jnp.zeros((depth, bsz, SSD_CONV - 1, CONV_DIM), F32)
    z_ssd = jnp.zeros((depth, bsz, HEADS, HEAD_DIM, SSD_STATE), F32)
    y_p, st_p = _run_trunk(x_prompt, p, weights, s5_tabs, z_s5, z_s5, z_conv, z_ssd, None)
    y_s, st_s = _run_trunk(x_sample, p, weights, s5_tabs, state_s5_re, state_s5_im, state_conv, state_ssd,
                           (cache_k, cache_v, cache_logf, page_table))
    return (y_p, y_s, *st_p, *st_s)
```

```python
import functools
import math

import jax
import jax.numpy as jnp
import numpy as np
from jax import lax
from jax.experimental import pallas as pl
from jax.experimental.pallas import tpu as pltpu

F32 = jnp.float32
BF16 = jnp.bfloat16
HIGHEST = lax.Precision.HIGHEST

D_MODEL = 1024
N_BRANCH = 3
WIDTH = 512
S5_GROUPS = 32
S5_GROUP = 16
S5_STATE = 64
S5_LANES = S5_GROUPS * S5_STATE
HEADS = 8
HEAD_DIM = 64
SSD_STATE = 128
SSD_GROUPS = 2
SSD_CONV = 4
SSD_CHUNK = 128
CONV_DIM = 1024
D_FF = 2816
N_EXPERTS = 8
PAGE = 128
ALPHA = 4.0 ** 0.25
LN_EPS = 1e-5
RMS_EPS = 1e-5
NEG = -1e30
LANES = 128
VMEM_LIMIT = 56 * 1024 * 1024

NT_DIMS = (((1,), (1,)), ((), ()))


def _params(*sem):
    return pltpu.CompilerParams(dimension_semantics=sem, vmem_limit_bytes=VMEM_LIMIT)


def _const_spec(shape):
    nd = len(shape)
    return pl.BlockSpec(shape, lambda *_: (0,) * nd, pipeline_mode=pl.Buffered(1))


def _layer_norm(x, g, b):
    mu = jnp.mean(x, axis=-1, keepdims=True)
    xc = x - mu
    var = jnp.mean(xc * xc, axis=-1, keepdims=True)
    return xc * lax.rsqrt(var + LN_EPS) * g + b


def _sigmoid(x):
    return 1.0 / (1.0 + jnp.exp(-x))


def _silu(x):
    return x * _sigmoid(x)


def _small_act(xb, ws_ref, bs_ref):
    s = jnp.dot(xb, ws_ref[...], preferred_element_type=F32) + bs_ref[...]
    t = jnp.log1p(jnp.exp(-jnp.abs(s)))
    lane = lax.broadcasted_iota(jnp.int32, s.shape, 1)
    return jnp.where(lane < HEADS, jnp.minimum(s, 0.0) - t,
                     jnp.where(lane < 2 * HEADS, jnp.maximum(s, 0.0) + t, 0.0))


SEG_U, SEG_Q, SEG_K, SEG_V, SEG_Z, SEG_X, SEG_END = 0, 512, 1024, 1536, 2048, 2560, 3584


def _in_proj_sample_kernel(x_ref, wm_ref, ws_ref, bs_ref, u_ref, ub_ref, qb_ref, k_ref, v_ref,
                           z_ref, xbc_ref, small_ref, smallt_ref, logf_ref):
    xb = x_ref[...].astype(BF16)

    def seg(a, b):
        return jnp.dot(xb, wm_ref[:, a:b], preferred_element_type=F32)

    u = seg(SEG_U, SEG_Q)
    u_ref[...] = u
    ub_ref[...] = u.astype(BF16)
    qb_ref[...] = (seg(SEG_Q, SEG_K) * (HEAD_DIM ** -0.5)).astype(BF16)
    k_ref[...] = seg(SEG_K, SEG_V)
    v_ref[...] = seg(SEG_V, SEG_Z)
    z_ref[...] = seg(SEG_Z, SEG_X)
    xbc_ref[...] = seg(SEG_X, SEG_END)
    sm = _small_act(xb, ws_ref, bs_ref)
    small_ref[...] = sm
    smallt_ref[...] = sm.T[:2 * HEADS, :]
    logf_ref[...] = sm[:, :HEADS]


def _in_proj_sample(x2, wm, ws, bs, tm):
    t = x2.shape[0]
    row = lambda w: pl.BlockSpec((tm, w), lambda i: (i, 0))
    out_shape = (
        jax.ShapeDtypeStruct((t, WIDTH), F32),
        jax.ShapeDtypeStruct((t, WIDTH), BF16),
        jax.ShapeDtypeStruct((t, WIDTH), BF16),
        jax.ShapeDtypeStruct((t, WIDTH), F32),
        jax.ShapeDtypeStruct((t, WIDTH), F32),
        jax.ShapeDtypeStruct((t, WIDTH), F32),
        jax.ShapeDtypeStruct((t, CONV_DIM), F32),
        jax.ShapeDtypeStruct((t, LANES), F32),
        jax.ShapeDtypeStruct((2 * HEADS, t), F32),
        jax.ShapeDtypeStruct((t, HEADS), F32),
    )
    out_specs = (row(WIDTH), row(WIDTH), row(WIDTH), row(WIDTH), row(WIDTH),
                 row(WIDTH), row(CONV_DIM), row(LANES),
                 pl.BlockSpec((2 * HEADS, tm), lambda i: (0, i)), row(HEADS))
    return pl.pallas_call(
        _in_proj_sample_kernel,
        grid=(t // tm,),
        in_specs=[row(D_MODEL), _const_spec(wm.shape), _const_spec(ws.shape), _const_spec(bs.shape)],
        out_specs=out_specs,
        out_shape=out_shape,
        compiler_params=_params("parallel"),
        name="in_proj_sample",
    )(x2, wm, ws, bs)


PSEG_U, PSEG_K, PSEG_Z, PSEG_X, PSEG_END = 0, 512, 1536, 2048, 3072
TSEG_Q, TSEG_K, TSEG_V, TSEG_END = 0, 1024, 1536, 2048
BIAS_LANES = 3


def _in_proj_prompt_kernel(x_ref, wr_ref, wt_ref, ws_ref, bs_ref, e_ref, u_ref, ub_ref, z_ref, xbc_ref,
                           small_ref, smallt_ref, kaug_ref, qaugt_ref, kt_ref, vt_ref, tot_ref):
    tm = x_ref.shape[0]
    xb = x_ref[...].astype(BF16)
    u = jnp.dot(xb, wr_ref[:, PSEG_U:PSEG_K], preferred_element_type=F32)
    u_ref[...] = u
    ub_ref[...] = u.astype(BF16)
    z_ref[...] = jnp.dot(xb, wr_ref[:, PSEG_Z:PSEG_X], preferred_element_type=F32)
    xbc_ref[...] = jnp.dot(xb, wr_ref[:, PSEG_X:PSEG_END], preferred_element_type=F32)
    sm = _small_act(xb, ws_ref, bs_ref)
    small_ref[...] = sm
    smt = sm.T[:2 * HEADS, :]
    smallt_ref[0] = smt
    r_i = lax.broadcasted_iota(jnp.int32, (tm, tm), 0)
    c_i = lax.broadcasted_iota(jnp.int32, (tm, tm), 1)
    cloc = jnp.dot((r_i >= c_i).astype(F32), sm, precision=HIGHEST, preferred_element_type=F32)
    tot_ref[0, 0] = jnp.dot(smt, jnp.ones((tm, tm), F32), precision=HIGHEST, preferred_element_type=F32)
    neg = -cloc
    hi = neg.astype(BF16)
    r1 = neg - hi.astype(F32)
    mid = r1.astype(BF16)
    lo = (r1 - mid.astype(F32)).astype(BF16)
    for h in range(HEADS):
        cols = slice(PSEG_K + LANES * h, PSEG_K + LANES * (h + 1))
        ecol = slice(LANES * h, LANES * (h + 1))
        kh = jnp.dot(xb, wr_ref[:, cols], preferred_element_type=F32)
        kh = (kh + jnp.dot(hi, e_ref[0, :, ecol], preferred_element_type=F32)
              + jnp.dot(mid, e_ref[1, :, ecol], preferred_element_type=F32)
              + jnp.dot(lo, e_ref[2, :, ecol], preferred_element_type=F32))
        kaug_ref[0, h] = kh.astype(BF16)
        qh = lax.dot_general(wt_ref[TSEG_Q + LANES * h:TSEG_Q + LANES * (h + 1), :], xb, NT_DIMS,
                             preferred_element_type=F32)
        row = lax.broadcasted_iota(jnp.int32, qh.shape, 0)
        ones_rows = (row >= HEAD_DIM) & (row < HEAD_DIM + BIAS_LANES)
        qaugt_ref[0, h] = jnp.where(ones_rows, 1.0, qh * (HEAD_DIM ** -0.5)).astype(BF16)
    kt_ref[0] = lax.dot_general(wt_ref[TSEG_K:TSEG_V, :], xb, NT_DIMS, preferred_element_type=F32)
    vt_ref[0] = lax.dot_general(wt_ref[TSEG_V:TSEG_END, :], xb, NT_DIMS, preferred_element_type=F32)


def _in_proj_prompt(x3, wr, wt, ws, bs, e3, tm):
    bsz, seq, _ = x3.shape
    nt = seq // tm
    x2 = x3.reshape(bsz * seq, D_MODEL)
    row = lambda w: pl.BlockSpec((tm, w), lambda b, i: (b * nt + i, 0))
    colt = lambda r: pl.BlockSpec((1, r, tm), lambda b, i: (b, 0, i))
    t = bsz * seq
    out_shape = (
        jax.ShapeDtypeStruct((t, WIDTH), F32),
        jax.ShapeDtypeStruct((t, WIDTH), BF16),
        jax.ShapeDtypeStruct((t, WIDTH), F32),
        jax.ShapeDtypeStruct((t, CONV_DIM), F32),
        jax.ShapeDtypeStruct((t, LANES), F32),
        jax.ShapeDtypeStruct((bsz, 2 * HEADS, seq), F32),
        jax.ShapeDtypeStruct((bsz, HEADS, seq, LANES), BF16),
        jax.ShapeDtypeStruct((bsz, HEADS, LANES, seq), BF16),
        jax.ShapeDtypeStruct((bsz, WIDTH, seq), F32),
        jax.ShapeDtypeStruct((bsz, WIDTH, seq), F32),
        jax.ShapeDtypeStruct((bsz, nt, 2 * HEADS, tm), F32),
    )
    out_specs = (row(WIDTH), row(WIDTH), row(WIDTH), row(CONV_DIM), row(LANES), colt(2 * HEADS),
                 pl.BlockSpec((1, HEADS, tm, LANES), lambda b, i: (b, 0, i, 0)),
                 pl.BlockSpec((1, HEADS, LANES, tm), lambda b, i: (b, 0, 0, i)),
                 colt(WIDTH), colt(WIDTH),
                 pl.BlockSpec((1, 1, 2 * HEADS, tm), lambda b, i: (b, i, 0, 0)))
    consts = [wr, wt, ws, bs, e3]
    return pl.pallas_call(
        _in_proj_prompt_kernel,
        grid=(bsz, nt),
        in_specs=[row(D_MODEL)] + [_const_spec(c.shape) for c in consts],
        out_specs=out_specs,
        out_shape=out_shape,
        compiler_params=_params("parallel", "parallel"),
        name="in_proj_prompt",
    )(x2, *consts)


def _cumsum_lanes(x):
    n = x.shape[-1]
    lane = lax.broadcasted_iota(jnp.int32, x.shape, x.ndim - 1)
    s = 1
    while s < n:
        x = x + jnp.where(lane >= s, pltpu.roll(x, s, axis=x.ndim - 1), 0.0)
        s *= 2
    return x


FOX_HEADS_PER_STEP = 4


def _fox_prompt_kernel(it_ref, jt_ref, q_ref, k_ref, v_ref, tot_ref, o_ref, m_sc, l_sc, c_sc, acc_sc, *, tq, tk):
    g = pl.program_id(1)
    t = pl.program_id(2)
    i = it_ref[t]
    j = jt_ref[t]
    hps = FOX_HEADS_PER_STEP

    @pl.when(j == 0)
    def _():
        m_sc[...] = jnp.full(m_sc.shape, NEG, F32)
        l_sc[...] = jnp.zeros(l_sc.shape, F32)
        c_sc[...] = jnp.zeros(c_sc.shape, F32)
        acc_sc[...] = jnp.zeros(acc_sc.shape, F32)

    def step(masked):
        if masked:
            kpos = j * tk + lax.broadcasted_iota(jnp.int32, (tk, tq), 0)
            qpos = i * tq + lax.broadcasted_iota(jnp.int32, (tk, tq), 1)
            causal = kpos <= qpos
        for h in range(hps):
            s = jnp.dot(k_ref[0, h], q_ref[0, h], preferred_element_type=F32)
            if masked:
                s = jnp.where(causal, s, NEG)
            c_j = c_sc[h]
            m_prev = m_sc[h]
            m_new = jnp.maximum(m_prev, jnp.max(s, axis=0, keepdims=True) - c_j)
            alpha = jnp.exp(m_prev - m_new)
            p = jnp.exp(s - (m_new + c_j))
            l_sc[h] = alpha * l_sc[h] + jnp.sum(p, axis=0, keepdims=True)
            rows = slice(HEAD_DIM * h, HEAD_DIM * (h + 1))
            acc_sc[rows, :] = alpha * acc_sc[rows, :] + jnp.dot(
                v_ref[0, rows, :].astype(BF16), p.astype(BF16), preferred_element_type=F32)
            m_sc[h] = m_new
            c_sc[h] = c_j + tot_ref[0, 0, pl.ds(g * hps + h, 1), :]

    @pl.when(j < i)
    def _():
        step(False)

    @pl.when(j == i)
    def _():
        step(True)
        inv = jnp.concatenate([jnp.broadcast_to(1.0 / l_sc[h], (HEAD_DIM, tq)) for h in range(hps)], axis=0)
        o_ref[0] = (acc_sc[...] * inv).T.astype(o_ref.dtype)


def _fox_prompt(qaugt, kaug, vt, tot, tq):
    bsz, _, _, seq = qaugt.shape
    nq = seq // tq
    hps = FOX_HEADS_PER_STEP
    it = np.array([i for i in range(nq) for j in range(i + 1)], np.int32)
    jt = np.array([j for i in range(nq) for j in range(i + 1)], np.int32)
    grid_spec = pltpu.PrefetchScalarGridSpec(
        num_scalar_prefetch=2,
        grid=(bsz, HEADS // hps, len(it)),
        in_specs=[
            pl.BlockSpec((1, hps, LANES, tq), lambda b, g, t, it, jt: (b, g, 0, it[t])),
            pl.BlockSpec((1, hps, tq, LANES), lambda b, g, t, it, jt: (b, g, jt[t], 0)),
            pl.BlockSpec((1, hps * HEAD_DIM, tq), lambda b, g, t, it, jt: (b, g, jt[t])),
            pl.BlockSpec((1, 1, 2 * HEADS, tq), lambda b, g, t, it, jt: (b, jt[t], 0, 0)),
        ],
        out_specs=pl.BlockSpec((1, tq, hps * HEAD_DIM), lambda b, g, t, it, jt: (b, it[t], g)),
        scratch_shapes=[pltpu.VMEM((hps, 1, tq), F32), pltpu.VMEM((hps, 1, tq), F32),
                        pltpu.VMEM((hps, 1, tq), F32), pltpu.VMEM((hps * HEAD_DIM, tq), F32)],
    )
    return pl.pallas_call(
        functools.partial(_fox_prompt_kernel, tq=tq, tk=tq),
        grid_spec=grid_spec,
        out_shape=jax.ShapeDtypeStruct((bsz, seq, WIDTH), BF16),
        compiler_params=_params("parallel", "parallel", "arbitrary"),
        name="fox_prompt",
    )(jnp.asarray(it), jnp.asarray(jt), qaugt, kaug, vt, tot)


def _fox_sample_kernel(pt_ref, q_ref, kn_ref, vn_ref, lfn_ref, *refs, n_pages, tq):
    k_refs = refs[:n_pages]
    v_refs = refs[n_pages:2 * n_pages]
    lf_refs = refs[2 * n_pages:3 * n_pages]
    o_ref = refs[3 * n_pages]
    rows = tq * HEADS
    r_i = lax.broadcasted_iota(jnp.int32, (rows, tq), 0)
    t_i = lax.broadcasted_iota(jnp.int32, (rows, tq), 1)
    rep = (r_i // HEADS == t_i).astype(F32)
    q = q_ref[0].astype(F32)
    qe = jnp.dot(rep, q, precision=HIGHEST, preferred_element_type=F32)
    r_w = lax.broadcasted_iota(jnp.int32, (rows, WIDTH), 0)
    l_w = lax.broadcasted_iota(jnp.int32, (rows, WIDTH), 1)
    head_mask = (l_w // HEAD_DIM) == (r_w % HEADS)
    qbd = jnp.where(head_mask, qe, 0.0).astype(BF16)

    scores = [None] * n_pages
    carry = jnp.zeros((HEADS, 1), F32)
    for p in range(n_pages - 1, -1, -1):
        cs = _cumsum_lanes(lf_refs[p][0, 0])
        tot = cs[:, PAGE - 1:PAGE]
        dec = (carry + tot) - cs
        carry = carry + tot
        kt = k_refs[p][0, 0].reshape(WIDTH, PAGE).astype(BF16)
        s = jnp.dot(qbd, kt, preferred_element_type=F32)
        scores[p] = s + jnp.concatenate([dec] * tq, axis=0)
    pad = jnp.zeros((PAGE - tq, WIDTH), F32)
    kn = jnp.concatenate([kn_ref[0], pad], axis=0).astype(BF16)
    vn = jnp.concatenate([vn_ref[0], pad], axis=0).astype(BF16)
    m_i = lax.broadcasted_iota(jnp.int32, (PAGE, PAGE), 0)
    c_i = lax.broadcasted_iota(jnp.int32, (PAGE, PAGE), 1)
    cn = jnp.dot(lfn_ref[0], (m_i <= c_i).astype(F32), precision=HIGHEST,
                 preferred_element_type=F32)
    r_p = lax.broadcasted_iota(jnp.int32, (rows, PAGE), 0)
    c_p = lax.broadcasted_iota(jnp.int32, (rows, PAGE), 1)
    s_new = lax.dot_general(qbd, kn, NT_DIMS, preferred_element_type=F32)
    s_new = jnp.where(c_p <= r_p // HEADS, s_new - jnp.concatenate([cn] * tq, axis=0), NEG)

    m = s_new
    for p in range(n_pages):
        m = jnp.maximum(m, scores[p])
    m = jnp.max(m, axis=-1, keepdims=True)
    pn = jnp.exp(s_new - m)
    l = jnp.sum(pn, axis=-1, keepdims=True)
    o = jnp.dot(pn.astype(BF16), vn, preferred_element_type=F32)
    for p in range(n_pages):
        pp = jnp.exp(scores[p] - m)
        l = l + jnp.sum(pp, axis=-1, keepdims=True)
        vt = v_refs[p][0, 0].reshape(WIDTH, PAGE).astype(BF16)
        o = o + lax.dot_general(pp.astype(BF16), vt, NT_DIMS, preferred_element_type=F32)
    om = jnp.where(head_mask, o / l, 0.0)
    rep_t = (lax.broadcasted_iota(jnp.int32, (tq, rows), 1) // HEADS
             == lax.broadcasted_iota(jnp.int32, (tq, rows), 0)).astype(F32)
    o_ref[0] = jnp.dot(rep_t, om, precision=HIGHEST, preferred_element_type=F32).astype(o_ref.dtype)


def _fox_sample(qb, k_new, v_new, lfn_t, cache_kt, cache_vt, cache_lft, layer, page_table):
    dbsz, tq, _ = qb.shape
    n_pages = page_table.shape[1]
    pt = page_table.reshape(-1).astype(jnp.int32)

    def page_spec(p, shape):
        zeros = (0,) * (len(shape) - 2)
        return pl.BlockSpec(shape, lambda b, pt: (layer, pt[b * n_pages + p]) + zeros)

    seq_spec = lambda shape: pl.BlockSpec(shape, lambda b, pt: (b, 0, 0))
    in_specs = [seq_spec((1, tq, WIDTH)), seq_spec((1, tq, WIDTH)), seq_spec((1, tq, WIDTH)),
                seq_spec((1, HEADS, LANES))]
    in_specs += [page_spec(p, (1, 1, HEADS, HEAD_DIM, PAGE)) for p in range(n_pages)]
    in_specs += [page_spec(p, (1, 1, HEADS, HEAD_DIM, PAGE)) for p in range(n_pages)]
    in_specs += [page_spec(p, (1, 1, HEADS, PAGE)) for p in range(n_pages)]
    grid_spec = pltpu.PrefetchScalarGridSpec(
        num_scalar_prefetch=1, grid=(dbsz,), in_specs=in_specs,
        out_specs=pl.BlockSpec((1, tq, WIDTH), lambda b, pt: (b, 0, 0)))
    return pl.pallas_call(
        functools.partial(_fox_sample_kernel, n_pages=n_pages, tq=tq),
        grid_spec=grid_spec,
        out_shape=jax.ShapeDtypeStruct((dbsz, tq, WIDTH), F32),
        compiler_params=_params("parallel"),
        name="fox_sample",
    )(pt, qb, k_new, v_new, lfn_t, *([cache_kt] * n_pages), *([cache_vt] * n_pages),
      *([cache_lft] * n_pages))


def _s5_tables(lam_re, lam_im, log_dt, b_re, b_im, c_re, c_im, q):
    hp = dict(precision=HIGHEST)
    dt = jnp.exp(log_dt)[:, None]
    mag = jnp.exp(lam_re * dt)
    ab_re = mag * jnp.cos(lam_im * dt)
    ab_im = mag * jnp.sin(lam_im * dt)
    nr = ab_re - 1.0
    den = lam_re * lam_re + lam_im * lam_im
    q_re = (nr * lam_re + ab_im * lam_im) / den
    q_im = (ab_im * lam_re - nr * lam_im) / den
    bb_re = q_re[..., None] * b_re - q_im[..., None] * b_im
    bb_im = q_re[..., None] * b_im + q_im[..., None] * b_re
    jj = jnp.arange(q + 1, dtype=F32)[:, None, None]
    pmag = jnp.exp(lam_re * dt * jj)
    p_re = pmag * jnp.cos(lam_im * dt * jj)
    p_im = pmag * jnp.sin(lam_im * dt * jj)
    ab_b_re = p_re[:q, :, :, None] * bb_re - p_im[:q, :, :, None] * bb_im
    ab_b_im = p_re[:q, :, :, None] * bb_im + p_im[:q, :, :, None] * bb_re
    kern = (jnp.einsum('gcn,jgnd->jgcd', c_re, ab_b_re, **hp)
            - jnp.einsum('gcn,jgnd->jgcd', c_im, ab_b_im, **hp))
    eye16 = jnp.eye(16, dtype=F32)
    kt = kern.transpose(0, 1, 3, 2).reshape(q, 2, 16, S5_GROUP, S5_GROUP)
    w_toep = jnp.einsum('jxgab,gh->jxgahb', kt, eye16).reshape(q, 2, 256, 256).astype(BF16)
    eye_g = jnp.eye(S5_GROUPS, dtype=F32)

    def in_map(bb):
        full = jnp.einsum('gnc,gh->gchn', bb, eye_g).reshape(WIDTH, S5_LANES)
        return jnp.stack([full[128 * (n // 2):128 * (n // 2) + 128, 256 * n:256 * n + 256]
                          for n in range(8)]).astype(BF16)

    def out_map(c):
        full = jnp.einsum('gcn,gh->gnhc', c, eye_g).reshape(S5_LANES, WIDTH)
        return jnp.stack([full[512 * m:512 * m + 512, 128 * m:128 * m + 128]
                          for m in range(4)]).astype(BF16)

    rows = ((q + 1 + 7) // 8) * 8
    padp = lambda p: jnp.pad(p.reshape(q + 1, S5_LANES), ((0, rows - q - 1), (0, 0)))
    return dict(w_toep=w_toep, wb_re=in_map(bb_re), wb_im=in_map(bb_im), wc_re=out_map(c_re),
                wc_im=out_map(-c_im), p_re=padp(p_re), p_im=padp(p_im))


def _s5_chunk_kernel(u_ref, w_ref, wbr_ref, wbi_ref, pr_ref, pi_ref, y_ref, sr_ref, si_ref, *, q):
    for s_out in range(q):
        for half in range(2):
            acc = None
            for j in range(s_out + 1):
                a = (s_out - j) * WIDTH + half * 256
                d = jnp.dot(u_ref[:, a:a + 256], w_ref[j, half], preferred_element_type=F32)
                acc = d if acc is None else acc + d
            o = s_out * WIDTH + half * 256
            y_ref[:, o:o + 256] = acc
    for n in range(8):
        lanes = slice(256 * n, 256 * (n + 1))
        acc_r = None
        acc_i = None
        for s in range(q):
            a = s * WIDTH + 128 * (n // 2)
            br = jnp.dot(u_ref[:, a:a + 128], wbr_ref[n], preferred_element_type=F32)
            bi = jnp.dot(u_ref[:, a:a + 128], wbi_ref[n], preferred_element_type=F32)
            pr = pr_ref[q - 1 - s:q - s, lanes]
            pi = pi_ref[q - 1 - s:q - s, lanes]
            tr = pr * br - pi * bi
            ti = pr * bi + pi * br
            acc_r = tr if acc_r is None else acc_r + tr
            acc_i = ti if acc_i is None else acc_i + ti
        sr_ref[:, lanes] = acc_r
        si_ref[:, lanes] = acc_i


def _s5_chunks(u2b, tab, q, rt):
    nc = u2b.shape[0]
    consts = [tab['w_toep'], tab['wb_re'], tab['wb_im'], tab['p_re'], tab['p_im']]
    state = pl.BlockSpec((rt, S5_LANES), lambda i: (i, 0))
    return pl.pallas_call(
        functools.partial(_s5_chunk_kernel, q=q),
        grid=(nc // rt,),
        in_specs=[pl.BlockSpec((rt, q * WIDTH), lambda i: (i, 0))] + [_const_spec(c.shape) for c in consts],
        out_specs=(pl.BlockSpec((rt, q * WIDTH), lambda i: (i, 0)), state, state),
        out_shape=(jax.ShapeDtypeStruct((nc, q * WIDTH), F32),
                   jax.ShapeDtypeStruct((nc, S5_LANES), F32), jax.ShapeDtypeStruct((nc, S5_LANES), F32)),
        compiler_params=_params("parallel"),
        name="s5_chunks",
    )(u2b, *consts)


def _s5_scan_kernel(u_ref, yi_ref, sre_ref, sim_ref, h0r_ref, h0i_ref, wcr_ref, wci_ref, pr_ref, pi_ref,
                    d_ref, wg_ref, bg_ref, y_ref, htr_ref, hti_ref,
                    xcr_sc, xci_sc, hr_sc, hi_sc, *, q, nct, carry):
    tile = pl.program_id(1)
    u = u_ref[0]
    aq_r = pr_ref[q:q + 1, :]
    aq_i = pi_ref[q:q + 1, :]
    p1r = pr_ref[1:q + 1, :]
    p1i = pi_ref[1:q + 1, :]

    if carry:
        @pl.when(tile == 0)
        def _():
            hr_sc[...] = h0r_ref[0]
            hi_sc[...] = h0i_ref[0]

        def body(c, h):
            hr, hi = h
            row0 = pl.multiple_of(c * q, q)
            xcr_sc[pl.ds(row0, q), :] = p1r * hr - p1i * hi
            xci_sc[pl.ds(row0, q), :] = p1r * hi + p1i * hr
            sr = sre_ref[pl.ds(c, 1), :]
            si = sim_ref[pl.ds(c, 1), :]
            return aq_r * hr - aq_i * hi + sr, aq_r * hi + aq_i * hr + si

        hr, hi = lax.fori_loop(0, nct, body, (hr_sc[...], hi_sc[...]))
        hr_sc[...] = hr
        hi_sc[...] = hi

        @pl.when(tile == pl.num_programs(1) - 1)
        def _():
            htr_ref[0] = hr
            hti_ref[0] = hi
    else:
        def body(c, _):
            hr = h0r_ref[pl.ds(c, 1), :]
            hi = h0i_ref[pl.ds(c, 1), :]
            row0 = pl.multiple_of(c * q, q)
            xcr_sc[pl.ds(row0, q), :] = p1r * hr - p1i * hi
            xci_sc[pl.ds(row0, q), :] = p1r * hi + p1i * hr
            return 0

        lax.fori_loop(0, nct, body, 0)
        h0r = h0r_ref[...]
        h0i = h0i_ref[...]
        htr_ref[...] = aq_r * h0r - aq_i * h0i + sre_ref[...]
        hti_ref[...] = aq_r * h0i + aq_i * h0r + sim_ref[...]

    cols = []
    for m in range(4):
        xr = xcr_sc[:, 512 * m:512 * m + 512].astype(BF16)
        xi = xci_sc[:, 512 * m:512 * m + 512].astype(BF16)
        cols.append(jnp.dot(xr, wcr_ref[m], preferred_element_type=F32)
                    + jnp.dot(xi, wci_ref[m], preferred_element_type=F32))
    y = yi_ref[0] + jnp.concatenate(cols, axis=-1) + d_ref[...] * u
    y = jax.nn.gelu(y)
    g = jnp.dot(y.astype(BF16), wg_ref[...], preferred_element_type=F32) + bg_ref[...]
    y_ref[0] = (y * _sigmoid(g)).astype(y_ref.dtype)


def _s5_scan(u3, yi3, s_re, s_im, h0_re, h0_im, tab, d_row, w_glu, b_glu, q, tm, carry):
    bsz, seq, _ = u3.shape
    nct = tm // q
    n_tiles = seq // tm
    tok = pl.BlockSpec((1, tm, WIDTH), lambda b, t: (b, t, 0))
    chunk = pl.BlockSpec((nct, S5_LANES), lambda b, t: (b * n_tiles + t, 0))
    if carry:
        h_spec = pl.BlockSpec((1, 1, S5_LANES), lambda b, t: (b, 0, 0))
        h_shape = jax.ShapeDtypeStruct((bsz, 1, S5_LANES), F32)
    else:
        h_spec = pl.BlockSpec((nct, S5_LANES), lambda b, t: (t, 0))
        h_shape = jax.ShapeDtypeStruct((seq // q, S5_LANES), F32)
    consts = [tab['wc_re'], tab['wc_im'], tab['p_re'], tab['p_im'], d_row, w_glu, b_glu]
    return pl.pallas_call(
        functools.partial(_s5_scan_kernel, q=q, nct=nct, carry=carry),
        grid=(bsz, n_tiles),
        in_specs=[tok, tok, chunk, chunk, h_spec, h_spec] + [_const_spec(c.shape) for c in consts],
        out_specs=(tok, h_spec, h_spec),
        out_shape=(jax.ShapeDtypeStruct((bsz, seq, WIDTH), BF16), h_shape, h_shape),
        scratch_shapes=[pltpu.VMEM((tm, S5_LANES), F32), pltpu.VMEM((tm, S5_LANES), F32),
                        pltpu.VMEM((1, S5_LANES), F32), pltpu.VMEM((1, S5_LANES), F32)],
        compiler_params=_params("parallel", "arbitrary"),
        name="s5_scan",
    )(u3, yi3, s_re, s_im, h0_re, h0_im, *consts)


def _ssd_kernel(z_ref, xbc_ref, sm_ref, smt_ref, conv0_ref, h0_ref, cw_ref, cb_ref, alog_row_ref,
                alog_col_ref, e_ref, d_ref, nw_ref, y_ref, convt_ref, ht_ref,
                xp_sc, st_sc, zp_sc, *, nv):
    q = SSD_CHUNK
    c = pl.program_id(1)
    last = c == pl.num_programs(1) - 1

    @pl.when(c == 0)
    def _():
        st_sc[...] = h0_ref[0, 0]
        xp_sc[...] = jnp.zeros(xp_sc.shape, F32)
        xp_sc[8 - (SSD_CONV - 1):8, :] = conv0_ref[0]
        if nv < q:
            zp_sc[...] = jnp.zeros(zp_sc.shape, F32)

    xp_sc[8:8 + nv, :] = xbc_ref[0]
    acc = None
    for j in range(SSD_CONV):
        o = 8 - (SSD_CONV - 1) + j
        term = xp_sc[o:o + q, :] * cw_ref[j:j + 1, :]
        acc = term if acc is None else acc + term
    xc = _silu(acc + cb_ref[...])
    tail = xp_sc[8 + nv - (SSD_CONV - 1):8 + nv, :]

    @pl.when(last)
    def _():
        convt_ref[0] = tail

    xp_sc[8 - (SSD_CONV - 1):8, :] = tail
    if nv < q:
        zp_sc[0:nv, :] = z_ref[0]
        z = zp_sc[...]
    else:
        z = z_ref[0]

    xs = xc[:, :WIDTH]
    bm = xc[:, WIDTH:WIDTH + SSD_GROUPS * SSD_STATE]
    cm = xc[:, WIDTH + SSD_GROUPS * SSD_STATE:]

    sm = sm_ref[0]
    smt = smt_ref[0]
    lane = lax.broadcasted_iota(jnp.int32, (1, LANES), 1)
    a_row = jnp.where((lane >= HEADS) & (lane < 2 * HEADS), -jnp.exp(alog_row_ref[...]), 0.0)
    rowi = lax.broadcasted_iota(jnp.int32, (2 * HEADS, 1), 0)
    a_col = jnp.where(rowi >= HEADS, -jnp.exp(alog_col_ref[...]), 0.0)
    adt = sm * a_row
    adt_t = smt * a_col
    r_i = lax.broadcasted_iota(jnp.int32, (q, q), 0)
    c_i = lax.broadcasted_iota(jnp.int32, (q, q), 1)
    tri = r_i >= c_i
    acum = jnp.dot(tri.astype(F32), adt, precision=HIGHEST, preferred_element_type=F32)
    acum_t = jnp.dot(adt_t, (r_i <= c_i).astype(F32), precision=HIGHEST, preferred_element_type=F32)
    atot = acum[q - 1:q, :]
    stacked = jnp.concatenate([sm, jnp.exp(acum), jnp.exp(atot - acum)], axis=0)
    hi = stacked.astype(BF16)
    r1 = stacked - hi.astype(F32)
    mid = r1.astype(BF16)
    lo = (r1 - mid.astype(F32)).astype(BF16)
    e = e_ref[...]
    expanded = (jnp.dot(hi, e, preferred_element_type=F32) + jnp.dot(mid, e, preferred_element_type=F32)
                + jnp.dot(lo, e, preferred_element_type=F32))
    dt_e = expanded[0:q]
    eac_e = expanded[q:2 * q]
    dec_e = expanded[2 * q:3 * q]
    xd = xs * dt_e
    xdd = xd * dec_e
    lane_half = lax.broadcasted_iota(jnp.int32, (q, LANES), 1) // HEAD_DIM

    y_cols = []
    for g in range(SSD_GROUPS):
        bg = bm[:, SSD_STATE * g:SSD_STATE * (g + 1)].astype(BF16)
        cg = cm[:, SSD_STATE * g:SSD_STATE * (g + 1)].astype(BF16)
        cb = lax.dot_general(cg, bg, NT_DIMS, preferred_element_type=F32)
        hpg = HEADS // SSD_GROUPS
        st_g = st_sc[hpg * HEAD_DIM * g:hpg * HEAD_DIM * (g + 1), :]
        y_off = lax.dot_general(cg, st_g.astype(BF16), NT_DIMS, preferred_element_type=F32)
        for pair in range(hpg // 2):
            ys = []
            for hh in range(2):
                h = hpg * g + 2 * pair + hh
                diff = acum[:, HEADS + h:HEADS + h + 1] - acum_t[HEADS + h:HEADS + h + 1, :]
                lmat = jnp.exp(jnp.where(tri, diff, NEG))
                w = (cb * lmat).astype(BF16)
                lo = LANES * (2 * g + pair)
                ys.append(jnp.dot(w, xd[:, lo:lo + LANES].astype(BF16), preferred_element_type=F32))
            y_cols.append(jnp.where(lane_half == 0, ys[0], ys[1]))
        y_cols[-2] = y_cols[-2] + y_off[:, :LANES] * eac_e[:, 256 * g:256 * g + LANES]
        y_cols[-1] = y_cols[-1] + y_off[:, LANES:] * eac_e[:, 256 * g + LANES:256 * (g + 1)]
        contrib = jnp.dot(xdd[:, 256 * g:256 * (g + 1)].T.astype(BF16), bg, preferred_element_type=F32)
        for hl in range(hpg):
            h = hpg * g + hl
            sl = slice(HEAD_DIM * h, HEAD_DIM * (h + 1))
            dec_h = jnp.exp(acum_t[HEADS + h:HEADS + h + 1, q - 1:q])
            st_sc[sl, :] = st_sc[sl, :] * dec_h + contrib[HEAD_DIM * hl:HEAD_DIM * (hl + 1), :]

    y = jnp.concatenate(y_cols, axis=-1) + d_ref[...] * xs
    y = y * _silu(z)
    y = y * lax.rsqrt(jnp.mean(y * y, axis=-1, keepdims=True) + RMS_EPS) * nw_ref[...]
    y_ref[0] = y[:nv].astype(y_ref.dtype)

    @pl.when(last)
    def _():
        ht_ref[0] = st_sc[...]


def _ssd(z3, xbc3, sm3, smt3, conv0, h0, layer, cw, cb, alog_row, alog_col, e_mat, d_row, nw, nv):
    bsz, seq, _ = z3.shape
    nchunk = seq // nv
    q = SSD_CHUNK
    consts = [cw, cb, alog_row, alog_col, e_mat, d_row, nw]
    return pl.pallas_call(
        functools.partial(_ssd_kernel, nv=nv),
        grid=(bsz, nchunk),
        in_specs=[pl.BlockSpec((1, nv, WIDTH), lambda b, c: (b, c, 0)),
                  pl.BlockSpec((1, nv, CONV_DIM), lambda b, c: (b, c, 0)),
                  pl.BlockSpec((1, q, LANES), lambda b, c: (b, c, 0)),
                  pl.BlockSpec((1, 2 * HEADS, q), lambda b, c: (b, 0, c)),
                  pl.BlockSpec((1, SSD_CONV - 1, CONV_DIM), lambda b, c: (b, 0, 0)),
                  pl.BlockSpec((1, 1, WIDTH, SSD_STATE), lambda b, c: (layer, b, 0, 0))]
                 + [_const_spec(x.shape) for x in consts],
        out_specs=(pl.BlockSpec((1, nv, WIDTH), lambda b, c: (b, c, 0)),
                   pl.BlockSpec((1, SSD_CONV - 1, CONV_DIM), lambda b, c: (b, 0, 0)),
                   pl.BlockSpec((1, WIDTH, SSD_STATE), lambda b, c: (b, 0, 0))),
        out_shape=(jax.ShapeDtypeStruct((bsz, seq, WIDTH), BF16 if nv % 16 == 0 else F32),
                   jax.ShapeDtypeStruct((bsz, SSD_CONV - 1, CONV_DIM), F32),
                   jax.ShapeDtypeStruct((bsz, WIDTH, SSD_STATE), F32)),
        scratch_shapes=[pltpu.VMEM((8 + q, CONV_DIM), F32), pltpu.VMEM((WIDTH, SSD_STATE), F32),
                        pltpu.VMEM((q, WIDTH), F32)],
        compiler_params=_params("parallel", "arbitrary"),
        name="ssd",
    )(z3, xbc3, sm3, smt3, conv0, h0, *consts)


def _merge_kernel(x_ref, ys5_ref, yfox_ref, yssd_ref, wg_ref, bg_ref, ws5_ref, wfox_ref, wssd_ref,
                  wo_ref, g_ref, b_ref, o_ref):
    x = x_ref[...]
    xb = x.astype(BF16)
    merged = None
    for br, (y_ref, w_ref) in enumerate(((ys5_ref, ws5_ref), (yfox_ref, wfox_ref), (yssd_ref, wssd_ref))):
        lo = br * D_MODEL
        gate = _sigmoid(jnp.dot(xb, wg_ref[:, lo:lo + D_MODEL], preferred_element_type=F32)
                        + bg_ref[:, lo:lo + D_MODEL])
        term = gate * jnp.dot(y_ref[...].astype(BF16), w_ref[...], preferred_element_type=F32)
        merged = term if merged is None else merged + term
    out = jnp.dot(merged.astype(BF16), wo_ref[...], preferred_element_type=F32)
    o_ref[...] = _layer_norm(ALPHA * x + out, g_ref[...], b_ref[...])


def _merge(x2, ys5, yfox, yssd, wg, bg, ws5, wfox, wssd, wo, g, b, tm):
    t = x2.shape[0]
    row = lambda w: pl.BlockSpec((tm, w), lambda i: (i, 0))
    consts = [wg, bg, ws5, wfox, wssd, wo, g, b]
    return pl.pallas_call(
        _merge_kernel,
        grid=(t // tm,),
        in_specs=[row(D_MODEL), row(WIDTH), row(WIDTH), row(WIDTH)] + [_const_spec(c.shape) for c in consts],
        out_specs=row(D_MODEL),
        out_shape=jax.ShapeDtypeStruct((t, D_MODEL), F32),
        compiler_params=_params("parallel"),
        name="merge",
    )(x2, ys5, yfox, yssd, *consts)


FF_CHUNK = 256


def _ffn_kernel(x_ref, wg_ref, wu_ref, wd_ref, g_ref, b_ref, o_ref):
    x = x_ref[...]
    xb = x.astype(BF16)
    acc = None
    for c in range(0, D_FF, FF_CHUNK):
        gt = jnp.dot(xb, wg_ref[:, c:c + FF_CHUNK], preferred_element_type=F32)
        up = jnp.dot(xb, wu_ref[:, c:c + FF_CHUNK], preferred_element_type=F32)
        h = (_silu(gt) * up).astype(BF16)
        d = jnp.dot(h, wd_ref[c:c + FF_CHUNK, :], preferred_element_type=F32)
        acc = d if acc is None else acc + d
    o_ref[...] = _layer_norm(ALPHA * x + acc, g_ref[...], b_ref[...])


def _ffn(x2, wg, wu, wd, g, b, tm):
    t = x2.shape[0]
    row = pl.BlockSpec((tm, D_MODEL), lambda i: (i, 0))
    consts = [wg, wu, wd, g, b]
    return pl.pallas_call(
        _ffn_kernel,
        grid=(t // tm,),
        in_specs=[row] + [_const_spec(c.shape) for c in consts],
        out_specs=row,
        out_shape=jax.ShapeDtypeStruct((t, D_MODEL), F32),
        compiler_params=_params("parallel"),
        name="ffn",
    )(x2, *consts)


def _moe_kernel(x_ref, wr_ref, br_ref, wg_ref, wu_ref, wd_ref, g_ref, b_ref, o_ref, gate_sc, acc_sc):
    e = pl.program_id(1)
    x = x_ref[...]
    xb = x.astype(BF16)
    lane = lax.broadcasted_iota(jnp.int32, (x.shape[0], LANES), 1)

    @pl.when(e == 0)
    def _():
        logits = jnp.dot(xb, wr_ref[...], preferred_element_type=F32) + br_ref[...]
        logits = jnp.where(lane < N_EXPERTS, logits, NEG)
        m1 = jnp.max(logits, axis=-1, keepdims=True)
        i1 = jnp.min(jnp.where(logits == m1, lane, LANES), axis=-1, keepdims=True)
        rest = jnp.where(lane == i1, NEG, logits)
        m2 = jnp.max(rest, axis=-1, keepdims=True)
        i2 = jnp.min(jnp.where(rest == m2, lane, LANES), axis=-1, keepdims=True)
        e2 = jnp.exp(m2 - m1)
        w1 = 1.0 / (1.0 + e2)
        w2 = e2 / (1.0 + e2)
        gate_sc[...] = jnp.where(lane == i1, w1, jnp.where(lane == i2, w2, 0.0))
        acc_sc[...] = jnp.zeros(acc_sc.shape, F32)

    gate_e = jnp.sum(jnp.where(lane == e, gate_sc[...], 0.0), axis=-1, keepdims=True)
    gt = jnp.dot(xb, wg_ref[0], preferred_element_type=F32)
    up = jnp.dot(xb, wu_ref[0], preferred_element_type=F32)
    h = (_silu(gt) * up).astype(BF16)
    acc_sc[...] += gate_e * jnp.dot(h, wd_ref[0], preferred_element_type=F32)

    @pl.when(e == N_EXPERTS - 1)
    def _():
        o_ref[...] = _layer_norm(ALPHA * x + acc_sc[...], g_ref[...], b_ref[...])


def _moe(x2, wr, br, wg, wu, wd, g, b, tm):
    t = x2.shape[0]
    row = pl.BlockSpec((tm, D_MODEL), lambda i, e: (i, 0))
    wspec = pl.BlockSpec((1, D_MODEL, D_MODEL), lambda i, e: (e, 0, 0))
    return pl.pallas_call(
        _moe_kernel,
        grid=(t // tm, N_EXPERTS),
        in_specs=[row, _const_spec(wr.shape), _const_spec(br.shape), wspec, wspec, wspec,
                  _const_spec(g.shape), _const_spec(b.shape)],
        out_specs=row,
        out_shape=jax.ShapeDtypeStruct((t, D_MODEL), F32),
        scratch_shapes=[pltpu.VMEM((tm, LANES), F32), pltpu.VMEM((tm, D_MODEL), F32)],
        compiler_params=_params("parallel", "arbitrary"),
        name="moe",
    )(x2, wr, br, wg, wu, wd, g, b)


def _row(v, width=None):
    v = v.reshape(1, -1).astype(F32)
    if width is not None and v.shape[1] < width:
        v = jnp.pad(v, ((0, 0), (0, width - v.shape[1])))
    return v


def _layer_weights(p, l):
    w_in = p['w_in'][l]
    o = np.cumsum([N_BRANCH * D_MODEL, WIDTH, WIDTH, WIDTH, WIDTH, HEADS, WIDTH, CONV_DIM, HEADS])
    gates, u, q, k, v, fg, z, xbc, dtw = (w_in[:, a:b] for a, b in zip([0] + list(o[:-1]), o))
    w = {}
    w['w_main'] = jnp.concatenate([u, q, k, v, z, xbc], axis=1).astype(BF16)
    slab = ((0, 0), (0, 0), (0, LANES - HEAD_DIM))
    k_slab = jnp.pad(k.reshape(D_MODEL, HEADS, HEAD_DIM), slab).reshape(D_MODEL, HEADS * LANES)
    q_slab = jnp.pad(q.reshape(D_MODEL, HEADS, HEAD_DIM), slab).reshape(D_MODEL, HEADS * LANES)
    w['w_rows'] = jnp.concatenate([u, k_slab, z, xbc], axis=1).astype(BF16)
    w['w_cols'] = jnp.concatenate([q_slab, k, v], axis=1).T.astype(BF16)
    e3 = np.zeros((BIAS_LANES, LANES, HEADS * LANES), np.float32)
    for piece in range(BIAS_LANES):
        for h in range(HEADS):
            e3[piece, h, LANES * h + HEAD_DIM + piece] = 1.0
    w['bias_place'] = jnp.asarray(e3, BF16)
    w['w_small'] = jnp.pad(jnp.concatenate([fg, dtw], axis=1), ((0, 0), (0, LANES - 2 * HEADS))).astype(BF16)
    w['b_small'] = _row(jnp.concatenate([p['b_fgate'][l], p['ssd_dt_bias'][l]]), LANES)
    w['w_gates'] = gates.astype(BF16)
    w['b_gates'] = _row(p['b_gate'][l])
    for name in ('w_branch_s5', 'w_branch_fox', 'w_branch_ssd', 'w_o'):
        w[name] = p[name][l].astype(BF16)
    for name in ('ln1_g', 'ln1_b', 'ln2_g', 'ln2_b', 's5_d', 's5_b_glu', 'ssd_conv_b', 'ssd_norm_w'):
        w[name] = _row(p[name][l])
    w['s5_w_glu'] = p['s5_w_glu'][l].astype(BF16)
    w['ssd_conv_w'] = p['ssd_conv_w'][l].astype(F32)
    w['ssd_alog_row'] = _row(jnp.concatenate([jnp.zeros((HEADS,), F32), p['ssd_a_log'][l]]), LANES)
    w['ssd_alog_col'] = jnp.concatenate([jnp.zeros((HEADS,), F32), p['ssd_a_log'][l]]).reshape(2 * HEADS, 1)
    w['ssd_d'] = _row(jnp.repeat(p['ssd_d'][l], HEAD_DIM))
    e = np.zeros((LANES, WIDTH), np.float32)
    for h in range(HEADS):
        e[HEADS + h, HEAD_DIM * h:HEAD_DIM * (h + 1)] = 1.0
    w['ssd_expand'] = jnp.asarray(e, BF16)
    if l % 2 == 0:
        for name in ('ffn_w_gate', 'ffn_w_up', 'ffn_w_down'):
            w[name] = p[name][l // 2].astype(BF16)
    else:
        w['moe_w_router'] = jnp.pad(p['moe_w_router'][l // 2], ((0, 0), (0, LANES - N_EXPERTS))).astype(BF16)
        w['moe_b_router'] = _row(p['moe_b_router'][l // 2], LANES)
        for name in ('moe_w_gate', 'moe_w_up', 'moe_w_down'):
            w[name] = p[name][l // 2].astype(BF16)
    return w


def _s5_q(seq):
    return min(16, seq)


def _run_trunk(x, p, weights, s5_tabs, s5_re0, s5_im0, conv0, ssd0, paged):
    bsz, seq, _ = x.shape
    t = bsz * seq
    tm_mix = min(512, t)
    fox_tile = min(512, seq)
    x2 = x.reshape(t, D_MODEL)
    new = [[] for _ in range(7)]
    q5 = _s5_q(seq)
    ssd0 = ssd0.reshape(ssd0.shape[0], bsz, WIDTH, SSD_STATE)
    if paged is not None:
        cache_k, cache_v, cache_lf, page_table = paged
        cache_kt = cache_k.transpose(0, 1, 3, 4, 2)
        cache_vt = cache_v.transpose(0, 1, 3, 4, 2)
        cache_lft = cache_lf.transpose(0, 1, 3, 2)
    for l in range(len(weights)):
        w = weights[l]
        if paged is None:
            u, ub, z, xbc, small, smt3, kaug, qaugt, kt, vt, tot = _in_proj_prompt(
                x2.reshape(bsz, seq, D_MODEL), w['w_rows'], w['w_cols'], w['w_small'], w['b_small'],
                w['bias_place'], fox_tile)
            k_out = kt.reshape(bsz, HEADS, HEAD_DIM, seq).transpose(0, 3, 1, 2)
            v_out = vt.reshape(bsz, HEADS, HEAD_DIM, seq).transpose(0, 3, 1, 2)
            logf_out = smt3[:, :HEADS, :].transpose(0, 2, 1)
        else:
            u, ub, qb, k, v, z, xbc, small, small_t, logf = _in_proj_sample(
                x2, w['w_main'], w['w_small'], w['b_small'], min(256, t))
            k_out = k.reshape(bsz, seq, HEADS, HEAD_DIM)
            v_out = v.reshape(bsz, seq, HEADS, HEAD_DIM)
            logf_out = logf.reshape(bsz, seq, HEADS)
            smt3 = small_t.reshape(2 * HEADS, bsz, seq).transpose(1, 0, 2)

        tab = s5_tabs[q5][l]
        yi, cs_re, cs_im = _s5_chunks(ub.reshape(t // q5, q5 * WIDTH), tab, q5, min(128, t // q5))
        d_row, w_glu, b_glu = w['s5_d'], w['s5_w_glu'], w['s5_b_glu']
        if paged is None:
            y_s5, s5_re, s5_im = _s5_scan(u.reshape(bsz, seq, WIDTH), yi.reshape(bsz, seq, WIDTH), cs_re, cs_im,
                                          s5_re0[l].reshape(bsz, 1, S5_LANES), s5_im0[l].reshape(bsz, 1, S5_LANES),
                                          tab, d_row, w_glu, b_glu, q5, min(256, seq), True)
        else:
            y_s5, s5_re, s5_im = _s5_scan(u.reshape(1, t, WIDTH), yi.reshape(1, t, WIDTH), cs_re, cs_im,
                                          s5_re0[l].reshape(bsz, S5_LANES), s5_im0[l].reshape(bsz, S5_LANES),
                                          tab, d_row, w_glu, b_glu, q5, min(256, t), False)
        y_s5 = y_s5.reshape(t, WIDTH)
        s5_re = s5_re.reshape(bsz, S5_GROUPS, S5_STATE)
        s5_im = s5_im.reshape(bsz, S5_GROUPS, S5_STATE)

        if paged is None:
            y_fox = _fox_prompt(qaugt, kaug, vt, tot, fox_tile)
        else:
            lfn_t = jnp.pad(logf.reshape(bsz, seq, HEADS).transpose(0, 2, 1),
                            ((0, 0), (0, 0), (0, LANES - seq)))
            y_fox = _fox_sample(qb.astype(F32).reshape(bsz, seq, WIDTH), k.reshape(bsz, seq, WIDTH),
                                v.reshape(bsz, seq, WIDTH), lfn_t, cache_kt, cache_vt, cache_lft, l, page_table)
        y_fox = y_fox.reshape(t, WIDTH)

        nv = math.gcd(seq, SSD_CHUNK)
        sm3 = small.reshape(bsz, seq, LANES)
        if nv < SSD_CHUNK:
            sm3 = jnp.pad(sm3, ((0, 0), (0, SSD_CHUNK - nv), (0, 0)))
            smt3 = jnp.pad(smt3, ((0, 0), (0, 0), (0, SSD_CHUNK - nv)))
        y_ssd, conv_new, ssd_new = _ssd(
            z.reshape(bsz, seq, WIDTH), xbc.reshape(bsz, seq, CONV_DIM), sm3, smt3, conv0[l],
            ssd0, l, w['ssd_conv_w'], w['ssd_conv_b'], w['ssd_alog_row'],
            w['ssd_alog_col'], w['ssd_expand'], w['ssd_d'], w['ssd_norm_w'], nv)
        y_ssd = y_ssd.reshape(t, WIDTH)
        ssd_new = ssd_new.reshape(bsz, HEADS, HEAD_DIM, SSD_STATE)

        x2 = _merge(x2, y_s5, y_fox, y_ssd, w['w_gates'], w['b_gates'], w['w_branch_s5'],
                    w['w_branch_fox'], w['w_branch_ssd'], w['w_o'], w['ln1_g'], w['ln1_b'], tm_mix)
        if l % 2 == 0:
            x2 = _ffn(x2, w['ffn_w_gate'], w['ffn_w_up'], w['ffn_w_down'], w['ln2_g'], w['ln2_b'], tm_mix)
        else:
            x2 = _moe(x2, w['moe_w_router'], w['moe_b_router'], w['moe_w_gate'], w['moe_w_up'],
                      w['moe_w_down'], w['ln2_g'], w['ln2_b'], tm_mix)

        st = (k_out, v_out, logf_out, s5_re, s5_im, conv_new, ssd_new)
        for lst, s in zip(new, st):
            lst.append(s)
    return x2.reshape(bsz, seq, D_MODEL), [jnp.stack(s) for s in new]


def kernel(x_prompt, x_sample, cache_k, cache_v, cache_logf, page_table, state_s5_re, state_s5_im,
           state_conv, state_ssd, w_in, b_gate, b_fgate, s5_lam_re, s5_lam_im, s5_log_dt, s5_b_re,
           s5_b_im, s5_c_re, s5_c_im, s5_d, s5_w_glu, s5_b_glu, ssd_conv_w, ssd_conv_b, ssd_dt_bias,
           ssd_a_log, ssd_d, ssd_norm_w, w_branch_s5, w_branch_fox, w_branch_ssd, w_o, ln1_g, ln1_b,
           ln2_g, ln2_b, ffn_w_gate, ffn_w_up, ffn_w_down, moe_w_router, moe_b_router, moe_w_gate,
           moe_w_up, moe_w_down):
    p = dict(w_in=w_in, b_gate=b_gate, b_fgate=b_fgate, s5_d=s5_d, s5_w_glu=s5_w_glu, s5_b_glu=s5_b_glu,
             ssd_conv_w=ssd_conv_w, ssd_conv_b=ssd_conv_b, ssd_dt_bias=ssd_dt_bias, ssd_a_log=ssd_a_log,
             ssd_d=ssd_d, ssd_norm_w=ssd_norm_w, w_branch_s5=w_branch_s5, w_branch_fox=w_branch_fox,
             w_branch_ssd=w_branch_ssd, w_o=w_o, ln1_g=ln1_g, ln1_b=ln1_b, ln2_g=ln2_g, ln2_b=ln2_b,
             ffn_w_gate=ffn_w_gate, ffn_w_up=ffn_w_up, ffn_w_down=ffn_w_down, moe_w_router=moe_w_router,
             moe_b_router=moe_b_router, moe_w_gate=moe_w_gate, moe_w_up=moe_w_up, moe_w_down=moe_w_down)
    depth = w_in.shape[0]
    weights = [_layer_weights(p, l) for l in range(depth)]
    qs = {_s5_q(x_prompt.shape[1]), _s5_q(x_sample.shape[1])}
    s5_tabs = {q: [_s5_tables(s5_lam_re[l].astype(F32), s5_lam_im[l].astype(F32), s5_log_dt[l].astype(F32),
                              s5_b_re[l].astype(F32), s5_b_im[l].astype(F32), s5_c_re[l].astype(F32),
                              s5_c_im[l].astype(F32), q) for l in range(depth)] for q in qs}
    bsz = x_prompt.shape[0]
    z_s5 = jnp.zeros((depth, bsz, S5_GROUPS, S5_STATE), F32)
    z_conv = jnp.zeros((depth, bsz, SSD_CONV - 1, CONV_DIM), F32)
    z_ssd = jnp.zeros((depth, bsz, HEADS, HEAD_DIM, SSD_STATE), F32)
    y_p, st_p = _run_trunk(x_prompt, p, weights, s5_tabs, z_s5, z_s5, z_conv, z_ssd, None)
    y_s, st_s = _run_trunk(x_sample, p, weights, s5_tabs, state_s5_re, state_s5_im, state_conv, state_ssd,
                           (cache_k, cache_v, cache_logf, page_table))
    return (y_p, y_s, *st_p, *st_s)
```

```python
import functools
import math

import jax
import jax.numpy as jnp
import numpy as np
from jax import lax
from jax.experimental import pallas as pl
from jax.experimental.pallas import tpu as pltpu

F32 = jnp.float32
BF16 = jnp.bfloat16
HIGHEST = lax.Precision.HIGHEST

D_MODEL = 1024
N_BRANCH = 3
WIDTH = 512
S5_GROUPS = 32
S5_GROUP = 16
S5_STATE = 64
S5_LANES = S5_GROUPS * S5_STATE
S5_CHUNK = 16
HEADS = 8
HEAD_DIM = 64
SSD_STATE = 128
SSD_GROUPS = 2
SSD_CONV = 4
SSD_CHUNK = 128
CONV_DIM = 1024
D_FF = 2816
N_EXPERTS = 8
PAGE = 128
ALPHA = 4.0 ** 0.25
LN_EPS = 1e-5
RMS_EPS = 1e-5
NEG = -1e30
LANES = 128
VMEM_LIMIT = 56 * 1024 * 1024

NT_DIMS = (((1,), (1,)), ((), ()))


def _params(*sem):
    return pltpu.CompilerParams(dimension_semantics=sem, vmem_limit_bytes=VMEM_LIMIT)


def _const_spec(shape):
    nd = len(shape)
    return pl.BlockSpec(shape, lambda *_: (0,) * nd, pipeline_mode=pl.Buffered(1))


def _layer_norm(x, g, b):
    mu = jnp.mean(x, axis=-1, keepdims=True)
    xc = x - mu
    var = jnp.mean(xc * xc, axis=-1, keepdims=True)
    return xc * lax.rsqrt(var + LN_EPS) * g + b


def _sigmoid(x):
    return 1.0 / (1.0 + jnp.exp(-x))


def _silu(x):
    return x * _sigmoid(x)


def _bf16_pieces(x):
    hi = x.astype(BF16)
    r1 = x - hi.astype(F32)
    mid = r1.astype(BF16)
    lo = (r1 - mid.astype(F32)).astype(BF16)
    return hi, mid, lo


def _small_act(xb, ws_ref, bs_ref):
    s = jnp.dot(xb, ws_ref[...], preferred_element_type=F32) + bs_ref[...]
    t = jnp.log1p(jnp.exp(-jnp.abs(s)))
    lane = lax.broadcasted_iota(jnp.int32, s.shape, 1)
    return jnp.where(lane < HEADS, jnp.minimum(s, 0.0) - t,
                     jnp.where(lane < 2 * HEADS, jnp.maximum(s, 0.0) + t, 0.0))


SEG_U, SEG_Q, SEG_K, SEG_V, SEG_Z, SEG_X, SEG_END = 0, 512, 1024, 1536, 2048, 2560, 3584


def _in_proj_sample_kernel(x_ref, wm_ref, ws_ref, bs_ref, u_ref, ub_ref, qb_ref, k_ref, v_ref,
                           z_ref, xbc_ref, small_ref, smallt_ref, logf_ref):
    xb = x_ref[...].astype(BF16)

    def seg(a, b):
        return jnp.dot(xb, wm_ref[:, a:b], preferred_element_type=F32)

    u = seg(SEG_U, SEG_Q)
    u_ref[...] = u
    ub_ref[...] = u.astype(BF16)
    qb_ref[...] = (seg(SEG_Q, SEG_K) * (HEAD_DIM ** -0.5)).astype(BF16)
    k_ref[...] = seg(SEG_K, SEG_V)
    v_ref[...] = seg(SEG_V, SEG_Z)
    z_ref[...] = seg(SEG_Z, SEG_X)
    xbc_ref[...] = seg(SEG_X, SEG_END)
    sm = _small_act(xb, ws_ref, bs_ref)
    small_ref[...] = sm
    smallt_ref[...] = sm.T[:2 * HEADS, :]
    logf_ref[...] = sm[:, :HEADS]


def _in_proj_sample(x2, wm, ws, bs, tm):
    t = x2.shape[0]
    row = lambda w: pl.BlockSpec((tm, w), lambda i: (i, 0))
    out_shape = (
        jax.ShapeDtypeStruct((t, WIDTH), F32),
        jax.ShapeDtypeStruct((t, WIDTH), BF16),
        jax.ShapeDtypeStruct((t, WIDTH), BF16),
        jax.ShapeDtypeStruct((t, WIDTH), F32),
        jax.ShapeDtypeStruct((t, WIDTH), F32),
        jax.ShapeDtypeStruct((t, WIDTH), F32),
        jax.ShapeDtypeStruct((t, CONV_DIM), F32),
        jax.ShapeDtypeStruct((t, LANES), F32),
        jax.ShapeDtypeStruct((2 * HEADS, t), F32),
        jax.ShapeDtypeStruct((t, HEADS), F32),
    )
    out_specs = (row(WIDTH), row(WIDTH), row(WIDTH), row(WIDTH), row(WIDTH),
                 row(WIDTH), row(CONV_DIM), row(LANES),
                 pl.BlockSpec((2 * HEADS, tm), lambda i: (0, i)), row(HEADS))
    return pl.pallas_call(
        _in_proj_sample_kernel,
        grid=(t // tm,),
        in_specs=[row(D_MODEL), _const_spec(wm.shape), _const_spec(ws.shape), _const_spec(bs.shape)],
        out_specs=out_specs,
        out_shape=out_shape,
        compiler_params=_params("parallel"),
        name="in_proj_sample",
    )(x2, wm, ws, bs)


PSEG_U, PSEG_K, PSEG_Z, PSEG_X, PSEG_END = 0, 512, 1536, 2048, 3072
TSEG_Q, TSEG_K, TSEG_V, TSEG_END = 0, 1024, 1536, 2048
BIAS_LANES = 3


def _in_proj_prompt_kernel(x_ref, wr_ref, wt_ref, ws_ref, bs_ref, e_ref, u_ref, ub_ref, z_ref, xbc_ref,
                           small_ref, smallt_ref, kaug_ref, qaugt_ref, kt_ref, vt_ref, tot_ref, u_sc):
    tm = x_ref.shape[0]
    xb = x_ref[...].astype(BF16)
    u = jnp.dot(xb, wr_ref[:, PSEG_U:PSEG_K], preferred_element_type=F32)
    u_ref[...] = u
    for k in range(WIDTH // LANES):
        u_sc[k] = u[:, LANES * k:LANES * (k + 1)]
    for s in range(S5_CHUNK):
        for k in range(WIDTH // LANES):
            lo_lane = s * WIDTH + LANES * k
            ub_ref[:, lo_lane:lo_lane + LANES] = u_sc[k, pl.ds(s, tm // S5_CHUNK, stride=S5_CHUNK), :].astype(BF16)
    z_ref[...] = jnp.dot(xb, wr_ref[:, PSEG_Z:PSEG_X], preferred_element_type=F32)
    xbc_ref[...] = jnp.dot(xb, wr_ref[:, PSEG_X:PSEG_END], preferred_element_type=F32)
    sm = _small_act(xb, ws_ref, bs_ref)
    small_ref[...] = sm
    smt = sm.T[:2 * HEADS, :]
    smallt_ref[0] = smt
    r_i = lax.broadcasted_iota(jnp.int32, (tm, tm), 0)
    c_i = lax.broadcasted_iota(jnp.int32, (tm, tm), 1)
    tri = (r_i >= c_i).astype(BF16)
    ones = jnp.ones((tm, tm), BF16)
    cloc = None
    tot = None
    for piece, piece_t in zip(_bf16_pieces(sm), _bf16_pieces(smt)):
        d = jnp.dot(tri, piece, preferred_element_type=F32)
        dt = jnp.dot(piece_t, ones, preferred_element_type=F32)
        cloc = d if cloc is None else cloc + d
        tot = dt if tot is None else tot + dt
    tot_ref[0, 0] = tot
    hi, mid, lo = _bf16_pieces(-cloc)
    for h in range(HEADS):
        cols = slice(PSEG_K + LANES * h, PSEG_K + LANES * (h + 1))
        ecol = slice(LANES * h, LANES * (h + 1))
        kh = jnp.dot(xb, wr_ref[:, cols], preferred_element_type=F32)
        kh = (kh + jnp.dot(hi, e_ref[0, :, ecol], preferred_element_type=F32)
              + jnp.dot(mid, e_ref[1, :, ecol], preferred_element_type=F32)
              + jnp.dot(lo, e_ref[2, :, ecol], preferred_element_type=F32))
        kaug_ref[0, h] = kh.astype(BF16)
    tall = lax.dot_general(wt_ref[...], xb, NT_DIMS, preferred_element_type=F32)
    row = lax.broadcasted_iota(jnp.int32, (LANES, tm), 0)
    ones_rows = (row >= HEAD_DIM) & (row < HEAD_DIM + BIAS_LANES)
    for h in range(HEADS):
        qh = tall[TSEG_Q + LANES * h:TSEG_Q + LANES * (h + 1), :]
        qaugt_ref[0, h] = jnp.where(ones_rows, 1.0, qh * (HEAD_DIM ** -0.5)).astype(BF16)
    kt_ref[0] = tall[TSEG_K:TSEG_V, :]
    vt_ref[0] = tall[TSEG_V:TSEG_END, :]


def _in_proj_prompt(x3, wr, wt, ws, bs, e3, tm):
    bsz, seq, _ = x3.shape
    nt = seq // tm
    x2 = x3.reshape(bsz * seq, D_MODEL)
    row = lambda w: pl.BlockSpec((tm, w), lambda b, i: (b * nt + i, 0))
    colt = lambda r: pl.BlockSpec((1, r, tm), lambda b, i: (b, 0, i))
    t = bsz * seq
    out_shape = (
        jax.ShapeDtypeStruct((t, WIDTH), F32),
        jax.ShapeDtypeStruct((t // S5_CHUNK, S5_CHUNK * WIDTH), BF16),
        jax.ShapeDtypeStruct((t, WIDTH), F32),
        jax.ShapeDtypeStruct((t, CONV_DIM), F32),
        jax.ShapeDtypeStruct((t, LANES), F32),
        jax.ShapeDtypeStruct((bsz, 2 * HEADS, seq), F32),
        jax.ShapeDtypeStruct((bsz, HEADS, seq, LANES), BF16),
        jax.ShapeDtypeStruct((bsz, HEADS, LANES, seq), BF16),
        jax.ShapeDtypeStruct((bsz, WIDTH, seq), F32),
        jax.ShapeDtypeStruct((bsz, WIDTH, seq), F32),
        jax.ShapeDtypeStruct((bsz, nt, 2 * HEADS, tm), F32),
    )
    out_specs = (row(WIDTH),
                 pl.BlockSpec((tm // S5_CHUNK, S5_CHUNK * WIDTH), lambda b, i: (b * nt + i, 0)),
                 row(WIDTH), row(CONV_DIM), row(LANES), colt(2 * HEADS),
                 pl.BlockSpec((1, HEADS, tm, LANES), lambda b, i: (b, 0, i, 0)),
                 pl.BlockSpec((1, HEADS, LANES, tm), lambda b, i: (b, 0, 0, i)),
                 colt(WIDTH), colt(WIDTH),
                 pl.BlockSpec((1, 1, 2 * HEADS, tm), lambda b, i: (b, i, 0, 0)))
    consts = [wr, wt, ws, bs, e3]
    return pl.pallas_call(
        _in_proj_prompt_kernel,
        grid=(bsz, nt),
        in_specs=[row(D_MODEL)] + [_const_spec(c.shape) for c in consts],
        out_specs=out_specs,
        out_shape=out_shape,
        scratch_shapes=[pltpu.VMEM((WIDTH // LANES, tm, LANES), F32)],
        compiler_params=_params("parallel", "parallel"),
        name="in_proj_prompt",
    )(x2, *consts)


def _cumsum_lanes(x):
    n = x.shape[-1]
    lane = lax.broadcasted_iota(jnp.int32, x.shape, x.ndim - 1)
    s = 1
    while s < n:
        x = x + jnp.where(lane >= s, pltpu.roll(x, s, axis=x.ndim - 1), 0.0)
        s *= 2
    return x


FOX_HEADS_PER_STEP = 8


def _fox_prompt_kernel(it_ref, jt_ref, q_ref, k_ref, v_ref, tot_ref, o_ref, m_sc, l_sc, c_sc, acc_sc, *, tq, tk):
    g = pl.program_id(1)
    t = pl.program_id(2)
    i = it_ref[t]
    j = jt_ref[t]
    hps = FOX_HEADS_PER_STEP

    @pl.when(j == 0)
    def _():
        m_sc[...] = jnp.full(m_sc.shape, NEG, F32)
        l_sc[...] = jnp.zeros(l_sc.shape, F32)
        c_sc[...] = jnp.zeros(c_sc.shape, F32)
        acc_sc[...] = jnp.zeros(acc_sc.shape, F32)

    def step(masked):
        if masked:
            kpos = j * tk + lax.broadcasted_iota(jnp.int32, (tk, tq), 0)
            qpos = i * tq + lax.broadcasted_iota(jnp.int32, (tk, tq), 1)
            causal = kpos <= qpos
        for h in range(hps):
            s = jnp.dot(k_ref[0, h], q_ref[0, h], preferred_element_type=F32)
            if masked:
                s = jnp.where(causal, s, NEG)
            c_j = c_sc[h]
            m_prev = m_sc[h]
            m_new = jnp.maximum(m_prev, jnp.max(s, axis=0, keepdims=True) - c_j)
            alpha = jnp.exp(m_prev - m_new)
            p = jnp.exp(s - (m_new + c_j))
            l_sc[h] = alpha * l_sc[h] + jnp.sum(p, axis=0, keepdims=True)
            rows = slice(HEAD_DIM * h, HEAD_DIM * (h + 1))
            acc_sc[rows, :] = alpha * acc_sc[rows, :] + jnp.dot(
                v_ref[0, rows, :].astype(BF16), p.astype(BF16), preferred_element_type=F32)
            m_sc[h] = m_new
            c_sc[h] = c_j + tot_ref[0, 0, pl.ds(g * hps + h, 1), :]

    @pl.when(j < i)
    def _():
        step(False)

    @pl.when(j == i)
    def _():
        step(True)
        inv = jnp.concatenate([jnp.broadcast_to(1.0 / l_sc[h], (HEAD_DIM, tq)) for h in range(hps)], axis=0)
        o_ref[0] = (acc_sc[...] * inv).T.astype(o_ref.dtype)


def _fox_prompt(qaugt, kaug, vt, tot, tq):
    bsz, _, _, seq = qaugt.shape
    nq = seq // tq
    hps = FOX_HEADS_PER_STEP
    it = np.array([i for i in range(nq) for j in range(i + 1)], np.int32)
    jt = np.array([j for i in range(nq) for j in range(i + 1)], np.int32)
    grid_spec = pltpu.PrefetchScalarGridSpec(
        num_scalar_prefetch=2,
        grid=(bsz, HEADS // hps, len(it)),
        in_specs=[
            pl.BlockSpec((1, hps, LANES, tq), lambda b, g, t, it, jt: (b, g, 0, it[t])),
            pl.BlockSpec((1, hps, tq, LANES), lambda b, g, t, it, jt: (b, g, jt[t], 0)),
            pl.BlockSpec((1, hps * HEAD_DIM, tq), lambda b, g, t, it, jt: (b, g, jt[t])),
            pl.BlockSpec((1, 1, 2 * HEADS, tq), lambda b, g, t, it, jt: (b, jt[t], 0, 0)),
        ],
        out_specs=pl.BlockSpec((1, tq, hps * HEAD_DIM), lambda b, g, t, it, jt: (b, it[t], g)),
        scratch_shapes=[pltpu.VMEM((hps, 1, tq), F32), pltpu.VMEM((hps, 1, tq), F32),
                        pltpu.VMEM((hps, 1, tq), F32), pltpu.VMEM((hps * HEAD_DIM, tq), F32)],
    )
    return pl.pallas_call(
        functools.partial(_fox_prompt_kernel, tq=tq, tk=tq),
        grid_spec=grid_spec,
        out_shape=jax.ShapeDtypeStruct((bsz, seq, WIDTH), BF16),
        compiler_params=_params("parallel", "parallel", "arbitrary"),
        name="fox_prompt",
    )(jnp.asarray(it), jnp.asarray(jt), qaugt, kaug, vt, tot)


def _fox_sample_kernel(pt_ref, q_ref, kn_ref, vn_ref, lfn_ref, *refs, n_pages, tq):
    k_refs = refs[:n_pages]
    v_refs = refs[n_pages:2 * n_pages]
    lf_ref = refs[2 * n_pages]
    o_ref = refs[2 * n_pages + 1]
    b = pl.program_id(0)
    rows = tq * HEADS
    r_i = lax.broadcasted_iota(jnp.int32, (rows, tq), 0)
    t_i = lax.broadcasted_iota(jnp.int32, (rows, tq), 1)
    rep = (r_i // HEADS == t_i).astype(F32)
    q = q_ref[0].astype(F32)
    qe = jnp.dot(rep, q, precision=HIGHEST, preferred_element_type=F32)
    r_w = lax.broadcasted_iota(jnp.int32, (rows, WIDTH), 0)
    l_w = lax.broadcasted_iota(jnp.int32, (rows, WIDTH), 1)
    head_mask = (l_w // HEAD_DIM) == (r_w % HEADS)
    qbd = jnp.where(head_mask, qe, 0.0).astype(BF16)

    scores = [None] * n_pages
    carry = jnp.zeros((HEADS, 1), F32)
    for p in range(n_pages - 1, -1, -1):
        cs = _cumsum_lanes(lf_ref[0, pt_ref[b * n_pages + p]])
        tot = cs[:, PAGE - 1:PAGE]
        dec = (carry + tot) - cs
        carry = carry + tot
        kt = k_refs[p][0, 0].reshape(WIDTH, PAGE).astype(BF16)
        s = jnp.dot(qbd, kt, preferred_element_type=F32)
        scores[p] = s + jnp.concatenate([dec] * tq, axis=0)
    pad = jnp.zeros((PAGE - tq, WIDTH), F32)
    kn = jnp.concatenate([kn_ref[0], pad], axis=0).astype(BF16)
    vn = jnp.concatenate([vn_ref[0], pad], axis=0).astype(BF16)
    m_i = lax.broadcasted_iota(jnp.int32, (PAGE, PAGE), 0)
    c_i = lax.broadcasted_iota(jnp.int32, (PAGE, PAGE), 1)
    cn = jnp.dot(lfn_ref[0], (m_i <= c_i).astype(F32), precision=HIGHEST,
                 preferred_element_type=F32)
    r_p = lax.broadcasted_iota(jnp.int32, (rows, PAGE), 0)
    c_p = lax.broadcasted_iota(jnp.int32, (rows, PAGE), 1)
    s_new = lax.dot_general(qbd, kn, NT_DIMS, preferred_element_type=F32)
    s_new = jnp.where(c_p <= r_p // HEADS, s_new - jnp.concatenate([cn] * tq, axis=0), NEG)

    m = s_new
    for p in range(n_pages):
        m = jnp.maximum(m, scores[p])
    m = jnp.max(m, axis=-1, keepdims=True)
    pn = jnp.exp(s_new - m)
    l = jnp.sum(pn, axis=-1, keepdims=True)
    o = jnp.dot(pn.astype(BF16), vn, preferred_element_type=F32)
    for p in range(n_pages):
        pp = jnp.exp(scores[p] - m)
        l = l + jnp.sum(pp, axis=-1, keepdims=True)
        vt = v_refs[p][0, 0].reshape(WIDTH, PAGE).astype(BF16)
        o = o + lax.dot_general(pp.astype(BF16), vt, NT_DIMS, preferred_element_type=F32)
    om = jnp.where(head_mask, o / l, 0.0)
    rep_t = (lax.broadcasted_iota(jnp.int32, (tq, rows), 1) // HEADS
             == lax.broadcasted_iota(jnp.int32, (tq, rows), 0)).astype(F32)
    o_ref[0] = jnp.dot(rep_t, om, precision=HIGHEST, preferred_element_type=F32).astype(o_ref.dtype)


def _fox_sample(qb, k_new, v_new, lfn_t, cache_kt, cache_vt, cache_lft, layer, page_table):
    dbsz, tq, _ = qb.shape
    n_pages = page_table.shape[1]
    pt = page_table.reshape(-1).astype(jnp.int32)

    def page_spec(p, shape):
        zeros = (0,) * (len(shape) - 2)
        return pl.BlockSpec(shape, lambda b, pt: (layer, pt[b * n_pages + p]) + zeros)

    seq_spec = lambda shape: pl.BlockSpec(shape, lambda b, pt: (b, 0, 0))
    in_specs = [seq_spec((1, tq, WIDTH)), seq_spec((1, tq, WIDTH)), seq_spec((1, tq, WIDTH)),
                seq_spec((1, HEADS, LANES))]
    in_specs += [page_spec(p, (1, 1, HEADS, HEAD_DIM, PAGE)) for p in range(n_pages)]
    in_specs += [page_spec(p, (1, 1, HEADS, HEAD_DIM, PAGE)) for p in range(n_pages)]
    in_specs += [pl.BlockSpec((1,) + cache_lft.shape[1:], lambda b, pt: (layer, 0, 0, 0),
                              pipeline_mode=pl.Buffered(1))]
    grid_spec = pltpu.PrefetchScalarGridSpec(
        num_scalar_prefetch=1, grid=(dbsz,), in_specs=in_specs,
        out_specs=pl.BlockSpec((1, tq, WIDTH), lambda b, pt: (b, 0, 0)))
    return pl.pallas_call(
        functools.partial(_fox_sample_kernel, n_pages=n_pages, tq=tq),
        grid_spec=grid_spec,
        out_shape=jax.ShapeDtypeStruct((dbsz, tq, WIDTH), F32),
        compiler_params=_params("parallel"),
        name="fox_sample",
    )(pt, qb, k_new, v_new, lfn_t, *([cache_kt] * n_pages), *([cache_vt] * n_pages), cache_lft)


def _s5_tables(lam_re, lam_im, log_dt, b_re, b_im, c_re, c_im, q):
    hp = dict(precision=HIGHEST)
    dt = jnp.exp(log_dt)[:, None]
    mag = jnp.exp(lam_re * dt)
    ab_re = mag * jnp.cos(lam_im * dt)
    ab_im = mag * jnp.sin(lam_im * dt)
    nr = ab_re - 1.0
    den = lam_re * lam_re + lam_im * lam_im
    q_re = (nr * lam_re + ab_im * lam_im) / den
    q_im = (ab_im * lam_re - nr * lam_im) / den
    bb_re = q_re[..., None] * b_re - q_im[..., None] * b_im
    bb_im = q_re[..., None] * b_im + q_im[..., None] * b_re
    jj = jnp.arange(q + 1, dtype=F32)[:, None, None]
    pmag = jnp.exp(lam_re * dt * jj)
    p_re = pmag * jnp.cos(lam_im * dt * jj)
    p_im = pmag * jnp.sin(lam_im * dt * jj)
    ab_b_re = p_re[:q, :, :, None] * bb_re - p_im[:q, :, :, None] * bb_im
    ab_b_im = p_re[:q, :, :, None] * bb_im + p_im[:q, :, :, None] * bb_re
    kern = (jnp.einsum('gcn,jgnd->jgcd', c_re, ab_b_re, **hp)
            - jnp.einsum('gcn,jgnd->jgcd', c_im, ab_b_im, **hp))
    eye16 = jnp.eye(16, dtype=F32)
    kt = kern.transpose(0, 1, 3, 2).reshape(q, 2, 16, S5_GROUP, S5_GROUP)
    w_toep = jnp.einsum('jxgab,gh->jxgahb', kt, eye16).reshape(q, 2, 256, 256).astype(BF16)
    eye_g = jnp.eye(S5_GROUPS, dtype=F32)

    def in_map(bb):
        full = jnp.einsum('gnc,gh->gchn', bb, eye_g).reshape(WIDTH, S5_LANES)
        return jnp.stack([full[128 * (n // 2):128 * (n // 2) + 128, 256 * n:256 * n + 256]
                          for n in range(8)]).astype(BF16)

    def out_map(c):
        full = jnp.einsum('gcn,gh->gnhc', c, eye_g).reshape(S5_LANES, WIDTH)
        return jnp.stack([full[512 * m:512 * m + 512, 128 * m:128 * m + 128]
                          for m in range(4)]).astype(BF16)

    rows = ((q + 1 + 7) // 8) * 8
    padp = lambda p: jnp.pad(p.reshape(q + 1, S5_LANES), ((0, rows - q - 1), (0, 0)))
    return dict(w_toep=w_toep, wb_re=in_map(bb_re), wb_im=in_map(bb_im), wc_re=out_map(c_re),
                wc_im=out_map(-c_im), p_re=padp(p_re), p_im=padp(p_im))


def _all_s5_tables(lam_re, lam_im, log_dt, b_re, b_im, c_re, c_im, qs):
    f32 = lambda a: a.astype(F32)
    q_max = max(qs)
    full = jax.vmap(functools.partial(_s5_tables, q=q_max))(
        f32(lam_re), f32(lam_im), f32(log_dt), f32(b_re), f32(b_im), f32(c_re), f32(c_im))
    out = {}
    for q in qs:
        rows = ((q + 1 + 7) // 8) * 8
        out[q] = []
        for l in range(lam_re.shape[0]):
            tab = {name: arr[l] for name, arr in full.items()}
            tab['w_toep'] = tab['w_toep'][:q]
            tab['p_re'] = tab['p_re'][:rows]
            tab['p_im'] = tab['p_im'][:rows]
            out[q].append(tab)
    return out


def _s5_chunk_kernel(u_ref, w_ref, wbr_ref, wbi_ref, pr_ref, pi_ref, y_ref, sr_ref, si_ref, *, q):
    for s_out in range(q):
        for half in range(2):
            acc = None
            for j in range(s_out + 1):
                a = (s_out - j) * WIDTH + half * 256
                d = jnp.dot(u_ref[:, a:a + 256], w_ref[j, half], preferred_element_type=F32)
                acc = d if acc is None else acc + d
            o = s_out * WIDTH + half * 256
            y_ref[:, o:o + 256] = acc
    for n in range(8):
        lanes = slice(256 * n, 256 * (n + 1))
        acc_r = None
        acc_i = None
        for s in range(q):
            a = s * WIDTH + 128 * (n // 2)
            br = jnp.dot(u_ref[:, a:a + 128], wbr_ref[n], preferred_element_type=F32)
            bi = jnp.dot(u_ref[:, a:a + 128], wbi_ref[n], preferred_element_type=F32)
            pr = pr_ref[q - 1 - s:q - s, lanes]
            pi = pi_ref[q - 1 - s:q - s, lanes]
            tr = pr * br - pi * bi
            ti = pr * bi + pi * br
            acc_r = tr if acc_r is None else acc_r + tr
            acc_i = ti if acc_i is None else acc_i + ti
        sr_ref[:, lanes] = acc_r
        si_ref[:, lanes] = acc_i


def _s5_chunks(u2b, tab, q, rt):
    nc = u2b.shape[0]
    consts = [tab['w_toep'], tab['wb_re'], tab['wb_im'], tab['p_re'], tab['p_im']]
    state = pl.BlockSpec((rt, S5_LANES), lambda i: (i, 0))
    return pl.pallas_call(
        functools.partial(_s5_chunk_kernel, q=q),
        grid=(nc // rt,),
        in_specs=[pl.BlockSpec((rt, q * WIDTH), lambda i: (i, 0))] + [_const_spec(c.shape) for c in consts],
        out_specs=(pl.BlockSpec((rt, q * WIDTH), lambda i: (i, 0)), state, state),
        out_shape=(jax.ShapeDtypeStruct((nc, q * WIDTH), F32),
                   jax.ShapeDtypeStruct((nc, S5_LANES), F32), jax.ShapeDtypeStruct((nc, S5_LANES), F32)),
        compiler_params=_params("parallel"),
        name="s5_chunks",
    )(u2b, *consts)


def _s5_scan_kernel(u_ref, yi_ref, sre_ref, sim_ref, h0r_ref, h0i_ref, wcr_ref, wci_ref, pr_ref, pi_ref,
                    d_ref, wg_ref, bg_ref, y_ref, htr_ref, hti_ref,
                    xcr_sc, xci_sc, hr_sc, hi_sc, yi_sc, *, q, nct, carry):
    tile = pl.program_id(1)
    u = u_ref[0]
    aq_r = pr_ref[q:q + 1, :]
    aq_i = pi_ref[q:q + 1, :]
    p1r = pr_ref[1:q + 1, :]
    p1i = pi_ref[1:q + 1, :]

    if carry:
        @pl.when(tile == 0)
        def _():
            hr_sc[...] = h0r_ref[0]
            hi_sc[...] = h0i_ref[0]

        def body(c, h):
            hr, hi = h
            row0 = pl.multiple_of(c * q, q)
            xcr_sc[pl.ds(row0, q), :] = p1r * hr - p1i * hi
            xci_sc[pl.ds(row0, q), :] = p1r * hi + p1i * hr
            sr = sre_ref[pl.ds(c, 1), :]
            si = sim_ref[pl.ds(c, 1), :]
            return aq_r * hr - aq_i * hi + sr, aq_r * hi + aq_i * hr + si

        hr, hi = lax.fori_loop(0, nct, body, (hr_sc[...], hi_sc[...]))
        hr_sc[...] = hr
        hi_sc[...] = hi

        @pl.when(tile == pl.num_programs(1) - 1)
        def _():
            htr_ref[0] = hr
            hti_ref[0] = hi
    else:
        def body(c, _):
            hr = h0r_ref[pl.ds(c, 1), :]
            hi = h0i_ref[pl.ds(c, 1), :]
            row0 = pl.multiple_of(c * q, q)
            xcr_sc[pl.ds(row0, q), :] = p1r * hr - p1i * hi
            xci_sc[pl.ds(row0, q), :] = p1r * hi + p1i * hr
            return 0

        lax.fori_loop(0, nct, body, 0)
        h0r = h0r_ref[...]
        h0i = h0i_ref[...]
        htr_ref[...] = aq_r * h0r - aq_i * h0i + sre_ref[...]
        hti_ref[...] = aq_r * h0i + aq_i * h0r + sim_ref[...]

    cols = []
    for m in range(4):
        xr = xcr_sc[:, 512 * m:512 * m + 512].astype(BF16)
        xi = xci_sc[:, 512 * m:512 * m + 512].astype(BF16)
        cols.append(jnp.dot(xr, wcr_ref[m], preferred_element_type=F32)
                    + jnp.dot(xi, wci_ref[m], preferred_element_type=F32))
    for s in range(q):
        for k in range(WIDTH // LANES):
            lo_lane = s * WIDTH + LANES * k
            yi_sc[k, pl.ds(s, nct, stride=q), :] = yi_ref[:, lo_lane:lo_lane + LANES]
    yi = jnp.concatenate([yi_sc[k] for k in range(WIDTH // LANES)], axis=-1)
    y = yi + jnp.concatenate(cols, axis=-1) + d_ref[...] * u
    y = jax.nn.gelu(y)
    g = jnp.dot(y.astype(BF16), wg_ref[...], preferred_element_type=F32) + bg_ref[...]
    y_ref[0] = (y * _sigmoid(g)).astype(y_ref.dtype)


def _s5_scan(u3, yi2, s_re, s_im, h0_re, h0_im, tab, d_row, w_glu, b_glu, q, tm, carry):
    bsz, seq, _ = u3.shape
    nct = tm // q
    n_tiles = seq // tm
    tok = pl.BlockSpec((1, tm, WIDTH), lambda b, t: (b, t, 0))
    chunk = pl.BlockSpec((nct, S5_LANES), lambda b, t: (b * n_tiles + t, 0))
    chunk_y = pl.BlockSpec((nct, q * WIDTH), lambda b, t: (b * n_tiles + t, 0))
    if carry:
        h_spec = pl.BlockSpec((1, 1, S5_LANES), lambda b, t: (b, 0, 0))
        h_shape = jax.ShapeDtypeStruct((bsz, 1, S5_LANES), F32)
    else:
        h_spec = pl.BlockSpec((nct, S5_LANES), lambda b, t: (t, 0))
        h_shape = jax.ShapeDtypeStruct((seq // q, S5_LANES), F32)
    consts = [tab['wc_re'], tab['wc_im'], tab['p_re'], tab['p_im'], d_row, w_glu, b_glu]
    return pl.pallas_call(
        functools.partial(_s5_scan_kernel, q=q, nct=nct, carry=carry),
        grid=(bsz, n_tiles),
        in_specs=[tok, chunk_y, chunk, chunk, h_spec, h_spec] + [_const_spec(c.shape) for c in consts],
        out_specs=(tok, h_spec, h_spec),
        out_shape=(jax.ShapeDtypeStruct((bsz, seq, WIDTH), BF16), h_shape, h_shape),
        scratch_shapes=[pltpu.VMEM((tm, S5_LANES), F32), pltpu.VMEM((tm, S5_LANES), F32),
                        pltpu.VMEM((1, S5_LANES), F32), pltpu.VMEM((1, S5_LANES), F32),
                        pltpu.VMEM((WIDTH // LANES, tm, LANES), F32)],
        compiler_params=_params("parallel", "arbitrary"),
        name="s5_scan",
    )(u3, yi2, s_re, s_im, h0_re, h0_im, *consts)


def _ssd_kernel(z_ref, xbc_ref, sm_ref, smt_ref, conv0_ref, h0_ref, cw_ref, cb_ref, alog_row_ref,
                alog_col_ref, e_ref, d_ref, nw_ref, y_ref, convt_ref, ht_ref,
                xp_sc, st_sc, zp_sc, *, nv):
    q = SSD_CHUNK
    c = pl.program_id(1)
    last = c == pl.num_programs(1) - 1

    @pl.when(c == 0)
    def _():
        st_sc[...] = h0_ref[0, 0]
        xp_sc[...] = jnp.zeros(xp_sc.shape, F32)
        xp_sc[8 - (SSD_CONV - 1):8, :] = conv0_ref[0]
        if nv < q:
            zp_sc[...] = jnp.zeros(zp_sc.shape, F32)

    xp_sc[8:8 + nv, :] = xbc_ref[0]
    acc = None
    for j in range(SSD_CONV):
        o = 8 - (SSD_CONV - 1) + j
        term = xp_sc[o:o + q, :] * cw_ref[j:j + 1, :]
        acc = term if acc is None else acc + term
    xc = _silu(acc + cb_ref[...])
    tail = xp_sc[8 + nv - (SSD_CONV - 1):8 + nv, :]

    @pl.when(last)
    def _():
        convt_ref[0] = tail

    xp_sc[8 - (SSD_CONV - 1):8, :] = tail
    if nv < q:
        zp_sc[0:nv, :] = z_ref[0]
        z = zp_sc[...]
    else:
        z = z_ref[0]

    xs = xc[:, :WIDTH]
    bm = xc[:, WIDTH:WIDTH + SSD_GROUPS * SSD_STATE]
    cm = xc[:, WIDTH + SSD_GROUPS * SSD_STATE:]

    sm = sm_ref[0]
    smt = smt_ref[0]
    lane = lax.broadcasted_iota(jnp.int32, (1, LANES), 1)
    a_row = jnp.where((lane >= HEADS) & (lane < 2 * HEADS), -jnp.exp(alog_row_ref[...]), 0.0)
    rowi = lax.broadcasted_iota(jnp.int32, (2 * HEADS, 1), 0)
    a_col = jnp.where(rowi >= HEADS, -jnp.exp(alog_col_ref[...]), 0.0)
    adt = sm * a_row
    adt_t = smt * a_col
    r_i = lax.broadcasted_iota(jnp.int32, (q, q), 0)
    c_i = lax.broadcasted_iota(jnp.int32, (q, q), 1)
    tri = r_i >= c_i
    tri_lo = tri.astype(BF16)
    tri_up = (r_i <= c_i).astype(BF16)
    acum = None
    acum_t = None
    for piece, piece_t in zip(_bf16_pieces(adt), _bf16_pieces(adt_t)):
        d = jnp.dot(tri_lo, piece, preferred_element_type=F32)
        dt = jnp.dot(piece_t, tri_up, preferred_element_type=F32)
        acum = d if acum is None else acum + d
        acum_t = dt if acum_t is None else acum_t + dt
    atot = acum[q - 1:q, :]
    stacked = jnp.concatenate([sm, jnp.exp(acum), jnp.exp(atot - acum)], axis=0)
    hi, mid, lo = _bf16_pieces(stacked)
    e = e_ref[...]
    expanded = (jnp.dot(hi, e, preferred_element_type=F32) + jnp.dot(mid, e, preferred_element_type=F32)
                + jnp.dot(lo, e, preferred_element_type=F32))
    dt_e = expanded[0:q]
    eac_e = expanded[q:2 * q]
    dec_e = expanded[2 * q:3 * q]
    xd = xs * dt_e
    xdd = xd * dec_e
    lane_half = lax.broadcasted_iota(jnp.int32, (q, LANES), 1) // HEAD_DIM

    y_cols = []
    for g in range(SSD_GROUPS):
        bg = bm[:, SSD_STATE * g:SSD_STATE * (g + 1)].astype(BF16)
        cg = cm[:, SSD_STATE * g:SSD_STATE * (g + 1)].astype(BF16)
        cb = lax.dot_general(cg, bg, NT_DIMS, preferred_element_type=F32)
        hpg = HEADS // SSD_GROUPS
        st_g = st_sc[hpg * HEAD_DIM * g:hpg * HEAD_DIM * (g + 1), :]
        y_off = lax.dot_general(cg, st_g.astype(BF16), NT_DIMS, preferred_element_type=F32)
        for pair in range(hpg // 2):
            ys = []
            for hh in range(2):
                h = hpg * g + 2 * pair + hh
                diff = acum[:, HEADS + h:HEADS + h + 1] - acum_t[HEADS + h:HEADS + h + 1, :]
                lmat = jnp.exp(jnp.where(tri, diff, NEG))
                w = (cb * lmat).astype(BF16)
                lo = LANES * (2 * g + pair)
                ys.append(jnp.dot(w, xd[:, lo:lo + LANES].astype(BF16), preferred_element_type=F32))
            y_cols.append(jnp.where(lane_half == 0, ys[0], ys[1]))
        y_cols[-2] = y_cols[-2] + y_off[:, :LANES] * eac_e[:, 256 * g:256 * g + LANES]
        y_cols[-1] = y_cols[-1] + y_off[:, LANES:] * eac_e[:, 256 * g + LANES:256 * (g + 1)]
        contrib = jnp.dot(xdd[:, 256 * g:256 * (g + 1)].T.astype(BF16), bg, preferred_element_type=F32)
        for hl in range(hpg):
            h = hpg * g + hl
            sl = slice(HEAD_DIM * h, HEAD_DIM * (h + 1))
            dec_h = jnp.exp(acum_t[HEADS + h:HEADS + h + 1, q - 1:q])
            st_sc[sl, :] = st_sc[sl, :] * dec_h + contrib[HEAD_DIM * hl:HEAD_DIM * (hl + 1), :]

    y = jnp.concatenate(y_cols, axis=-1) + d_ref[...] * xs
    y = y * _silu(z)
    y = y * lax.rsqrt(jnp.mean(y * y, axis=-1, keepdims=True) + RMS_EPS) * nw_ref[...]
    y_ref[0] = y[:nv].astype(y_ref.dtype)

    @pl.when(last)
    def _():
        ht_ref[0] = st_sc[...]


def _ssd(z3, xbc3, sm3, smt3, conv0, h0, layer, cw, cb, alog_row, alog_col, e_mat, d_row, nw, nv):
    bsz, seq, _ = z3.shape
    nchunk = seq // nv
    q = SSD_CHUNK
    consts = [cw, cb, alog_row, alog_col, e_mat, d_row, nw]
    return pl.pallas_call(
        functools.partial(_ssd_kernel, nv=nv),
        grid=(bsz, nchunk),
        in_specs=[pl.BlockSpec((1, nv, WIDTH), lambda b, c: (b, c, 0)),
                  pl.BlockSpec((1, nv, CONV_DIM), lambda b, c: (b, c, 0)),
                  pl.BlockSpec((1, q, LANES), lambda b, c: (b, c, 0)),
                  pl.BlockSpec((1, 2 * HEADS, q), lambda b, c: (b, 0, c)),
                  pl.BlockSpec((1, SSD_CONV - 1, CONV_DIM), lambda b, c: (b, 0, 0)),
                  pl.BlockSpec((1, 1, WIDTH, SSD_STATE), lambda b, c: (layer, b, 0, 0))]
                 + [_const_spec(x.shape) for x in consts],
        out_specs=(pl.BlockSpec((1, nv, WIDTH), lambda b, c: (b, c, 0)),
                   pl.BlockSpec((1, SSD_CONV - 1, CONV_DIM), lambda b, c: (b, 0, 0)),
                   pl.BlockSpec((1, WIDTH, SSD_STATE), lambda b, c: (b, 0, 0))),
        out_shape=(jax.ShapeDtypeStruct((bsz, seq, WIDTH), BF16 if nv % 16 == 0 else F32),
                   jax.ShapeDtypeStruct((bsz, SSD_CONV - 1, CONV_DIM), F32),
                   jax.ShapeDtypeStruct((bsz, WIDTH, SSD_STATE), F32)),
        scratch_shapes=[pltpu.VMEM((8 + q, CONV_DIM), F32), pltpu.VMEM((WIDTH, SSD_STATE), F32),
                        pltpu.VMEM((q, WIDTH), F32)],
        compiler_params=_params("parallel", "arbitrary"),
        name="ssd",
    )(z3, xbc3, sm3, smt3, conv0, h0, *consts)


def _merge_kernel(x_ref, ys5_ref, yfox_ref, yssd_ref, wg_ref, bg_ref, ws5_ref, wfox_ref, wssd_ref,
                  wo_ref, g_ref, b_ref, o_ref):
    x = x_ref[...]
    xb = x.astype(BF16)
    merged = None
    for br, (y_ref, w_ref) in enumerate(((ys5_ref, ws5_ref), (yfox_ref, wfox_ref), (yssd_ref, wssd_ref))):
        lo = br * D_MODEL
        gate = _sigmoid(jnp.dot(xb, wg_ref[:, lo:lo + D_MODEL], preferred_element_type=F32)
                        + bg_ref[:, lo:lo + D_MODEL])
        term = gate * jnp.dot(y_ref[...].astype(BF16), w_ref[...], preferred_element_type=F32)
        merged = term if merged is None else merged + term
    out = jnp.dot(merged.astype(BF16), wo_ref[...], preferred_element_type=F32)
    o_ref[...] = _layer_norm(ALPHA * x + out, g_ref[...], b_ref[...])


def _merge(x2, ys5, yfox, yssd, wg, bg, ws5, wfox, wssd, wo, g, b, tm):
    t = x2.shape[0]
    row = lambda w: pl.BlockSpec((tm, w), lambda i: (i, 0))
    consts = [wg, bg, ws5, wfox, wssd, wo, g, b]
    return pl.pallas_call(
        _merge_kernel,
        grid=(t // tm,),
        in_specs=[row(D_MODEL), row(WIDTH), row(WIDTH), row(WIDTH)] + [_const_spec(c.shape) for c in consts],
        out_specs=row(D_MODEL),
        out_shape=jax.ShapeDtypeStruct((t, D_MODEL), F32),
        compiler_params=_params("parallel"),
        name="merge",
    )(x2, ys5, yfox, yssd, *consts)


FF_CHUNK = 256


def _ffn_kernel(x_ref, wg_ref, wu_ref, wd_ref, g_ref, b_ref, o_ref):
    x = x_ref[...]
    xb = x.astype(BF16)
    acc = None
    for c in range(0, D_FF, FF_CHUNK):
        gt = jnp.dot(xb, wg_ref[:, c:c + FF_CHUNK], preferred_element_type=F32)
        up = jnp.dot(xb, wu_ref[:, c:c + FF_CHUNK], preferred_element_type=F32)
        h = (_silu(gt) * up).astype(BF16)
        d = jnp.dot(h, wd_ref[c:c + FF_CHUNK, :], preferred_element_type=F32)
        acc = d if acc is None else acc + d
    o_ref[...] = _layer_norm(ALPHA * x + acc, g_ref[...], b_ref[...])


def _ffn(x2, wg, wu, wd, g, b, tm):
    t = x2.shape[0]
    row = pl.BlockSpec((tm, D_MODEL), lambda i: (i, 0))
    consts = [wg, wu, wd, g, b]
    return pl.pallas_call(
        _ffn_kernel,
        grid=(t // tm,),
        in_specs=[row] + [_const_spec(c.shape) for c in consts],
        out_specs=row,
        out_shape=jax.ShapeDtypeStruct((t, D_MODEL), F32),
        compiler_params=_params("parallel"),
        name="ffn",
    )(x2, *consts)


def _moe_kernel(x_ref, wr_ref, br_ref, wg_ref, wu_ref, wd_ref, g_ref, b_ref, o_ref, gate_sc, acc_sc):
    e = pl.program_id(1)
    x = x_ref[...]
    xb = x.astype(BF16)
    lane = lax.broadcasted_iota(jnp.int32, (x.shape[0], LANES), 1)

    @pl.when(e == 0)
    def _():
        logits = jnp.dot(xb, wr_ref[...], preferred_element_type=F32) + br_ref[...]
        logits = jnp.where(lane < N_EXPERTS, logits, NEG)
        m1 = jnp.max(logits, axis=-1, keepdims=True)
        i1 = jnp.min(jnp.where(logits == m1, lane, LANES), axis=-1, keepdims=True)
        rest = jnp.where(lane == i1, NEG, logits)
        m2 = jnp.max(rest, axis=-1, keepdims=True)
        i2 = jnp.min(jnp.where(rest == m2, lane, LANES), axis=-1, keepdims=True)
        e2 = jnp.exp(m2 - m1)
        w1 = 1.0 / (1.0 + e2)
        w2 = e2 / (1.0 + e2)
        gate_sc[...] = jnp.where(lane == i1, w1, jnp.where(lane == i2, w2, 0.0))
        acc_sc[...] = jnp.zeros(acc_sc.shape, F32)

    gate_e = jnp.sum(jnp.where(lane == e, gate_sc[...], 0.0), axis=-1, keepdims=True)
    gt = jnp.dot(xb, wg_ref[0], preferred_element_type=F32)
    up = jnp.dot(xb, wu_ref[0], preferred_element_type=F32)
    h = (_silu(gt) * up).astype(BF16)
    acc_sc[...] += gate_e * jnp.dot(h, wd_ref[0], preferred_element_type=F32)

    @pl.when(e == N_EXPERTS - 1)
    def _():
        o_ref[...] = _layer_norm(ALPHA * x + acc_sc[...], g_ref[...], b_ref[...])


def _moe(x2, wr, br, wg, wu, wd, g, b, tm):
    t = x2.shape[0]
    row = pl.BlockSpec((tm, D_MODEL), lambda i, e: (i, 0))
    wspec = pl.BlockSpec((1, D_MODEL, D_MODEL), lambda i, e: (e, 0, 0))
    return pl.pallas_call(
        _moe_kernel,
        grid=(t // tm, N_EXPERTS),
        in_specs=[row, _const_spec(wr.shape), _const_spec(br.shape), wspec, wspec, wspec,
                  _const_spec(g.shape), _const_spec(b.shape)],
        out_specs=row,
        out_shape=jax.ShapeDtypeStruct((t, D_MODEL), F32),
        scratch_shapes=[pltpu.VMEM((tm, LANES), F32), pltpu.VMEM((tm, D_MODEL), F32)],
        compiler_params=_params("parallel", "arbitrary"),
        name="moe",
    )(x2, wr, br, wg, wu, wd, g, b)


def _row(v, width=None):
    v = v.reshape(1, -1).astype(F32)
    if width is not None and v.shape[1] < width:
        v = jnp.pad(v, ((0, 0), (0, width - v.shape[1])))
    return v


def _layer_weights(p, l):
    w_in = p['w_in'][l]
    o = np.cumsum([N_BRANCH * D_MODEL, WIDTH, WIDTH, WIDTH, WIDTH, HEADS, WIDTH, CONV_DIM, HEADS])
    gates, u, q, k, v, fg, z, xbc, dtw = (w_in[:, a:b] for a, b in zip([0] + list(o[:-1]), o))
    w = {}
    w['w_main'] = jnp.concatenate([u, q, k, v, z, xbc], axis=1).astype(BF16)
    slab = ((0, 0), (0, 0), (0, LANES - HEAD_DIM))
    k_slab = jnp.pad(k.reshape(D_MODEL, HEADS, HEAD_DIM), slab).reshape(D_MODEL, HEADS * LANES)
    q_slab = jnp.pad(q.reshape(D_MODEL, HEADS, HEAD_DIM), slab).reshape(D_MODEL, HEADS * LANES)
    w['w_rows'] = jnp.concatenate([u, k_slab, z, xbc], axis=1).astype(BF16)
    w['w_cols'] = jnp.concatenate([q_slab, k, v], axis=1).T.astype(BF16)
    e3 = np.zeros((BIAS_LANES, LANES, HEADS * LANES), np.float32)
    for piece in range(BIAS_LANES):
        for h in range(HEADS):
            e3[piece, h, LANES * h + HEAD_DIM + piece] = 1.0
    w['bias_place'] = jnp.asarray(e3, BF16)
    w['w_small'] = jnp.pad(jnp.concatenate([fg, dtw], axis=1), ((0, 0), (0, LANES - 2 * HEADS))).astype(BF16)
    w['b_small'] = _row(jnp.concatenate([p['b_fgate'][l], p['ssd_dt_bias'][l]]), LANES)
    w['w_gates'] = gates.astype(BF16)
    w['b_gates'] = _row(p['b_gate'][l])
    for name in ('w_branch_s5', 'w_branch_fox', 'w_branch_ssd', 'w_o'):
        w[name] = p[name][l].astype(BF16)
    for name in ('ln1_g', 'ln1_b', 'ln2_g', 'ln2_b', 's5_d', 's5_b_glu', 'ssd_conv_b', 'ssd_norm_w'):
        w[name] = _row(p[name][l])
    w['s5_w_glu'] = p['s5_w_glu'][l].astype(BF16)
    w['ssd_conv_w'] = p['ssd_conv_w'][l].astype(F32)
    w['ssd_alog_row'] = _row(jnp.concatenate([jnp.zeros((HEADS,), F32), p['ssd_a_log'][l]]), LANES)
    w['ssd_alog_col'] = jnp.concatenate([jnp.zeros((HEADS,), F32), p['ssd_a_log'][l]]).reshape(2 * HEADS, 1)
    w['ssd_d'] = _row(jnp.repeat(p['ssd_d'][l], HEAD_DIM))
    e = np.zeros((LANES, WIDTH), np.float32)
    for h in range(HEADS):
        e[HEADS + h, HEAD_DIM * h:HEAD_DIM * (h + 1)] = 1.0
    w['ssd_expand'] = jnp.asarray(e, BF16)
    if l % 2 == 0:
        for name in ('ffn_w_gate', 'ffn_w_up', 'ffn_w_down'):
            w[name] = p[name][l // 2].astype(BF16)
    else:
        w['moe_w_router'] = jnp.pad(p['moe_w_router'][l // 2], ((0, 0), (0, LANES - N_EXPERTS))).astype(BF16)
        w['moe_b_router'] = _row(p['moe_b_router'][l // 2], LANES)
        for name in ('moe_w_gate', 'moe_w_up', 'moe_w_down'):
            w[name] = p[name][l // 2].astype(BF16)
    return w


def _s5_q(seq):
    return min(S5_CHUNK, seq)


def _run_trunk(x, p, weights, s5_tabs, s5_re0, s5_im0, conv0, ssd0, paged):
    bsz, seq, _ = x.shape
    t = bsz * seq
    tm_mix = min(512, t)
    fox_tile = min(512, seq)
    x2 = x.reshape(t, D_MODEL)
    new = [[] for _ in range(7)]
    q5 = _s5_q(seq)
    ssd0 = ssd0.reshape(ssd0.shape[0], bsz, WIDTH, SSD_STATE)
    if paged is not None:
        cache_k, cache_v, cache_lf, page_table = paged
        cache_kt = cache_k.transpose(0, 1, 3, 4, 2)
        cache_vt = cache_v.transpose(0, 1, 3, 4, 2)
        cache_lft = cache_lf.transpose(0, 1, 3, 2)
    for l in range(len(weights)):
        w = weights[l]
        if paged is None:
            u, ub, z, xbc, small, smt3, kaug, qaugt, kt, vt, tot = _in_proj_prompt(
                x2.reshape(bsz, seq, D_MODEL), w['w_rows'], w['w_cols'], w['w_small'], w['b_small'],
                w['bias_place'], fox_tile)
            k_out = kt.reshape(bsz, HEADS, HEAD_DIM, seq).transpose(0, 3, 1, 2)
            v_out = vt.reshape(bsz, HEADS, HEAD_DIM, seq).transpose(0, 3, 1, 2)
            logf_out = smt3[:, :HEADS, :].transpose(0, 2, 1)
        else:
            u, ub, qb, k, v, z, xbc, small, small_t, logf = _in_proj_sample(
                x2, w['w_main'], w['w_small'], w['b_small'], min(256, t))
            k_out = k.reshape(bsz, seq, HEADS, HEAD_DIM)
            v_out = v.reshape(bsz, seq, HEADS, HEAD_DIM)
            logf_out = logf.reshape(bsz, seq, HEADS)
            smt3 = small_t.reshape(2 * HEADS, bsz, seq).transpose(1, 0, 2)

        tab = s5_tabs[q5][l]
        yi, cs_re, cs_im = _s5_chunks(ub.reshape(t // q5, q5 * WIDTH), tab, q5, min(128, t // q5))
        d_row, w_glu, b_glu = w['s5_d'], w['s5_w_glu'], w['s5_b_glu']
        if paged is None:
            y_s5, s5_re, s5_im = _s5_scan(u.reshape(bsz, seq, WIDTH), yi, cs_re, cs_im,
                                          s5_re0[l].reshape(bsz, 1, S5_LANES), s5_im0[l].reshape(bsz, 1, S5_LANES),
                                          tab, d_row, w_glu, b_glu, q5, min(256, seq), True)
        else:
            y_s5, s5_re, s5_im = _s5_scan(u.reshape(1, t, WIDTH), yi, cs_re, cs_im,
                                          s5_re0[l].reshape(bsz, S5_LANES), s5_im0[l].reshape(bsz, S5_LANES),
                                          tab, d_row, w_glu, b_glu, q5, min(256, t), False)
        y_s5 = y_s5.reshape(t, WIDTH)
        s5_re = s5_re.reshape(bsz, S5_GROUPS, S5_STATE)
        s5_im = s5_im.reshape(bsz, S5_GROUPS, S5_STATE)

        if paged is None:
            y_fox = _fox_prompt(qaugt, kaug, vt, tot, fox_tile)
        else:
            lfn_t = jnp.pad(logf.reshape(bsz, seq, HEADS).transpose(0, 2, 1),
                            ((0, 0), (0, 0), (0, LANES - seq)))
            y_fox = _fox_sample(qb.astype(F32).reshape(bsz, seq, WIDTH), k.reshape(bsz, seq, WIDTH),
                                v.reshape(bsz, seq, WIDTH), lfn_t, cache_kt, cache_vt, cache_lft, l, page_table)
        y_fox = y_fox.reshape(t, WIDTH)

        nv = math.gcd(seq, SSD_CHUNK)
        sm3 = small.reshape(bsz, seq, LANES)
        if nv < SSD_CHUNK:
            sm3 = jnp.pad(sm3, ((0, 0), (0, SSD_CHUNK - nv), (0, 0)))
            smt3 = jnp.pad(smt3, ((0, 0), (0, 0), (0, SSD_CHUNK - nv)))
        y_ssd, conv_new, ssd_new = _ssd(
            z.reshape(bsz, seq, WIDTH), xbc.reshape(bsz, seq, CONV_DIM), sm3, smt3, conv0[l],
            ssd0, l, w['ssd_conv_w'], w['ssd_conv_b'], w['ssd_alog_row'],
            w['ssd_alog_col'], w['ssd_expand'], w['ssd_d'], w['ssd_norm_w'], nv)
        y_ssd = y_ssd.reshape(t, WIDTH)
        ssd_new = ssd_new.reshape(bsz, HEADS, HEAD_DIM, SSD_STATE)

        x2 = _merge(x2, y_s5, y_fox, y_ssd, w['w_gates'], w['b_gates'], w['w_branch_s5'],
                    w['w_branch_fox'], w['w_branch_ssd'], w['w_o'], w['ln1_g'], w['ln1_b'], tm_mix)
        if l % 2 == 0:
            x2 = _ffn(x2, w['ffn_w_gate'], w['ffn_w_up'], w['ffn_w_down'], w['ln2_g'], w['ln2_b'], tm_mix)
        else:
            x2 = _moe(x2, w['moe_w_router'], w['moe_b_router'], w['moe_w_gate'], w['moe_w_up'],
                      w['moe_w_down'], w['ln2_g'], w['ln2_b'], tm_mix)

        st = (k_out, v_out, logf_out, s5_re, s5_im, conv_new, ssd_new)
        for lst, s in zip(new, st):
            lst.append(s)
    return x2.reshape(bsz, seq, D_MODEL), [jnp.stack(s) for s in new]


def kernel(x_prompt, x_sample, cache_k, cache_v, cache_logf, page_table, state_s5_re, state_s5_im,
           state_conv, state_ssd, w_in, b_gate, b_fgate, s5_lam_re, s5_lam_im, s5_log_dt, s5_b_re,
           s5_b_im, s5_c_re, s5_c_im, s5_d, s5_w_glu, s5_b_glu, ssd_conv_w, ssd_conv_b, ssd_dt_bias,
           ssd_a_log, ssd_d, ssd_norm_w, w_branch_s5, w_branch_fox, w_branch_ssd, w_o, ln1_g, ln1_b,
           ln2_g, ln2_b, ffn_w_gate, ffn_w_up, ffn_w_down, moe_w_router, moe_b_router, moe_w_gate,
           moe_w_up, moe_w_down):
    p = dict(w_in=w_in, b_gate=b_gate, b_fgate=b_fgate, s5_d=s5_d, s5_w_glu=s5_w_glu, s5_b_glu=s5_b_glu,
             ssd_conv_w=ssd_conv_w, ssd_conv_b=ssd_conv_b, ssd_dt_bias=ssd_dt_bias, ssd_a_log=ssd_a_log,
             ssd_d=ssd_d, ssd_norm_w=ssd_norm_w, w_branch_s5=w_branch_s5, w_branch_fox=w_branch_fox,
             w_branch_ssd=w_branch_ssd, w_o=w_o, ln1_g=ln1_g, ln1_b=ln1_b, ln2_g=ln2_g, ln2_b=ln2_b,
             ffn_w_gate=ffn_w_gate, ffn_w_up=ffn_w_up, ffn_w_down=ffn_w_down, moe_w_router=moe_w_router,
             moe_b_router=moe_b_router, moe_w_gate=moe_w_gate, moe_w_up=moe_w_up, moe_w_down=moe_w_down)
    depth = w_in.shape[0]
    weights = [_layer_weights(p, l) for l in range(depth)]
    qs = {_s5_q(x_prompt.shape[1]), _s5_q(x_sample.shape[1])}
    s5_tabs = _all_s5_tables(s5_lam_re, s5_lam_im, s5_log_dt, s5_b_re, s5_b_im, s5_c_re, s5_c_im, qs)
    bsz = x_prompt.shape[0]
    z_s5 = jnp.zeros((depth, bsz, S5_GROUPS, S5_STATE), F32)
    z_conv = jnp.zeros((depth, bsz, SSD_CONV - 1, CONV_DIM), F32)
    z_ssd = jnp.zeros((depth, bsz, HEADS, HEAD_DIM, SSD_STATE), F32)
    y_p, st_p = _run_trunk(x_prompt, p, weights, s5_tabs, z_s5, z_s5, z_conv, z_ssd, None)
    y_s, st_s = _run_trunk(x_sample, p, weights, s5_tabs, state_s5_re, state_s5_im, state_conv, state_ssd,
                           (cache_k, cache_v, cache_logf, page_table))
    return (y_p, y_s, *st_p, *st_s)
```

```python
import functools
import math

import jax
import jax.numpy as jnp
import numpy as np
from jax import lax
from jax.experimental import pallas as pl
from jax.experimental.pallas import tpu as pltpu

F32 = jnp.float32
BF16 = jnp.bfloat16
HIGHEST = lax.Precision.HIGHEST

D_MODEL = 1024
N_BRANCH = 3
WIDTH = 512
S5_GROUPS = 32
S5_GROUP = 16
S5_STATE = 64
S5_LANES = S5_GROUPS * S5_STATE
S5_CHUNK = 16
HEADS = 8
HEAD_DIM = 64
SSD_STATE = 128
SSD_GROUPS = 2
SSD_CONV = 4
SSD_CHUNK = 128
SSD_MIN_ROWS = 16
CONV_DIM = 1024
D_FF = 2816
N_EXPERTS = 8
PAGE = 128
ALPHA = 4.0 ** 0.25
LN_EPS = 1e-5
RMS_EPS = 1e-5
NEG = -1e30
LANES = 128
VMEM_LIMIT = 56 * 1024 * 1024

NT_DIMS = (((1,), (1,)), ((), ()))


def _params(*sem):
    return pltpu.CompilerParams(dimension_semantics=sem, vmem_limit_bytes=VMEM_LIMIT)


def _const_spec(shape):
    nd = len(shape)
    return pl.BlockSpec(shape, lambda *_: (0,) * nd, pipeline_mode=pl.Buffered(1))


def _layer_norm(x, g, b):
    mu = jnp.mean(x, axis=-1, keepdims=True)
    xc = x - mu
    var = jnp.mean(xc * xc, axis=-1, keepdims=True)
    return xc * lax.rsqrt(var + LN_EPS) * g + b


def _sigmoid(x):
    return 1.0 / (1.0 + jnp.exp(-x))


def _silu(x):
    return x * _sigmoid(x)


def _bf16_pieces(x):
    hi = x.astype(BF16)
    r1 = x - hi.astype(F32)
    mid = r1.astype(BF16)
    lo = (r1 - mid.astype(F32)).astype(BF16)
    return hi, mid, lo


def _small_act(xb, ws_ref, bs_ref):
    s = jnp.dot(xb, ws_ref[...], preferred_element_type=F32) + bs_ref[...]
    t = jnp.log1p(jnp.exp(-jnp.abs(s)))
    lane = lax.broadcasted_iota(jnp.int32, s.shape, 1)
    return jnp.where(lane < HEADS, jnp.minimum(s, 0.0) - t,
                     jnp.where(lane < 2 * HEADS, jnp.maximum(s, 0.0) + t, 0.0))


SEG_U, SEG_Q, SEG_K, SEG_V, SEG_Z, SEG_X, SEG_END = 0, 512, 1024, 1536, 2048, 2560, 3584


def _in_proj_sample_kernel(x_ref, wm_ref, ws_ref, bs_ref, u_ref, ub_ref, qb_ref, k_ref, v_ref,
                           z_ref, xbc_ref, small_ref, smallt_ref, logf_ref):
    xb = x_ref[...].astype(BF16)

    def seg(a, b):
        return jnp.dot(xb, wm_ref[:, a:b], preferred_element_type=F32)

    u = seg(SEG_U, SEG_Q)
    u_ref[...] = u
    ub_ref[...] = u.astype(BF16)
    qb_ref[...] = (seg(SEG_Q, SEG_K) * (HEAD_DIM ** -0.5)).astype(BF16)
    k_ref[...] = seg(SEG_K, SEG_V)
    v_ref[...] = seg(SEG_V, SEG_Z)
    z_ref[...] = seg(SEG_Z, SEG_X)
    xbc_ref[...] = seg(SEG_X, SEG_END)
    sm = _small_act(xb, ws_ref, bs_ref)
    small_ref[...] = sm
    smallt_ref[...] = sm.T[:2 * HEADS, :]
    logf_ref[...] = sm[:, :HEADS]


def _in_proj_sample(x2, wm, ws, bs, tm):
    t = x2.shape[0]
    row = lambda w: pl.BlockSpec((tm, w), lambda i: (i, 0))
    out_shape = (
        jax.ShapeDtypeStruct((t, WIDTH), F32),
        jax.ShapeDtypeStruct((t, WIDTH), BF16),
        jax.ShapeDtypeStruct((t, WIDTH), BF16),
        jax.ShapeDtypeStruct((t, WIDTH), F32),
        jax.ShapeDtypeStruct((t, WIDTH), F32),
        jax.ShapeDtypeStruct((t, WIDTH), F32),
        jax.ShapeDtypeStruct((t, CONV_DIM), F32),
        jax.ShapeDtypeStruct((t, LANES), F32),
        jax.ShapeDtypeStruct((2 * HEADS, t), F32),
        jax.ShapeDtypeStruct((t, HEADS), F32),
    )
    out_specs = (row(WIDTH), row(WIDTH), row(WIDTH), row(WIDTH), row(WIDTH),
                 row(WIDTH), row(CONV_DIM), row(LANES),
                 pl.BlockSpec((2 * HEADS, tm), lambda i: (0, i)), row(HEADS))
    return pl.pallas_call(
        _in_proj_sample_kernel,
        grid=(t // tm,),
        in_specs=[row(D_MODEL), _const_spec(wm.shape), _const_spec(ws.shape), _const_spec(bs.shape)],
        out_specs=out_specs,
        out_shape=out_shape,
        compiler_params=_params("parallel"),
        name="in_proj_sample",
    )(x2, wm, ws, bs)


PSEG_U, PSEG_K, PSEG_Z, PSEG_X, PSEG_END = 0, 512, 1536, 2048, 3072
TSEG_Q, TSEG_K, TSEG_V, TSEG_END = 0, 1024, 1536, 2048
BIAS_LANES = 3


def _in_proj_prompt_kernel(x_ref, wr_ref, wt_ref, ws_ref, bs_ref, e_ref, u_ref, ub_ref, z_ref, xbc_ref,
                           small_ref, smallt_ref, kaug_ref, qaugt_ref, kt_ref, vt_ref, tot_ref, u_sc):
    tm = x_ref.shape[0]
    xb = x_ref[...].astype(BF16)
    u = jnp.dot(xb, wr_ref[:, PSEG_U:PSEG_K], preferred_element_type=F32)
    u_ref[...] = u
    for k in range(WIDTH // LANES):
        u_sc[k] = u[:, LANES * k:LANES * (k + 1)]
    for s in range(S5_CHUNK):
        for k in range(WIDTH // LANES):
            lo_lane = s * WIDTH + LANES * k
            ub_ref[:, lo_lane:lo_lane + LANES] = u_sc[k, pl.ds(s, tm // S5_CHUNK, stride=S5_CHUNK), :].astype(BF16)
    z_ref[...] = jnp.dot(xb, wr_ref[:, PSEG_Z:PSEG_X], preferred_element_type=F32)
    xbc_ref[...] = jnp.dot(xb, wr_ref[:, PSEG_X:PSEG_END], preferred_element_type=F32)
    sm = _small_act(xb, ws_ref, bs_ref)
    small_ref[...] = sm
    smt = sm.T[:2 * HEADS, :]
    smallt_ref[0] = smt
    r_i = lax.broadcasted_iota(jnp.int32, (tm, tm), 0)
    c_i = lax.broadcasted_iota(jnp.int32, (tm, tm), 1)
    tri = (r_i >= c_i).astype(BF16)
    ones = jnp.ones((tm, tm), BF16)
    cloc = None
    tot = None
    for piece, piece_t in zip(_bf16_pieces(sm), _bf16_pieces(smt)):
        d = jnp.dot(tri, piece, preferred_element_type=F32)
        dt = jnp.dot(piece_t, ones, preferred_element_type=F32)
        cloc = d if cloc is None else cloc + d
        tot = dt if tot is None else tot + dt
    tot_ref[0, 0] = tot
    pieces = jnp.concatenate(_bf16_pieces(-cloc), axis=1)
    k_slabs = (jnp.dot(xb, wr_ref[:, PSEG_K:PSEG_Z], preferred_element_type=F32)
               + jnp.dot(pieces, e_ref[...], preferred_element_type=F32)).astype(BF16)
    for h in range(HEADS):
        kaug_ref[0, h] = k_slabs[:, LANES * h:LANES * (h + 1)]
    tall = lax.dot_general(wt_ref[...], xb, NT_DIMS, preferred_element_type=F32)
    row = lax.broadcasted_iota(jnp.int32, (LANES, tm), 0)
    ones_rows = (row >= HEAD_DIM) & (row < HEAD_DIM + BIAS_LANES)
    for h in range(HEADS):
        qh = tall[TSEG_Q + LANES * h:TSEG_Q + LANES * (h + 1), :]
        qaugt_ref[0, h] = jnp.where(ones_rows, 1.0, qh * (HEAD_DIM ** -0.5)).astype(BF16)
    kt_ref[0] = tall[TSEG_K:TSEG_V, :]
    vt_ref[0] = tall[TSEG_V:TSEG_END, :]


def _in_proj_prompt(x3, wr, wt, ws, bs, e3, tm):
    bsz, seq, _ = x3.shape
    nt = seq // tm
    x2 = x3.reshape(bsz * seq, D_MODEL)
    row = lambda w: pl.BlockSpec((tm, w), lambda b, i: (b * nt + i, 0))
    colt = lambda r: pl.BlockSpec((1, r, tm), lambda b, i: (b, 0, i))
    t = bsz * seq
    out_shape = (
        jax.ShapeDtypeStruct((t, WIDTH), F32),
        jax.ShapeDtypeStruct((t // S5_CHUNK, S5_CHUNK * WIDTH), BF16),
        jax.ShapeDtypeStruct((t, WIDTH), F32),
        jax.ShapeDtypeStruct((t, CONV_DIM), F32),
        jax.ShapeDtypeStruct((t, LANES), F32),
        jax.ShapeDtypeStruct((bsz, 2 * HEADS, seq), F32),
        jax.ShapeDtypeStruct((bsz, HEADS, seq, LANES), BF16),
        jax.ShapeDtypeStruct((bsz, HEADS, LANES, seq), BF16),
        jax.ShapeDtypeStruct((bsz, WIDTH, seq), F32),
        jax.ShapeDtypeStruct((bsz, WIDTH, seq), F32),
        jax.ShapeDtypeStruct((bsz, nt, 2 * HEADS, tm), F32),
    )
    out_specs = (row(WIDTH),
                 pl.BlockSpec((tm // S5_CHUNK, S5_CHUNK * WIDTH), lambda b, i: (b * nt + i, 0)),
                 row(WIDTH), row(CONV_DIM), row(LANES), colt(2 * HEADS),
                 pl.BlockSpec((1, HEADS, tm, LANES), lambda b, i: (b, 0, i, 0)),
                 pl.BlockSpec((1, HEADS, LANES, tm), lambda b, i: (b, 0, 0, i)),
                 colt(WIDTH), colt(WIDTH),
                 pl.BlockSpec((1, 1, 2 * HEADS, tm), lambda b, i: (b, i, 0, 0)))
    consts = [wr, wt, ws, bs, e3]
    return pl.pallas_call(
        _in_proj_prompt_kernel,
        grid=(bsz, nt),
        in_specs=[row(D_MODEL)] + [_const_spec(c.shape) for c in consts],
        out_specs=out_specs,
        out_shape=out_shape,
        scratch_shapes=[pltpu.VMEM((WIDTH // LANES, tm, LANES), F32)],
        compiler_params=_params("parallel", "parallel"),
        name="in_proj_prompt",
    )(x2, *consts)


def _cumsum_lanes(x):
    n = x.shape[-1]
    lane = lax.broadcasted_iota(jnp.int32, x.shape, x.ndim - 1)
    s = 1
    while s < n:
        x = x + jnp.where(lane >= s, pltpu.roll(x, s, axis=x.ndim - 1), 0.0)
        s *= 2
    return x


FOX_HEADS_PER_STEP = 8


def _fox_prompt_kernel(it_ref, jt_ref, q_ref, k_ref, v_ref, tot_ref, o_ref, m_sc, l_sc, c_sc, acc_sc, *, tq, tk):
    g = pl.program_id(1)
    t = pl.program_id(2)
    i = it_ref[t]
    j = jt_ref[t]
    hps = FOX_HEADS_PER_STEP

    @pl.when(j == 0)
    def _():
        m_sc[...] = jnp.full(m_sc.shape, NEG, F32)
        l_sc[...] = jnp.zeros(l_sc.shape, F32)
        c_sc[...] = jnp.zeros(c_sc.shape, F32)
        acc_sc[...] = jnp.zeros(acc_sc.shape, F32)

    def step(masked):
        if masked:
            kpos = j * tk + lax.broadcasted_iota(jnp.int32, (tk, tq), 0)
            qpos = i * tq + lax.broadcasted_iota(jnp.int32, (tk, tq), 1)
            causal = kpos <= qpos
        for h in range(hps):
            s = jnp.dot(k_ref[0, h], q_ref[0, h], preferred_element_type=F32)
            if masked:
                s = jnp.where(causal, s, NEG)
            c_j = c_sc[h]
            m_prev = m_sc[h]
            m_new = jnp.maximum(m_prev, jnp.max(s, axis=0, keepdims=True) - c_j)
            alpha = jnp.exp(m_prev - m_new)
            p = jnp.exp(s - (m_new + c_j))
            l_sc[h] = alpha * l_sc[h] + jnp.sum(p, axis=0, keepdims=True)
            rows = slice(HEAD_DIM * h, HEAD_DIM * (h + 1))
            acc_sc[rows, :] = alpha * acc_sc[rows, :] + jnp.dot(
                v_ref[0, rows, :].astype(BF16), p.astype(BF16), preferred_element_type=F32)
            m_sc[h] = m_new
            c_sc[h] = c_j + tot_ref[0, 0, pl.ds(g * hps + h, 1), :]

    @pl.when(j < i)
    def _():
        step(False)

    @pl.when(j == i)
    def _():
        step(True)
        inv = jnp.concatenate([jnp.broadcast_to(1.0 / l_sc[h], (HEAD_DIM, tq)) for h in range(hps)], axis=0)
        o_ref[0] = (acc_sc[...] * inv).T.astype(o_ref.dtype)


def _fox_prompt(qaugt, kaug, vt, tot, tq):
    bsz, _, _, seq = qaugt.shape
    nq = seq // tq
    hps = FOX_HEADS_PER_STEP
    it = np.array([i for i in range(nq) for j in range(i + 1)], np.int32)
    jt = np.array([j for i in range(nq) for j in range(i + 1)], np.int32)
    grid_spec = pltpu.PrefetchScalarGridSpec(
        num_scalar_prefetch=2,
        grid=(bsz, HEADS // hps, len(it)),
        in_specs=[
            pl.BlockSpec((1, hps, LANES, tq), lambda b, g, t, it, jt: (b, g, 0, it[t])),
            pl.BlockSpec((1, hps, tq, LANES), lambda b, g, t, it, jt: (b, g, jt[t], 0)),
            pl.BlockSpec((1, hps * HEAD_DIM, tq), lambda b, g, t, it, jt: (b, g, jt[t])),
            pl.BlockSpec((1, 1, 2 * HEADS, tq), lambda b, g, t, it, jt: (b, jt[t], 0, 0)),
        ],
        out_specs=pl.BlockSpec((1, tq, hps * HEAD_DIM), lambda b, g, t, it, jt: (b, it[t], g)),
        scratch_shapes=[pltpu.VMEM((hps, 1, tq), F32), pltpu.VMEM((hps, 1, tq), F32),
                        pltpu.VMEM((hps, 1, tq), F32), pltpu.VMEM((hps * HEAD_DIM, tq), F32)],
    )
    return pl.pallas_call(
        functools.partial(_fox_prompt_kernel, tq=tq, tk=tq),
        grid_spec=grid_spec,
        out_shape=jax.ShapeDtypeStruct((bsz, seq, WIDTH), BF16),
        compiler_params=_params("parallel", "parallel", "arbitrary"),
        name="fox_prompt",
    )(jnp.asarray(it), jnp.asarray(jt), qaugt, kaug, vt, tot)


def _fox_sample_kernel(pt_ref, q_ref, kn_ref, vn_ref, lfn_ref, *refs, n_pages, tq):
    k_refs = refs[:n_pages]
    v_refs = refs[n_pages:2 * n_pages]
    lf_ref = refs[2 * n_pages]
    o_ref = refs[2 * n_pages + 1]
    b = pl.program_id(0)
    rows = tq * HEADS
    r_i = lax.broadcasted_iota(jnp.int32, (rows, tq), 0)
    t_i = lax.broadcasted_iota(jnp.int32, (rows, tq), 1)
    rep = (r_i // HEADS == t_i).astype(F32)
    q = q_ref[0].astype(F32)
    qe = jnp.dot(rep, q, precision=HIGHEST, preferred_element_type=F32)
    r_w = lax.broadcasted_iota(jnp.int32, (rows, WIDTH), 0)
    l_w = lax.broadcasted_iota(jnp.int32, (rows, WIDTH), 1)
    head_mask = (l_w // HEAD_DIM) == (r_w % HEADS)
    qbd = jnp.where(head_mask, qe, 0.0).astype(BF16)

    scores = [None] * n_pages
    carry = jnp.zeros((HEADS, 1), F32)
    for p in range(n_pages - 1, -1, -1):
        cs = _cumsum_lanes(lf_ref[0, pt_ref[b * n_pages + p]])
        tot = cs[:, PAGE - 1:PAGE]
        dec = (carry + tot) - cs
        carry = carry + tot
        kt = k_refs[p][0, 0].reshape(WIDTH, PAGE).astype(BF16)
        s = jnp.dot(qbd, kt, preferred_element_type=F32)
        scores[p] = s + jnp.concatenate([dec] * tq, axis=0)
    pad = jnp.zeros((PAGE - tq, WIDTH), F32)
    kn = jnp.concatenate([kn_ref[0], pad], axis=0).astype(BF16)
    vn = jnp.concatenate([vn_ref[0], pad], axis=0).astype(BF16)
    m_i = lax.broadcasted_iota(jnp.int32, (PAGE, PAGE), 0)
    c_i = lax.broadcasted_iota(jnp.int32, (PAGE, PAGE), 1)
    cn = jnp.dot(lfn_ref[0], (m_i <= c_i).astype(F32), precision=HIGHEST,
                 preferred_element_type=F32)
    r_p = lax.broadcasted_iota(jnp.int32, (rows, PAGE), 0)
    c_p = lax.broadcasted_iota(jnp.int32, (rows, PAGE), 1)
    s_new = lax.dot_general(qbd, kn, NT_DIMS, preferred_element_type=F32)
    s_new = jnp.where(c_p <= r_p // HEADS, s_new - jnp.concatenate([cn] * tq, axis=0), NEG)

    m = s_new
    for p in range(n_pages):
        m = jnp.maximum(m, scores[p])
    m = jnp.max(m, axis=-1, keepdims=True)
    pn = jnp.exp(s_new - m)
    l = jnp.sum(pn, axis=-1, keepdims=True)
    o = jnp.dot(pn.astype(BF16), vn, preferred_element_type=F32)
    for p in range(n_pages):
        pp = jnp.exp(scores[p] - m)
        l = l + jnp.sum(pp, axis=-1, keepdims=True)
        vt = v_refs[p][0, 0].reshape(WIDTH, PAGE).astype(BF16)
        o = o + lax.dot_general(pp.astype(BF16), vt, NT_DIMS, preferred_element_type=F32)
    om = jnp.where(head_mask, o / l, 0.0)
    rep_t = (lax.broadcasted_iota(jnp.int32, (tq, rows), 1) // HEADS
             == lax.broadcasted_iota(jnp.int32, (tq, rows), 0)).astype(F32)
    o_ref[0] = jnp.dot(rep_t, om, precision=HIGHEST, preferred_element_type=F32).astype(o_ref.dtype)


def _fox_sample(qb, k_new, v_new, lfn_t, cache_kt, cache_vt, cache_lft, layer, page_table):
    dbsz, tq, _ = qb.shape
    n_pages = page_table.shape[1]
    pt = page_table.reshape(-1).astype(jnp.int32)

    def page_spec(p, shape):
        zeros = (0,) * (len(shape) - 2)
        return pl.BlockSpec(shape, lambda b, pt: (layer, pt[b * n_pages + p]) + zeros)

    seq_spec = lambda shape: pl.BlockSpec(shape, lambda b, pt: (b, 0, 0))
    in_specs = [seq_spec((1, tq, WIDTH)), seq_spec((1, tq, WIDTH)), seq_spec((1, tq, WIDTH)),
                seq_spec((1, HEADS, LANES))]
    in_specs += [page_spec(p, (1, 1, HEADS, HEAD_DIM, PAGE)) for p in range(n_pages)]
    in_specs += [page_spec(p, (1, 1, HEADS, HEAD_DIM, PAGE)) for p in range(n_pages)]
    in_specs += [pl.BlockSpec((1,) + cache_lft.shape[1:], lambda b, pt: (layer, 0, 0, 0),
                              pipeline_mode=pl.Buffered(1))]
    grid_spec = pltpu.PrefetchScalarGridSpec(
        num_scalar_prefetch=1, grid=(dbsz,), in_specs=in_specs,
        out_specs=pl.BlockSpec((1, tq, WIDTH), lambda b, pt: (b, 0, 0)))
    return pl.pallas_call(
        functools.partial(_fox_sample_kernel, n_pages=n_pages, tq=tq),
        grid_spec=grid_spec,
        out_shape=jax.ShapeDtypeStruct((dbsz, tq, WIDTH), F32),
        compiler_params=_params("parallel"),
        name="fox_sample",
    )(pt, qb, k_new, v_new, lfn_t, *([cache_kt] * n_pages), *([cache_vt] * n_pages), cache_lft)


def _s5_tables(lam_re, lam_im, log_dt, b_re, b_im, c_re, c_im, q):
    hp = dict(precision=HIGHEST)
    dt = jnp.exp(log_dt)[:, None]
    mag = jnp.exp(lam_re * dt)
    ab_re = mag * jnp.cos(lam_im * dt)
    ab_im = mag * jnp.sin(lam_im * dt)
    nr = ab_re - 1.0
    den = lam_re * lam_re + lam_im * lam_im
    q_re = (nr * lam_re + ab_im * lam_im) / den
    q_im = (ab_im * lam_re - nr * lam_im) / den
    bb_re = q_re[..., None] * b_re - q_im[..., None] * b_im
    bb_im = q_re[..., None] * b_im + q_im[..., None] * b_re
    jj = jnp.arange(q + 1, dtype=F32)[:, None, None]
    pmag = jnp.exp(lam_re * dt * jj)
    p_re = pmag * jnp.cos(lam_im * dt * jj)
    p_im = pmag * jnp.sin(lam_im * dt * jj)
    ab_b_re = p_re[:q, :, :, None] * bb_re - p_im[:q, :, :, None] * bb_im
    ab_b_im = p_re[:q, :, :, None] * bb_im + p_im[:q, :, :, None] * bb_re
    kern = (jnp.einsum('gcn,jgnd->jgcd', c_re, ab_b_re, **hp)
            - jnp.einsum('gcn,jgnd->jgcd', c_im, ab_b_im, **hp))
    eye16 = jnp.eye(16, dtype=F32)
    kt = kern.transpose(0, 1, 3, 2).reshape(q, 2, 16, S5_GROUP, S5_GROUP)
    w_toep = jnp.einsum('jxgab,gh->jxgahb', kt, eye16).reshape(q, 2, 256, 256).astype(BF16)
    eye_g = jnp.eye(S5_GROUPS, dtype=F32)

    def in_map(bb):
        full = jnp.einsum('gnc,gh->gchn', bb, eye_g).reshape(WIDTH, S5_LANES)
        return jnp.stack([full[128 * (n // 2):128 * (n // 2) + 128, 256 * n:256 * n + 256]
                          for n in range(8)]).astype(BF16)

    def out_map(c):
        full = jnp.einsum('gcn,gh->gnhc', c, eye_g).reshape(S5_LANES, WIDTH)
        return jnp.stack([full[512 * m:512 * m + 512, 128 * m:128 * m + 128]
                          for m in range(4)]).astype(BF16)

    rows = ((q + 1 + 7) // 8) * 8
    padp = lambda p: jnp.pad(p.reshape(q + 1, S5_LANES), ((0, rows - q - 1), (0, 0)))
    return dict(w_toep=w_toep, wb_re=in_map(bb_re), wb_im=in_map(bb_im), wc_re=out_map(c_re),
                wc_im=out_map(-c_im), p_re=padp(p_re), p_im=padp(p_im))


def _all_s5_tables(lam_re, lam_im, log_dt, b_re, b_im, c_re, c_im, qs):
    f32 = lambda a: a.astype(F32)
    q_max = max(qs)
    full = jax.vmap(functools.partial(_s5_tables, q=q_max))(
        f32(lam_re), f32(lam_im), f32(log_dt), f32(b_re), f32(b_im), f32(c_re), f32(c_im))
    out = {}
    for q in qs:
        rows = ((q + 1 + 7) // 8) * 8
        out[q] = []
        for l in range(lam_re.shape[0]):
            tab = {name: arr[l] for name, arr in full.items()}
            tab['w_toep'] = tab['w_toep'][:q]
            tab['p_re'] = tab['p_re'][:rows]
            tab['p_im'] = tab['p_im'][:rows]
            out[q].append(tab)
    return out


def _s5_chunk_kernel(u_ref, w_ref, wbr_ref, wbi_ref, pr_ref, pi_ref, y_ref, sr_ref, si_ref, *, q):
    for s_out in range(q):
        for half in range(2):
            acc = None
            for j in range(s_out + 1):
                a = (s_out - j) * WIDTH + half * 256
                d = jnp.dot(u_ref[:, a:a + 256], w_ref[j, half], preferred_element_type=F32)
                acc = d if acc is None else acc + d
            o = s_out * WIDTH + half * 256
            y_ref[:, o:o + 256] = acc
    for n in range(8):
        lanes = slice(256 * n, 256 * (n + 1))
        acc_r = None
        acc_i = None
        for s in range(q):
            a = s * WIDTH + 128 * (n // 2)
            br = jnp.dot(u_ref[:, a:a + 128], wbr_ref[n], preferred_element_type=F32)
            bi = jnp.dot(u_ref[:, a:a + 128], wbi_ref[n], preferred_element_type=F32)
            pr = pr_ref[q - 1 - s:q - s, lanes]
            pi = pi_ref[q - 1 - s:q - s, lanes]
            tr = pr * br - pi * bi
            ti = pr * bi + pi * br
            acc_r = tr if acc_r is None else acc_r + tr
            acc_i = ti if acc_i is None else acc_i + ti
        sr_ref[:, lanes] = acc_r
        si_ref[:, lanes] = acc_i


def _s5_chunks(u2b, tab, q, rt):
    nc = u2b.shape[0]
    consts = [tab['w_toep'], tab['wb_re'], tab['wb_im'], tab['p_re'], tab['p_im']]
    state = pl.BlockSpec((rt, S5_LANES), lambda i: (i, 0))
    return pl.pallas_call(
        functools.partial(_s5_chunk_kernel, q=q),
        grid=(nc // rt,),
        in_specs=[pl.BlockSpec((rt, q * WIDTH), lambda i: (i, 0))] + [_const_spec(c.shape) for c in consts],
        out_specs=(pl.BlockSpec((rt, q * WIDTH), lambda i: (i, 0)), state, state),
        out_shape=(jax.ShapeDtypeStruct((nc, q * WIDTH), F32),
                   jax.ShapeDtypeStruct((nc, S5_LANES), F32), jax.ShapeDtypeStruct((nc, S5_LANES), F32)),
        compiler_params=_params("parallel"),
        name="s5_chunks",
    )(u2b, *consts)


def _s5_scan_kernel(u_ref, yi_ref, sre_ref, sim_ref, h0r_ref, h0i_ref, wcr_ref, wci_ref, pr_ref, pi_ref,
                    d_ref, wg_ref, bg_ref, y_ref, htr_ref, hti_ref,
                    xcr_sc, xci_sc, hr_sc, hi_sc, yi_sc, *, q, nct, carry):
    tile = pl.program_id(1)
    u = u_ref[0]
    aq_r = pr_ref[q:q + 1, :]
    aq_i = pi_ref[q:q + 1, :]
    p1r = pr_ref[1:q + 1, :]
    p1i = pi_ref[1:q + 1, :]

    if carry:
        @pl.when(tile == 0)
        def _():
            hr_sc[...] = h0r_ref[0]
            hi_sc[...] = h0i_ref[0]

        def body(c, h):
            hr, hi = h
            row0 = pl.multiple_of(c * q, q)
            xcr_sc[pl.ds(row0, q), :] = p1r * hr - p1i * hi
            xci_sc[pl.ds(row0, q), :] = p1r * hi + p1i * hr
            sr = sre_ref[pl.ds(c, 1), :]
            si = sim_ref[pl.ds(c, 1), :]
            return aq_r * hr - aq_i * hi + sr, aq_r * hi + aq_i * hr + si

        hr, hi = lax.fori_loop(0, nct, body, (hr_sc[...], hi_sc[...]))
        hr_sc[...] = hr
        hi_sc[...] = hi

        @pl.when(tile == pl.num_programs(1) - 1)
        def _():
            htr_ref[0] = hr
            hti_ref[0] = hi
    else:
        def body(c, _):
            hr = h0r_ref[pl.ds(c, 1), :]
            hi = h0i_ref[pl.ds(c, 1), :]
            row0 = pl.multiple_of(c * q, q)
            xcr_sc[pl.ds(row0, q), :] = p1r * hr - p1i * hi
            xci_sc[pl.ds(row0, q), :] = p1r * hi + p1i * hr
            return 0

        lax.fori_loop(0, nct, body, 0)
        h0r = h0r_ref[...]
        h0i = h0i_ref[...]
        htr_ref[...] = aq_r * h0r - aq_i * h0i + sre_ref[...]
        hti_ref[...] = aq_r * h0i + aq_i * h0r + sim_ref[...]

    cols = []
    for m in range(4):
        xr = xcr_sc[:, 512 * m:512 * m + 512].astype(BF16)
        xi = xci_sc[:, 512 * m:512 * m + 512].astype(BF16)
        cols.append(jnp.dot(xr, wcr_ref[m], preferred_element_type=F32)
                    + jnp.dot(xi, wci_ref[m], preferred_element_type=F32))
    for s in range(q):
        for k in range(WIDTH // LANES):
            lo_lane = s * WIDTH + LANES * k
            yi_sc[k, pl.ds(s, nct, stride=q), :] = yi_ref[:, lo_lane:lo_lane + LANES]
    yi = jnp.concatenate([yi_sc[k] for k in range(WIDTH // LANES)], axis=-1)
    y = yi + jnp.concatenate(cols, axis=-1) + d_ref[...] * u
    y = jax.nn.gelu(y)
    g = jnp.dot(y.astype(BF16), wg_ref[...], preferred_element_type=F32) + bg_ref[...]
    y_ref[0] = (y * _sigmoid(g)).astype(y_ref.dtype)


def _s5_scan(u3, yi2, s_re, s_im, h0_re, h0_im, tab, d_row, w_glu, b_glu, q, tm, carry):
    bsz, seq, _ = u3.shape
    nct = tm // q
    n_tiles = seq // tm
    tok = pl.BlockSpec((1, tm, WIDTH), lambda b, t: (b, t, 0))
    chunk = pl.BlockSpec((nct, S5_LANES), lambda b, t: (b * n_tiles + t, 0))
    chunk_y = pl.BlockSpec((nct, q * WIDTH), lambda b, t: (b * n_tiles + t, 0))
    if carry:
        h_spec = pl.BlockSpec((1, 1, S5_LANES), lambda b, t: (b, 0, 0))
        h_shape = jax.ShapeDtypeStruct((bsz, 1, S5_LANES), F32)
    else:
        h_spec = pl.BlockSpec((nct, S5_LANES), lambda b, t: (t, 0))
        h_shape = jax.ShapeDtypeStruct((seq // q, S5_LANES), F32)
    consts = [tab['wc_re'], tab['wc_im'], tab['p_re'], tab['p_im'], d_row, w_glu, b_glu]
    return pl.pallas_call(
        functools.partial(_s5_scan_kernel, q=q, nct=nct, carry=carry),
        grid=(bsz, n_tiles),
        in_specs=[tok, chunk_y, chunk, chunk, h_spec, h_spec] + [_const_spec(c.shape) for c in consts],
        out_specs=(tok, h_spec, h_spec),
        out_shape=(jax.ShapeDtypeStruct((bsz, seq, WIDTH), BF16), h_shape, h_shape),
        scratch_shapes=[pltpu.VMEM((tm, S5_LANES), F32), pltpu.VMEM((tm, S5_LANES), F32),
                        pltpu.VMEM((1, S5_LANES), F32), pltpu.VMEM((1, S5_LANES), F32),
                        pltpu.VMEM((WIDTH // LANES, tm, LANES), F32)],
        compiler_params=_params("parallel", "arbitrary"),
        name="s5_scan",
    )(u3, yi2, s_re, s_im, h0_re, h0_im, *consts)


def _ssd_kernel(z_ref, xbc_ref, sm_ref, smt_ref, conv0_ref, h0_ref, cw_ref, cb_ref, alog_row_ref,
                alog_col_ref, e_ref, d_ref, nw_ref, y_ref, convt_ref, ht_ref,
                xp_sc, st_sc, zp_sc, *, nv, q):
    c = pl.program_id(1)
    last = c == pl.num_programs(1) - 1

    @pl.when(c == 0)
    def _():
        st_sc[...] = h0_ref[0, 0]
        xp_sc[...] = jnp.zeros(xp_sc.shape, F32)
        xp_sc[8 - (SSD_CONV - 1):8, :] = conv0_ref[0]
        if nv < q:
            zp_sc[...] = jnp.zeros(zp_sc.shape, F32)

    xp_sc[8:8 + nv, :] = xbc_ref[0]
    acc = None
    for j in range(SSD_CONV):
        o = 8 - (SSD_CONV - 1) + j
        term = xp_sc[o:o + q, :] * cw_ref[j:j + 1, :]
        acc = term if acc is None else acc + term
    xc = _silu(acc + cb_ref[...])
    tail = xp_sc[8 + nv - (SSD_CONV - 1):8 + nv, :]

    @pl.when(last)
    def _():
        convt_ref[0] = tail

    xp_sc[8 - (SSD_CONV - 1):8, :] = tail
    if nv < q:
        zp_sc[0:nv, :] = z_ref[0]
        z = zp_sc[...]
    else:
        z = z_ref[0]

    xs = xc[:, :WIDTH]
    bm = xc[:, WIDTH:WIDTH + SSD_GROUPS * SSD_STATE]
    cm = xc[:, WIDTH + SSD_GROUPS * SSD_STATE:]

    sm = sm_ref[0]
    smt = smt_ref[0]
    lane = lax.broadcasted_iota(jnp.int32, (1, LANES), 1)
    a_row = jnp.where((lane >= HEADS) & (lane < 2 * HEADS), -jnp.exp(alog_row_ref[...]), 0.0)
    rowi = lax.broadcasted_iota(jnp.int32, (2 * HEADS, 1), 0)
    a_col = jnp.where(rowi >= HEADS, -jnp.exp(alog_col_ref[...]), 0.0)
    adt = sm * a_row
    adt_t = smt * a_col
    r_i = lax.broadcasted_iota(jnp.int32, (q, q), 0)
    c_i = lax.broadcasted_iota(jnp.int32, (q, q), 1)
    tri = r_i >= c_i
    tri_lo = tri.astype(BF16)
    tri_up = (r_i <= c_i).astype(BF16)
    acum = None
    acum_t = None
    for piece, piece_t in zip(_bf16_pieces(adt), _bf16_pieces(adt_t)):
        d = jnp.dot(tri_lo, piece, preferred_element_type=F32)
        dt = jnp.dot(piece_t, tri_up, preferred_element_type=F32)
        acum = d if acum is None else acum + d
        acum_t = dt if acum_t is None else acum_t + dt
    atot = acum[q - 1:q, :]
    stacked = jnp.concatenate([sm, jnp.exp(acum), jnp.exp(atot - acum)], axis=0)
    hi, mid, lo = _bf16_pieces(stacked)
    e = e_ref[...]
    expanded = (jnp.dot(hi, e, preferred_element_type=F32) + jnp.dot(mid, e, preferred_element_type=F32)
                + jnp.dot(lo, e, preferred_element_type=F32))
    dt_e = expanded[0:q]
    eac_e = expanded[q:2 * q]
    dec_e = expanded[2 * q:3 * q]
    xd = xs * dt_e
    xdd = xd * dec_e
    lane_half = lax.broadcasted_iota(jnp.int32, (q, LANES), 1) // HEAD_DIM

    y_cols = []
    for g in range(SSD_GROUPS):
        bg = bm[:, SSD_STATE * g:SSD_STATE * (g + 1)].astype(BF16)
        cg = cm[:, SSD_STATE * g:SSD_STATE * (g + 1)].astype(BF16)
        cb = lax.dot_general(cg, bg, NT_DIMS, preferred_element_type=F32)
        hpg = HEADS // SSD_GROUPS
        st_g = st_sc[hpg * HEAD_DIM * g:hpg * HEAD_DIM * (g + 1), :]
        y_off = lax.dot_general(cg, st_g.astype(BF16), NT_DIMS, preferred_element_type=F32)
        for pair in range(hpg // 2):
            ys = []
            for hh in range(2):
                h = hpg * g + 2 * pair + hh
                diff = acum[:, HEADS + h:HEADS + h + 1] - acum_t[HEADS + h:HEADS + h + 1, :]
                lmat = jnp.exp(jnp.where(tri, diff, NEG))
                w = (cb * lmat).astype(BF16)
                lo = LANES * (2 * g + pair)
                ys.append(jnp.dot(w, xd[:, lo:lo + LANES].astype(BF16), preferred_element_type=F32))
            y_cols.append(jnp.where(lane_half == 0, ys[0], ys[1]))
        y_cols[-2] = y_cols[-2] + y_off[:, :LANES] * eac_e[:, 256 * g:256 * g + LANES]
        y_cols[-1] = y_cols[-1] + y_off[:, LANES:] * eac_e[:, 256 * g + LANES:256 * (g + 1)]
        contrib = jnp.dot(xdd[:, 256 * g:256 * (g + 1)].T.astype(BF16), bg, preferred_element_type=F32)
        for hl in range(hpg):
            h = hpg * g + hl
            sl = slice(HEAD_DIM * h, HEAD_DIM * (h + 1))
            dec_h = jnp.exp(acum_t[HEADS + h:HEADS + h + 1, q - 1:q])
            st_sc[sl, :] = st_sc[sl, :] * dec_h + contrib[HEAD_DIM * hl:HEAD_DIM * (hl + 1), :]

    y = jnp.concatenate(y_cols, axis=-1) + d_ref[...] * xs
    y = y * _silu(z)
    y = y * lax.rsqrt(jnp.mean(y * y, axis=-1, keepdims=True) + RMS_EPS) * nw_ref[...]
    y_ref[0] = y[:nv].astype(y_ref.dtype)

    @pl.when(last)
    def _():
        ht_ref[0] = st_sc[...]


def _ssd(z3, xbc3, sm3, smt3, conv0, h0, layer, cw, cb, alog_row, alog_col, e_mat, d_row, nw, nv, q):
    bsz, seq, _ = z3.shape
    nchunk = seq // nv
    consts = [cw, cb, alog_row, alog_col, e_mat, d_row, nw]
    return pl.pallas_call(
        functools.partial(_ssd_kernel, nv=nv, q=q),
        grid=(bsz, nchunk),
        in_specs=[pl.BlockSpec((1, nv, WIDTH), lambda b, c: (b, c, 0)),
                  pl.BlockSpec((1, nv, CONV_DIM), lambda b, c: (b, c, 0)),
                  pl.BlockSpec((1, q, LANES), lambda b, c: (b, c, 0)),
                  pl.BlockSpec((1, 2 * HEADS, q), lambda b, c: (b, 0, c)),
                  pl.BlockSpec((1, SSD_CONV - 1, CONV_DIM), lambda b, c: (b, 0, 0)),
                  pl.BlockSpec((1, 1, WIDTH, SSD_STATE), lambda b, c: (layer, b, 0, 0))]
                 + [_const_spec(x.shape) for x in consts],
        out_specs=(pl.BlockSpec((1, nv, WIDTH), lambda b, c: (b, c, 0)),
                   pl.BlockSpec((1, SSD_CONV - 1, CONV_DIM), lambda b, c: (b, 0, 0)),
                   pl.BlockSpec((1, WIDTH, SSD_STATE), lambda b, c: (b, 0, 0))),
        out_shape=(jax.ShapeDtypeStruct((bsz, seq, WIDTH), BF16 if nv % 16 == 0 else F32),
                   jax.ShapeDtypeStruct((bsz, SSD_CONV - 1, CONV_DIM), F32),
                   jax.ShapeDtypeStruct((bsz, WIDTH, SSD_STATE), F32)),
        scratch_shapes=[pltpu.VMEM((8 + q, CONV_DIM), F32), pltpu.VMEM((WIDTH, SSD_STATE), F32),
                        pltpu.VMEM((q, WIDTH), F32)],
        compiler_params=_params("parallel", "arbitrary"),
        name="ssd",
    )(z3, xbc3, sm3, smt3, conv0, h0, *consts)


def _merge_kernel(x_ref, ys5_ref, yfox_ref, yssd_ref, wg_ref, bg_ref, ws5_ref, wfox_ref, wssd_ref,
                  wo_ref, g_ref, b_ref, o_ref):
    x = x_ref[...]
    xb = x.astype(BF16)
    merged = None
    for br, (y_ref, w_ref) in enumerate(((ys5_ref, ws5_ref), (yfox_ref, wfox_ref), (yssd_ref, wssd_ref))):
        lo = br * D_MODEL
        gate = _sigmoid(jnp.dot(xb, wg_ref[:, lo:lo + D_MODEL], preferred_element_type=F32)
                        + bg_ref[:, lo:lo + D_MODEL])
        term = gate * jnp.dot(y_ref[...].astype(BF16), w_ref[...], preferred_element_type=F32)
        merged = term if merged is None else merged + term
    out = jnp.dot(merged.astype(BF16), wo_ref[...], preferred_element_type=F32)
    o_ref[...] = _layer_norm(ALPHA * x + out, g_ref[...], b_ref[...])


def _merge(x2, ys5, yfox, yssd, wg, bg, ws5, wfox, wssd, wo, g, b, tm):
    t = x2.shape[0]
    row = lambda w: pl.BlockSpec((tm, w), lambda i: (i, 0))
    consts = [wg, bg, ws5, wfox, wssd, wo, g, b]
    return pl.pallas_call(
        _merge_kernel,
        grid=(t // tm,),
        in_specs=[row(D_MODEL), row(WIDTH), row(WIDTH), row(WIDTH)] + [_const_spec(c.shape) for c in consts],
        out_specs=row(D_MODEL),
        out_shape=jax.ShapeDtypeStruct((t, D_MODEL), F32),
        compiler_params=_params("parallel"),
        name="merge",
    )(x2, ys5, yfox, yssd, *consts)


FF_CHUNK = 256


def _ffn_kernel(x_ref, wg_ref, wu_ref, wd_ref, g_ref, b_ref, o_ref):
    x = x_ref[...]
    xb = x.astype(BF16)
    acc = None
    for c in range(0, D_FF, FF_CHUNK):
        gt = jnp.dot(xb, wg_ref[:, c:c + FF_CHUNK], preferred_element_type=F32)
        up = jnp.dot(xb, wu_ref[:, c:c + FF_CHUNK], preferred_element_type=F32)
        h = (_silu(gt) * up).astype(BF16)
        d = jnp.dot(h, wd_ref[c:c + FF_CHUNK, :], preferred_element_type=F32)
        acc = d if acc is None else acc + d
    o_ref[...] = _layer_norm(ALPHA * x + acc, g_ref[...], b_ref[...])


def _ffn(x2, wg, wu, wd, g, b, tm):
    t = x2.shape[0]
    row = pl.BlockSpec((tm, D_MODEL), lambda i: (i, 0))
    consts = [wg, wu, wd, g, b]
    return pl.pallas_call(
        _ffn_kernel,
        grid=(t // tm,),
        in_specs=[row] + [_const_spec(c.shape) for c in consts],
        out_specs=row,
        out_shape=jax.ShapeDtypeStruct((t, D_MODEL), F32),
        compiler_params=_params("parallel"),
        name="ffn",
    )(x2, *consts)


def _moe_kernel(x_ref, wr_ref, br_ref, wg_ref, wu_ref, wd_ref, g_ref, b_ref, o_ref, gate_sc, acc_sc):
    e = pl.program_id(1)
    x = x_ref[...]
    xb = x.astype(BF16)
    lane = lax.broadcasted_iota(jnp.int32, (x.shape[0], LANES), 1)

    @pl.when(e == 0)
    def _():
        logits = jnp.dot(xb, wr_ref[...], preferred_element_type=F32) + br_ref[...]
        logits = jnp.where(lane < N_EXPERTS, logits, NEG)
        m1 = jnp.max(logits, axis=-1, keepdims=True)
        i1 = jnp.min(jnp.where(logits == m1, lane, LANES), axis=-1, keepdims=True)
        rest = jnp.where(lane == i1, NEG, logits)
        m2 = jnp.max(rest, axis=-1, keepdims=True)
        i2 = jnp.min(jnp.where(rest == m2, lane, LANES), axis=-1, keepdims=True)
        e2 = jnp.exp(m2 - m1)
        w1 = 1.0 / (1.0 + e2)
        w2 = e2 / (1.0 + e2)
        gate_sc[...] = jnp.where(lane == i1, w1, jnp.where(lane == i2, w2, 0.0))
        acc_sc[...] = jnp.zeros(acc_sc.shape, F32)

    gate_e = jnp.sum(jnp.where(lane == e, gate_sc[...], 0.0), axis=-1, keepdims=True)
    out_e = None
    for c in range(0, D_MODEL, FF_CHUNK):
        gt = jnp.dot(xb, wg_ref[0, :, c:c + FF_CHUNK], preferred_element_type=F32)
        up = jnp.dot(xb, wu_ref[0, :, c:c + FF_CHUNK], preferred_element_type=F32)
        h = (_silu(gt) * up).astype(BF16)
        d = jnp.dot(h, wd_ref[0, c:c + FF_CHUNK, :], preferred_element_type=F32)
        out_e = d if out_e is None else out_e + d
    acc_sc[...] += gate_e * out_e

    @pl.when(e == N_EXPERTS - 1)
    def _():
        o_ref[...] = _layer_norm(ALPHA * x + acc_sc[...], g_ref[...], b_ref[...])


def _moe(x2, wr, br, wg, wu, wd, g, b, tm):
    t = x2.shape[0]
    row = pl.BlockSpec((tm, D_MODEL), lambda i, e: (i, 0))
    wspec = pl.BlockSpec((1, D_MODEL, D_MODEL), lambda i, e: (e, 0, 0))
    return pl.pallas_call(
        _moe_kernel,
        grid=(t // tm, N_EXPERTS),
        in_specs=[row, _const_spec(wr.shape), _const_spec(br.shape), wspec, wspec, wspec,
                  _const_spec(g.shape), _const_spec(b.shape)],
        out_specs=row,
        out_shape=jax.ShapeDtypeStruct((t, D_MODEL), F32),
        scratch_shapes=[pltpu.VMEM((tm, LANES), F32), pltpu.VMEM((tm, D_MODEL), F32)],
        compiler_params=_params("parallel", "arbitrary"),
        name="moe",
    )(x2, wr, br, wg, wu, wd, g, b)


def _row(v, width=None):
    v = v.reshape(1, -1).astype(F32)
    if width is not None and v.shape[1] < width:
        v = jnp.pad(v, ((0, 0), (0, width - v.shape[1])))
    return v


def _layer_weights(p, l):
    w_in = p['w_in'][l]
    o = np.cumsum([N_BRANCH * D_MODEL, WIDTH, WIDTH, WIDTH, WIDTH, HEADS, WIDTH, CONV_DIM, HEADS])
    gates, u, q, k, v, fg, z, xbc, dtw = (w_in[:, a:b] for a, b in zip([0] + list(o[:-1]), o))
    w = {}
    w['w_main'] = jnp.concatenate([u, q, k, v, z, xbc], axis=1).astype(BF16)
    slab = ((0, 0), (0, 0), (0, LANES - HEAD_DIM))
    k_slab = jnp.pad(k.reshape(D_MODEL, HEADS, HEAD_DIM), slab).reshape(D_MODEL, HEADS * LANES)
    q_slab = jnp.pad(q.reshape(D_MODEL, HEADS, HEAD_DIM), slab).reshape(D_MODEL, HEADS * LANES)
    w['w_rows'] = jnp.concatenate([u, k_slab, z, xbc], axis=1).astype(BF16)
    w['w_cols'] = jnp.concatenate([q_slab, k, v], axis=1).T.astype(BF16)
    e3 = np.zeros((BIAS_LANES, LANES, HEADS * LANES), np.float32)
    for piece in range(BIAS_LANES):
        for h in range(HEADS):
            e3[piece, h, LANES * h + HEAD_DIM + piece] = 1.0
    w['bias_place'] = jnp.asarray(e3.reshape(BIAS_LANES * LANES, HEADS * LANES), BF16)
    w['w_small'] = jnp.pad(jnp.concatenate([fg, dtw], axis=1), ((0, 0), (0, LANES - 2 * HEADS))).astype(BF16)
    w['b_small'] = _row(jnp.concatenate([p['b_fgate'][l], p['ssd_dt_bias'][l]]), LANES)
    w['w_gates'] = gates.astype(BF16)
    w['b_gates'] = _row(p['b_gate'][l])
    for name in ('w_branch_s5', 'w_branch_fox', 'w_branch_ssd', 'w_o'):
        w[name] = p[name][l].astype(BF16)
    for name in ('ln1_g', 'ln1_b', 'ln2_g', 'ln2_b', 's5_d', 's5_b_glu', 'ssd_conv_b', 'ssd_norm_w'):
        w[name] = _row(p[name][l])
    w['s5_w_glu'] = p['s5_w_glu'][l].astype(BF16)
    w['ssd_conv_w'] = p['ssd_conv_w'][l].astype(F32)
    w['ssd_alog_row'] = _row(jnp.concatenate([jnp.zeros((HEADS,), F32), p['ssd_a_log'][l]]), LANES)
    w['ssd_alog_col'] = jnp.concatenate([jnp.zeros((HEADS,), F32), p['ssd_a_log'][l]]).reshape(2 * HEADS, 1)
    w['ssd_d'] = _row(jnp.repeat(p['ssd_d'][l], HEAD_DIM))
    e = np.zeros((LANES, WIDTH), np.float32)
    for h in range(HEADS):
        e[HEADS + h, HEAD_DIM * h:HEAD_DIM * (h + 1)] = 1.0
    w['ssd_expand'] = jnp.asarray(e, BF16)
    if l % 2 == 0:
        for name in ('ffn_w_gate', 'ffn_w_up', 'ffn_w_down'):
            w[name] = p[name][l // 2].astype(BF16)
    else:
        w['moe_w_router'] = jnp.pad(p['moe_w_router'][l // 2], ((0, 0), (0, LANES - N_EXPERTS))).astype(BF16)
        w['moe_b_router'] = _row(p['moe_b_router'][l // 2], LANES)
        for name in ('moe_w_gate', 'moe_w_up', 'moe_w_down'):
            w[name] = p[name][l // 2].astype(BF16)
    return w


def _s5_q(seq):
    return min(S5_CHUNK, seq)


def _run_trunk(x, p, weights, s5_tabs, s5_re0, s5_im0, conv0, ssd0, paged):
    bsz, seq, _ = x.shape
    t = bsz * seq
    tm_mix = min(512, t)
    fox_tile = min(512, seq)
    x2 = x.reshape(t, D_MODEL)
    new = [[] for _ in range(7)]
    q5 = _s5_q(seq)
    ssd0 = ssd0.reshape(ssd0.shape[0], bsz, WIDTH, SSD_STATE)
    if paged is not None:
        cache_k, cache_v, cache_lf, page_table = paged
        cache_kt = cache_k.transpose(0, 1, 3, 4, 2)
        cache_vt = cache_v.transpose(0, 1, 3, 4, 2)
        cache_lft = cache_lf.transpose(0, 1, 3, 2)
    for l in range(len(weights)):
        w = weights[l]
        if paged is None:
            u, ub, z, xbc, small, smt3, kaug, qaugt, kt, vt, tot = _in_proj_prompt(
                x2.reshape(bsz, seq, D_MODEL), w['w_rows'], w['w_cols'], w['w_small'], w['b_small'],
                w['bias_place'], fox_tile)
            k_out = kt.reshape(bsz, HEADS, HEAD_DIM, seq).transpose(0, 3, 1, 2)
            v_out = vt.reshape(bsz, HEADS, HEAD_DIM, seq).transpose(0, 3, 1, 2)
            logf_out = smt3[:, :HEADS, :].transpose(0, 2, 1)
        else:
            u, ub, qb, k, v, z, xbc, small, small_t, logf = _in_proj_sample(
                x2, w['w_main'], w['w_small'], w['b_small'], min(256, t))
            k_out = k.reshape(bsz, seq, HEADS, HEAD_DIM)
            v_out = v.reshape(bsz, seq, HEADS, HEAD_DIM)
            logf_out = logf.reshape(bsz, seq, HEADS)
            smt3 = small_t.reshape(2 * HEADS, bsz, seq).transpose(1, 0, 2)

        tab = s5_tabs[q5][l]
        yi, cs_re, cs_im = _s5_chunks(ub.reshape(t // q5, q5 * WIDTH), tab, q5, min(128, t // q5))
        d_row, w_glu, b_glu = w['s5_d'], w['s5_w_glu'], w['s5_b_glu']
        if paged is None:
            y_s5, s5_re, s5_im = _s5_scan(u.reshape(bsz, seq, WIDTH), yi, cs_re, cs_im,
                                          s5_re0[l].reshape(bsz, 1, S5_LANES), s5_im0[l].reshape(bsz, 1, S5_LANES),
                                          tab, d_row, w_glu, b_glu, q5, min(256, seq), True)
        else:
            y_s5, s5_re, s5_im = _s5_scan(u.reshape(1, t, WIDTH), yi, cs_re, cs_im,
                                          s5_re0[l].reshape(bsz, S5_LANES), s5_im0[l].reshape(bsz, S5_LANES),
                                          tab, d_row, w_glu, b_glu, q5, min(256, t), False)
        y_s5 = y_s5.reshape(t, WIDTH)
        s5_re = s5_re.reshape(bsz, S5_GROUPS, S5_STATE)
        s5_im = s5_im.reshape(bsz, S5_GROUPS, S5_STATE)

        if paged is None:
            y_fox = _fox_prompt(qaugt, kaug, vt, tot, fox_tile)
        else:
            lfn_t = jnp.pad(logf.reshape(bsz, seq, HEADS).transpose(0, 2, 1),
                            ((0, 0), (0, 0), (0, LANES - seq)))
            y_fox = _fox_sample(qb.astype(F32).reshape(bsz, seq, WIDTH), k.reshape(bsz, seq, WIDTH),
                                v.reshape(bsz, seq, WIDTH), lfn_t, cache_kt, cache_vt, cache_lft, l, page_table)
        y_fox = y_fox.reshape(t, WIDTH)

        nv = math.gcd(seq, SSD_CHUNK)
        q_ssd = max(nv, SSD_MIN_ROWS)
        sm3 = small.reshape(bsz, seq, LANES)
        if nv < q_ssd:
            sm3 = jnp.pad(sm3, ((0, 0), (0, q_ssd - nv), (0, 0)))
            smt3 = jnp.pad(smt3, ((0, 0), (0, 0), (0, q_ssd - nv)))
        y_ssd, conv_new, ssd_new = _ssd(
            z.reshape(bsz, seq, WIDTH), xbc.reshape(bsz, seq, CONV_DIM), sm3, smt3, conv0[l],
            ssd0, l, w['ssd_conv_w'], w['ssd_conv_b'], w['ssd_alog_row'],
            w['ssd_alog_col'], w['ssd_expand'], w['ssd_d'], w['ssd_norm_w'], nv, q_ssd)
        y_ssd = y_ssd.reshape(t, WIDTH)
        ssd_new = ssd_new.reshape(bsz, HEADS, HEAD_DIM, SSD_STATE)

        x2 = _merge(x2, y_s5, y_fox, y_ssd, w['w_gates'], w['b_gates'], w['w_branch_s5'],
                    w['w_branch_fox'], w['w_branch_ssd'], w['w_o'], w['ln1_g'], w['ln1_b'], tm_mix)
        if l % 2 == 0:
            x2 = _ffn(x2, w['ffn_w_gate'], w['ffn_w_up'], w['ffn_w_down'], w['ln2_g'], w['ln2_b'], tm_mix)
        else:
            x2 = _moe(x2, w['moe_w_router'], w['moe_b_router'], w['moe_w_gate'], w['moe_w_up'],
                      w['moe_w_down'], w['ln2_g'], w['ln2_b'], tm_mix)

        st = (k_out, v_out, logf_out, s5_re, s5_im, conv_new, ssd_new)
        for lst, s in zip(new, st):
            lst.append(s)
    return x2.reshape(bsz, seq, D_MODEL), [jnp.stack(s) for s in new]


def kernel(x_prompt, x_sample, cache_k, cache_v, cache_logf, page_table, state_s5_re, state_s5_im,
           state_conv, state_ssd, w_in, b_gate, b_fgate, s5_lam_re, s5_lam_im, s5_log_dt, s5_b_re,
           s5_b_im, s5_c_re, s5_c_im, s5_d, s5_w_glu, s5_b_glu, ssd_conv_w, ssd_conv_b, ssd_dt_bias,
           ssd_a_log, ssd_d, ssd_norm_w, w_branch_s5, w_branch_fox, w_branch_ssd, w_o, ln1_g, ln1_b,
           ln2_g, ln2_b, ffn_w_gate, ffn_w_up, ffn_w_down, moe_w_router, moe_b_router, moe_w_gate,
           moe_w_up, moe_w_down):
    p = dict(w_in=w_in, b_gate=b_gate, b_fgate=b_fgate, s5_d=s5_d, s5_w_glu=s5_w_glu, s5_b_glu=s5_b_glu,
             ssd_conv_w=ssd_conv_w, ssd_conv_b=ssd_conv_b, ssd_dt_bias=ssd_dt_bias, ssd_a_log=ssd_a_log,
             ssd_d=ssd_d, ssd_norm_w=ssd_norm_w, w_branch_s5=w_branch_s5, w_branch_fox=w_branch_fox,
             w_branch_ssd=w_branch_ssd, w_o=w_o, ln1_g=ln1_g, ln1_b=ln1_b, ln2_g=ln2_g, ln2_b=ln2_b,
             ffn_w_gate=ffn_w_gate, ffn_w_up=ffn_w_up, ffn_w_down=ffn_w_down, moe_w_router=moe_w_router,
             moe_b_router=moe_b_router, moe_w_gate=moe_w_gate, moe_w_up=moe_w_up, moe_w_down=moe_w_down)
    depth = w_in.shape[0]
    weights = [_layer_weights(p, l) for l in range(depth)]
    qs = {_s5_q(x_prompt.shape[1]), _s5_q(x_sample.shape[1])}
    s5_tabs = _all_s5_tables(s5_lam_re, s5_lam_im, s5_log_dt, s5_b_re, s5_b_im, s5_c_re, s5_c_im, qs)
    bsz = x_prompt.shape[0]
    z_s5 = jnp.zeros((depth, bsz, S5_GROUPS, S5_STATE), F32)
    z_conv = jnp.zeros((depth, bsz, SSD_CONV - 1, CONV_DIM), F32)
    z_ssd = jnp.zeros((depth, bsz, HEADS, HEAD_DIM, SSD_STATE), F32)
    y_p, st_p = _run_trunk(x_prompt, p, weights, s5_tabs, z_s5, z_s5, z_conv, z_ssd, None)
    y_s, st_s = _run_trunk(x_sample, p, weights, s5_tabs, state_s5_re, state_s5_im, state_conv, state_ssd,
                           (cache_k, cache_v, cache_logf, page_table))
    return (y_p, y_s, *st_p, *st_s)
```

```python
import functools
import math

import jax
import jax.numpy as jnp
import numpy as np
from jax import lax
from jax.experimental import pallas as pl
from jax.experimental.pallas import tpu as pltpu

F32 = jnp.float32
BF16 = jnp.bfloat16
HIGHEST = lax.Precision.HIGHEST

D_MODEL = 1024
N_BRANCH = 3
WIDTH = 512
S5_GROUPS = 32
S5_GROUP = 16
S5_STATE = 64
S5_LANES = S5_GROUPS * S5_STATE
S5_CHUNK = 16
HEADS = 8
HEAD_DIM = 64
SSD_STATE = 128
SSD_GROUPS = 2
SSD_CONV = 4
SSD_CHUNK = 128
SSD_MIN_ROWS = 16
SSD_SEQS_PER_STEP = 8
CONV_DIM = 1024
D_FF = 2816
N_EXPERTS = 8
PAGE = 128
ALPHA = 4.0 ** 0.25
LN_EPS = 1e-5
RMS_EPS = 1e-5
NEG = -1e30
LANES = 128
VMEM_LIMIT = 56 * 1024 * 1024

NT_DIMS = (((1,), (1,)), ((), ()))


def _params(*sem):
    return pltpu.CompilerParams(dimension_semantics=sem, vmem_limit_bytes=VMEM_LIMIT)


def _const_spec(shape):
    nd = len(shape)
    return pl.BlockSpec(shape, lambda *_: (0,) * nd, pipeline_mode=pl.Buffered(1))


def _layer_norm(x, g, b):
    mu = jnp.mean(x, axis=-1, keepdims=True)
    xc = x - mu
    var = jnp.mean(xc * xc, axis=-1, keepdims=True)
    return xc * lax.rsqrt(var + LN_EPS) * g + b


def _sigmoid(x):
    return 1.0 / (1.0 + jnp.exp(-x))


def _silu(x):
    return x * _sigmoid(x)


def _bf16_pieces(x):
    hi = x.astype(BF16)
    r1 = x - hi.astype(F32)
    mid = r1.astype(BF16)
    lo = (r1 - mid.astype(F32)).astype(BF16)
    return hi, mid, lo


def _small_act(xb, ws_ref, bs_ref):
    s = jnp.dot(xb, ws_ref[...], preferred_element_type=F32) + bs_ref[...]
    t = jnp.log1p(jnp.exp(-jnp.abs(s)))
    lane = lax.broadcasted_iota(jnp.int32, s.shape, 1)
    return jnp.where(lane < HEADS, jnp.minimum(s, 0.0) - t,
                     jnp.where(lane < 2 * HEADS, jnp.maximum(s, 0.0) + t, 0.0))


SEG_U, SEG_Q, SEG_K, SEG_V, SEG_Z, SEG_X, SEG_END = 0, 512, 1024, 1536, 2048, 2560, 3584


def _in_proj_sample_kernel(x_ref, wm_ref, ws_ref, bs_ref, u_ref, ub_ref, qb_ref, k_ref, v_ref,
                           z_ref, xbc_ref, small_ref, smallt_ref, logf_ref):
    xb = x_ref[...].astype(BF16)

    def seg(a, b):
        return jnp.dot(xb, wm_ref[:, a:b], preferred_element_type=F32)

    u = seg(SEG_U, SEG_Q)
    u_ref[...] = u
    ub_ref[...] = u.astype(BF16)
    qb_ref[...] = (seg(SEG_Q, SEG_K) * (HEAD_DIM ** -0.5)).astype(BF16)
    k_ref[...] = seg(SEG_K, SEG_V)
    v_ref[...] = seg(SEG_V, SEG_Z)
    z_ref[...] = seg(SEG_Z, SEG_X)
    xbc_ref[...] = seg(SEG_X, SEG_END)
    sm = _small_act(xb, ws_ref, bs_ref)
    small_ref[...] = sm
    smallt_ref[...] = sm.T[:2 * HEADS, :]
    logf_ref[...] = sm[:, :HEADS]


def _in_proj_sample(x2, wm, ws, bs, tm):
    t = x2.shape[0]
    row = lambda w: pl.BlockSpec((tm, w), lambda i: (i, 0))
    out_shape = (
        jax.ShapeDtypeStruct((t, WIDTH), F32),
        jax.ShapeDtypeStruct((t, WIDTH), BF16),
        jax.ShapeDtypeStruct((t, WIDTH), BF16),
        jax.ShapeDtypeStruct((t, WIDTH), F32),
        jax.ShapeDtypeStruct((t, WIDTH), F32),
        jax.ShapeDtypeStruct((t, WIDTH), F32),
        jax.ShapeDtypeStruct((t, CONV_DIM), F32),
        jax.ShapeDtypeStruct((t, LANES), F32),
        jax.ShapeDtypeStruct((2 * HEADS, t), F32),
        jax.ShapeDtypeStruct((t, HEADS), F32),
    )
    out_specs = (row(WIDTH), row(WIDTH), row(WIDTH), row(WIDTH), row(WIDTH),
                 row(WIDTH), row(CONV_DIM), row(LANES),
                 pl.BlockSpec((2 * HEADS, tm), lambda i: (0, i)), row(HEADS))
    return pl.pallas_call(
        _in_proj_sample_kernel,
        grid=(t // tm,),
        in_specs=[row(D_MODEL), _const_spec(wm.shape), _const_spec(ws.shape), _const_spec(bs.shape)],
        out_specs=out_specs,
        out_shape=out_shape,
        compiler_params=_params("parallel"),
        name="in_proj_sample",
    )(x2, wm, ws, bs)


PSEG_U, PSEG_K, PSEG_Z, PSEG_X, PSEG_END = 0, 512, 1536, 2048, 3072
TSEG_Q, TSEG_K, TSEG_V, TSEG_END = 0, 1024, 1536, 2048
BIAS_LANES = 3


def _in_proj_prompt_kernel(x_ref, wr_ref, wt_ref, ws_ref, bs_ref, e_ref, u_ref, ub_ref, z_ref, xbc_ref,
                           small_ref, smallt_ref, kaug_ref, qaugt_ref, kt_ref, vt_ref, tot_ref, u_sc):
    tm = x_ref.shape[0]
    xb = x_ref[...].astype(BF16)
    u = jnp.dot(xb, wr_ref[:, PSEG_U:PSEG_K], preferred_element_type=F32)
    u_ref[...] = u
    for k in range(WIDTH // LANES):
        u_sc[k] = u[:, LANES * k:LANES * (k + 1)]
    for s in range(S5_CHUNK):
        for k in range(WIDTH // LANES):
            lo_lane = s * WIDTH + LANES * k
            ub_ref[:, lo_lane:lo_lane + LANES] = u_sc[k, pl.ds(s, tm // S5_CHUNK, stride=S5_CHUNK), :].astype(BF16)
    z_ref[...] = jnp.dot(xb, wr_ref[:, PSEG_Z:PSEG_X], preferred_element_type=F32)
    xbc_ref[...] = jnp.dot(xb, wr_ref[:, PSEG_X:PSEG_END], preferred_element_type=F32)
    sm = _small_act(xb, ws_ref, bs_ref)
    small_ref[...] = sm
    smt = sm.T[:2 * HEADS, :]
    smallt_ref[0] = smt
    r_i = lax.broadcasted_iota(jnp.int32, (tm, tm), 0)
    c_i = lax.broadcasted_iota(jnp.int32, (tm, tm), 1)
    tri = (r_i >= c_i).astype(BF16)
    ones = jnp.ones((tm, tm), BF16)
    cloc = None
    tot = None
    for piece, piece_t in zip(_bf16_pieces(sm), _bf16_pieces(smt)):
        d = jnp.dot(tri, piece, preferred_element_type=F32)
        dt = jnp.dot(piece_t, ones, preferred_element_type=F32)
        cloc = d if cloc is None else cloc + d
        tot = dt if tot is None else tot + dt
    tot_ref[0, 0] = tot
    pieces = jnp.concatenate(_bf16_pieces(-cloc), axis=1)
    k_slabs = (jnp.dot(xb, wr_ref[:, PSEG_K:PSEG_Z], preferred_element_type=F32)
               + jnp.dot(pieces, e_ref[...], preferred_element_type=F32)).astype(BF16)
    for h in range(HEADS):
        kaug_ref[0, h] = k_slabs[:, LANES * h:LANES * (h + 1)]
    tall = lax.dot_general(wt_ref[...], xb, NT_DIMS, preferred_element_type=F32)
    row = lax.broadcasted_iota(jnp.int32, (LANES, tm), 0)
    ones_rows = (row >= HEAD_DIM) & (row < HEAD_DIM + BIAS_LANES)
    for h in range(HEADS):
        qh = tall[TSEG_Q + LANES * h:TSEG_Q + LANES * (h + 1), :]
        qaugt_ref[0, h] = jnp.where(ones_rows, 1.0, qh * (HEAD_DIM ** -0.5)).astype(BF16)
    kt_ref[0] = tall[TSEG_K:TSEG_V, :]
    vt_ref[0] = tall[TSEG_V:TSEG_END, :]


def _in_proj_prompt(x3, wr, wt, ws, bs, e3, tm):
    bsz, seq, _ = x3.shape
    nt = seq // tm
    x2 = x3.reshape(bsz * seq, D_MODEL)
    row = lambda w: pl.BlockSpec((tm, w), lambda b, i: (b * nt + i, 0))
    colt = lambda r: pl.BlockSpec((1, r, tm), lambda b, i: (b, 0, i))
    t = bsz * seq
    out_shape = (
        jax.ShapeDtypeStruct((t, WIDTH), F32),
        jax.ShapeDtypeStruct((t // S5_CHUNK, S5_CHUNK * WIDTH), BF16),
        jax.ShapeDtypeStruct((t, WIDTH), F32),
        jax.ShapeDtypeStruct((t, CONV_DIM), F32),
        jax.ShapeDtypeStruct((t, LANES), F32),
        jax.ShapeDtypeStruct((bsz, 2 * HEADS, seq), F32),
        jax.ShapeDtypeStruct((bsz, HEADS, seq, LANES), BF16),
        jax.ShapeDtypeStruct((bsz, HEADS, LANES, seq), BF16),
        jax.ShapeDtypeStruct((bsz, WIDTH, seq), F32),
        jax.ShapeDtypeStruct((bsz, WIDTH, seq), F32),
        jax.ShapeDtypeStruct((bsz, nt, 2 * HEADS, tm), F32),
    )
    out_specs = (row(WIDTH),
                 pl.BlockSpec((tm // S5_CHUNK, S5_CHUNK * WIDTH), lambda b, i: (b * nt + i, 0)),
                 row(WIDTH), row(CONV_DIM), row(LANES), colt(2 * HEADS),
                 pl.BlockSpec((1, HEADS, tm, LANES), lambda b, i: (b, 0, i, 0)),
                 pl.BlockSpec((1, HEADS, LANES, tm), lambda b, i: (b, 0, 0, i)),
                 colt(WIDTH), colt(WIDTH),
                 pl.BlockSpec((1, 1, 2 * HEADS, tm), lambda b, i: (b, i, 0, 0)))
    consts = [wr, wt, ws, bs, e3]
    return pl.pallas_call(
        _in_proj_prompt_kernel,
        grid=(bsz, nt),
        in_specs=[row(D_MODEL)] + [_const_spec(c.shape) for c in consts],
        out_specs=out_specs,
        out_shape=out_shape,
        scratch_shapes=[pltpu.VMEM((WIDTH // LANES, tm, LANES), F32)],
        compiler_params=_params("parallel", "parallel"),
        name="in_proj_prompt",
    )(x2, *consts)


def _cumsum_lanes(x):
    n = x.shape[-1]
    lane = lax.broadcasted_iota(jnp.int32, x.shape, x.ndim - 1)
    s = 1
    while s < n:
        x = x + jnp.where(lane >= s, pltpu.roll(x, s, axis=x.ndim - 1), 0.0)
        s *= 2
    return x


FOX_HEADS_PER_STEP = 8


def _fox_prompt_kernel(it_ref, jt_ref, q_ref, k_ref, v_ref, tot_ref, o_ref, m_sc, l_sc, c_sc, acc_sc, *, tq, tk):
    g = pl.program_id(1)
    t = pl.program_id(2)
    i = it_ref[t]
    j = jt_ref[t]
    hps = FOX_HEADS_PER_STEP

    @pl.when(j == 0)
    def _():
        m_sc[...] = jnp.full(m_sc.shape, NEG, F32)
        l_sc[...] = jnp.zeros(l_sc.shape, F32)
        c_sc[...] = jnp.zeros(c_sc.shape, F32)
        acc_sc[...] = jnp.zeros(acc_sc.shape, F32)

    def step(masked):
        if masked:
            kpos = j * tk + lax.broadcasted_iota(jnp.int32, (tk, tq), 0)
            qpos = i * tq + lax.broadcasted_iota(jnp.int32, (tk, tq), 1)
            causal = kpos <= qpos
        for h in range(hps):
            s = jnp.dot(k_ref[0, h], q_ref[0, h], preferred_element_type=F32)
            if masked:
                s = jnp.where(causal, s, NEG)
            c_j = c_sc[h]
            m_prev = m_sc[h]
            m_new = jnp.maximum(m_prev, jnp.max(s, axis=0, keepdims=True) - c_j)
            alpha = jnp.exp(m_prev - m_new)
            p = jnp.exp(s - (m_new + c_j))
            l_sc[h] = alpha * l_sc[h] + jnp.sum(p, axis=0, keepdims=True)
            rows = slice(HEAD_DIM * h, HEAD_DIM * (h + 1))
            acc_sc[rows, :] = alpha * acc_sc[rows, :] + jnp.dot(
                v_ref[0, rows, :].astype(BF16), p.astype(BF16), preferred_element_type=F32)
            m_sc[h] = m_new
            c_sc[h] = c_j + tot_ref[0, 0, pl.ds(g * hps + h, 1), :]

    @pl.when(j < i)
    def _():
        step(False)

    @pl.when(j == i)
    def _():
        step(True)
        inv = jnp.concatenate([jnp.broadcast_to(1.0 / l_sc[h], (HEAD_DIM, tq)) for h in range(hps)], axis=0)
        o_ref[0] = (acc_sc[...] * inv).T.astype(o_ref.dtype)


def _fox_prompt(qaugt, kaug, vt, tot, tq):
    bsz, _, _, seq = qaugt.shape
    nq = seq // tq
    hps = FOX_HEADS_PER_STEP
    it = np.array([i for i in range(nq) for j in range(i + 1)], np.int32)
    jt = np.array([j for i in range(nq) for j in range(i + 1)], np.int32)
    grid_spec = pltpu.PrefetchScalarGridSpec(
        num_scalar_prefetch=2,
        grid=(bsz, HEADS // hps, len(it)),
        in_specs=[
            pl.BlockSpec((1, hps, LANES, tq), lambda b, g, t, it, jt: (b, g, 0, it[t])),
            pl.BlockSpec((1, hps, tq, LANES), lambda b, g, t, it, jt: (b, g, jt[t], 0)),
            pl.BlockSpec((1, hps * HEAD_DIM, tq), lambda b, g, t, it, jt: (b, g, jt[t])),
            pl.BlockSpec((1, 1, 2 * HEADS, tq), lambda b, g, t, it, jt: (b, jt[t], 0, 0)),
        ],
        out_specs=pl.BlockSpec((1, tq, hps * HEAD_DIM), lambda b, g, t, it, jt: (b, it[t], g)),
        scratch_shapes=[pltpu.VMEM((hps, 1, tq), F32), pltpu.VMEM((hps, 1, tq), F32),
                        pltpu.VMEM((hps, 1, tq), F32), pltpu.VMEM((hps * HEAD_DIM, tq), F32)],
    )
    return pl.pallas_call(
        functools.partial(_fox_prompt_kernel, tq=tq, tk=tq),
        grid_spec=grid_spec,
        out_shape=jax.ShapeDtypeStruct((bsz, seq, WIDTH), BF16),
        compiler_params=_params("parallel", "parallel", "arbitrary"),
        name="fox_prompt",
    )(jnp.asarray(it), jnp.asarray(jt), qaugt, kaug, vt, tot)


FOX_SEQS_PER_STEP = 1


def _fox_sample_kernel(pt_ref, q_ref, kn_ref, vn_ref, lfn_ref, *refs, n_pages, tq, group):
    lf_ref = refs[2 * group * n_pages]
    o_ref = refs[2 * group * n_pages + 1]
    for s in range(group):
        one = pl.ds(s, 1)
        _fox_sample_sequence(pt_ref, pl.program_id(0) * group + s, q_ref.at[one], kn_ref.at[one],
                             vn_ref.at[one], lfn_ref.at[one], refs[s * n_pages:(s + 1) * n_pages],
                             refs[(group + s) * n_pages:(group + s + 1) * n_pages], lf_ref, o_ref.at[one],
                             n_pages=n_pages, tq=tq)


def _fox_sample_sequence(pt_ref, b, q_ref, kn_ref, vn_ref, lfn_ref, k_refs, v_refs, lf_ref, o_ref, *, n_pages, tq):
    rows = tq * HEADS
    q = q_ref[0].astype(F32)
    qe = jnp.concatenate([jnp.broadcast_to(q[t:t + 1, :], (HEADS, WIDTH)) for t in range(tq)],
                         axis=0)
    r_w = lax.broadcasted_iota(jnp.int32, (rows, WIDTH), 0)
    l_w = lax.broadcasted_iota(jnp.int32, (rows, WIDTH), 1)
    head_mask = (l_w // HEAD_DIM) == (r_w % HEADS)
    qbd = jnp.where(head_mask, qe, 0.0).astype(BF16)

    scores = [None] * n_pages
    carry = jnp.zeros((HEADS, 1), F32)
    for p in range(n_pages - 1, -1, -1):
        cs = _cumsum_lanes(lf_ref[0, pt_ref[b * n_pages + p]])
        tot = cs[:, PAGE - 1:PAGE]
        dec = (carry + tot) - cs
        carry = carry + tot
        kt = k_refs[p][0, 0].reshape(WIDTH, PAGE).astype(BF16)
        s = jnp.dot(qbd, kt, preferred_element_type=F32)
        scores[p] = s + jnp.concatenate([dec] * tq, axis=0)
    pad = jnp.zeros((PAGE - tq, WIDTH), F32)
    kn = jnp.concatenate([kn_ref[0], pad], axis=0).astype(BF16)
    vn = jnp.concatenate([vn_ref[0], pad], axis=0).astype(BF16)
    cn = _cumsum_lanes(lfn_ref[0])
    r_p = lax.broadcasted_iota(jnp.int32, (rows, PAGE), 0)
    c_p = lax.broadcasted_iota(jnp.int32, (rows, PAGE), 1)
    s_new = lax.dot_general(qbd, kn, NT_DIMS, preferred_element_type=F32)
    s_new = jnp.where(c_p <= r_p // HEADS, s_new - jnp.concatenate([cn] * tq, axis=0), NEG)

    m = s_new
    for p in range(n_pages):
        m = jnp.maximum(m, scores[p])
    m = jnp.max(m, axis=-1, keepdims=True)
    pn = jnp.exp(s_new - m)
    l = jnp.sum(pn, axis=-1, keepdims=True)
    o = jnp.dot(pn.astype(BF16), vn, preferred_element_type=F32)
    for p in range(n_pages):
        pp = jnp.exp(scores[p] - m)
        l = l + jnp.sum(pp, axis=-1, keepdims=True)
        vt = v_refs[p][0, 0].reshape(WIDTH, PAGE).astype(BF16)
        o = o + lax.dot_general(pp.astype(BF16), vt, NT_DIMS, preferred_element_type=F32)
    om = jnp.where(head_mask, o / l, 0.0)
    out = jnp.concatenate([jnp.sum(om[HEADS * t:HEADS * (t + 1), :], axis=0, keepdims=True) for t in range(tq)],
                          axis=0)
    o_ref[0] = out.astype(o_ref.dtype)


def _fox_sample(qb, k_new, v_new, lfn_t, cache_kt, cache_vt, cache_lft, layer, page_table):
    dbsz, tq, _ = qb.shape
    n_pages = page_table.shape[1]
    pt = page_table.reshape(-1).astype(jnp.int32)

    g = FOX_SEQS_PER_STEP if dbsz % FOX_SEQS_PER_STEP == 0 else 1

    def page_spec(s, p):
        return pl.BlockSpec((1, 1, HEADS, HEAD_DIM, PAGE),
                            lambda b, pt: (layer, pt[(b * g + s) * n_pages + p], 0, 0, 0))

    seq_spec = lambda shape: pl.BlockSpec(shape, lambda b, pt: (b, 0, 0))
    in_specs = [seq_spec((g, tq, WIDTH)), seq_spec((g, tq, WIDTH)), seq_spec((g, tq, WIDTH)),
                seq_spec((g, HEADS, LANES))]
    in_specs += [page_spec(s, p) for s in range(g) for p in range(n_pages)]
    in_specs += [page_spec(s, p) for s in range(g) for p in range(n_pages)]
    in_specs += [pl.BlockSpec((1,) + cache_lft.shape[1:], lambda b, pt: (layer, 0, 0, 0),
                              pipeline_mode=pl.Buffered(1))]
    grid_spec = pltpu.PrefetchScalarGridSpec(
        num_scalar_prefetch=1, grid=(dbsz // g,), in_specs=in_specs,
        out_specs=pl.BlockSpec((g, tq, WIDTH), lambda b, pt: (b, 0, 0)))
    return pl.pallas_call(
        functools.partial(_fox_sample_kernel, n_pages=n_pages, tq=tq, group=g),
        grid_spec=grid_spec,
        out_shape=jax.ShapeDtypeStruct((dbsz, tq, WIDTH), F32),
        compiler_params=_params("parallel"),
        name="fox_sample",
    )(pt, qb, k_new, v_new, lfn_t, *([cache_kt] * (g * n_pages)), *([cache_vt] * (g * n_pages)), cache_lft)


def _s5_tables(lam_re, lam_im, log_dt, b_re, b_im, c_re, c_im, q):
    hp = dict(precision=HIGHEST)
    dt = jnp.exp(log_dt)[:, None]
    mag = jnp.exp(lam_re * dt)
    ab_re = mag * jnp.cos(lam_im * dt)
    ab_im = mag * jnp.sin(lam_im * dt)
    nr = ab_re - 1.0
    den = lam_re * lam_re + lam_im * lam_im
    q_re = (nr * lam_re + ab_im * lam_im) / den
    q_im = (ab_im * lam_re - nr * lam_im) / den
    bb_re = q_re[..., None] * b_re - q_im[..., None] * b_im
    bb_im = q_re[..., None] * b_im + q_im[..., None] * b_re
    jj = jnp.arange(q + 1, dtype=F32)[:, None, None]
    pmag = jnp.exp(lam_re * dt * jj)
    p_re = pmag * jnp.cos(lam_im * dt * jj)
    p_im = pmag * jnp.sin(lam_im * dt * jj)
    ab_b_re = p_re[:q, :, :, None] * bb_re - p_im[:q, :, :, None] * bb_im
    ab_b_im = p_re[:q, :, :, None] * bb_im + p_im[:q, :, :, None] * bb_re
    kern = (jnp.einsum('gcn,jgnd->jgcd', c_re, ab_b_re, **hp)
            - jnp.einsum('gcn,jgnd->jgcd', c_im, ab_b_im, **hp))
    kt = kern.transpose(0, 1, 3, 2).reshape(q, 2, 16 * S5_GROUP, S5_GROUP)
    tile_cols = jnp.tile(jnp.eye(S5_GROUP, dtype=F32), (1, 16))
    r_g = lax.broadcasted_iota(jnp.int32, (256, 256), 0) // S5_GROUP
    c_g = lax.broadcasted_iota(jnp.int32, (256, 256), 1) // S5_GROUP
    w_toep = jnp.where(r_g == c_g, jnp.einsum('jxrb,bc->jxrc', kt, tile_cols, **hp), 0.0).astype(BF16)
    eye_g = jnp.eye(S5_GROUPS, dtype=F32)

    def in_map(bb):
        full = jnp.einsum('gnc,gh->gchn', bb, eye_g).reshape(WIDTH, S5_LANES)
        return jnp.stack([full[128 * (n // 2):128 * (n // 2) + 128, 256 * n:256 * n + 256]
                          for n in range(8)]).astype(BF16)

    def out_map(c):
        full = jnp.einsum('gcn,gh->gnhc', c, eye_g).reshape(S5_LANES, WIDTH)
        return jnp.stack([full[512 * m:512 * m + 512, 128 * m:128 * m + 128]
                          for m in range(4)]).astype(BF16)

    rows = ((q + 1 + 7) // 8) * 8
    padp = lambda p: jnp.pad(p.reshape(q + 1, S5_LANES), ((0, rows - q - 1), (0, 0)))
    return dict(w_toep=w_toep, wb_re=in_map(bb_re), wb_im=in_map(bb_im), wc_re=out_map(c_re),
                wc_im=out_map(-c_im), p_re=padp(p_re), p_im=padp(p_im))


def _all_s5_tables(lam_re, lam_im, log_dt, b_re, b_im, c_re, c_im, qs):
    f32 = lambda a: a.astype(F32)
    q_max = max(qs)
    full = jax.vmap(functools.partial(_s5_tables, q=q_max))(
        f32(lam_re), f32(lam_im), f32(log_dt), f32(b_re), f32(b_im), f32(c_re), f32(c_im))
    out = {}
    for q in qs:
        rows = ((q + 1 + 7) // 8) * 8
        out[q] = []
        for l in range(lam_re.shape[0]):
            tab = {name: arr[l] for name, arr in full.items()}
            tab['w_toep'] = tab['w_toep'][:q]
            tab['p_re'] = tab['p_re'][:rows]
            tab['p_im'] = tab['p_im'][:rows]
            out[q].append(tab)
    return out


def _s5_chunk_kernel(u_ref, w_ref, wbr_ref, wbi_ref, pr_ref, pi_ref, y_ref, sr_ref, si_ref, *, q):
    for s_out in range(q):
        for half in range(2):
            acc = None
            for j in range(s_out + 1):
                a = (s_out - j) * WIDTH + half * 256
                d = jnp.dot(u_ref[:, a:a + 256], w_ref[j, half], preferred_element_type=F32)
                acc = d if acc is None else acc + d
            o = s_out * WIDTH + half * 256
            y_ref[:, o:o + 256] = acc
    for n in range(8):
        lanes = slice(256 * n, 256 * (n + 1))
        acc_r = None
        acc_i = None
        for s in range(q):
            a = s * WIDTH + 128 * (n // 2)
            br = jnp.dot(u_ref[:, a:a + 128], wbr_ref[n], preferred_element_type=F32)
            bi = jnp.dot(u_ref[:, a:a + 128], wbi_ref[n], preferred_element_type=F32)
            pr = pr_ref[q - 1 - s:q - s, lanes]
            pi = pi_ref[q - 1 - s:q - s, lanes]
            tr = pr * br - pi * bi
            ti = pr * bi + pi * br
            acc_r = tr if acc_r is None else acc_r + tr
            acc_i = ti if acc_i is None else acc_i + ti
        sr_ref[:, lanes] = acc_r
        si_ref[:, lanes] = acc_i


def _s5_chunks(u2b, tab, q, rt):
    nc = u2b.shape[0]
    consts = [tab['w_toep'], tab['wb_re'], tab['wb_im'], tab['p_re'], tab['p_im']]
    state = pl.BlockSpec((rt, S5_LANES), lambda i: (i, 0))
    return pl.pallas_call(
        functools.partial(_s5_chunk_kernel, q=q),
        grid=(nc // rt,),
        in_specs=[pl.BlockSpec((rt, q * WIDTH), lambda i: (i, 0))] + [_const_spec(c.shape) for c in consts],
        out_specs=(pl.BlockSpec((rt, q * WIDTH), lambda i: (i, 0)), state, state),
        out_shape=(jax.ShapeDtypeStruct((nc, q * WIDTH), F32),
                   jax.ShapeDtypeStruct((nc, S5_LANES), F32), jax.ShapeDtypeStruct((nc, S5_LANES), F32)),
        compiler_params=_params("parallel"),
        name="s5_chunks",
    )(u2b, *consts)


def _s5_scan_kernel(u_ref, yi_ref, sre_ref, sim_ref, h0r_ref, h0i_ref, wcr_ref, wci_ref, pr_ref, pi_ref,
                    d_ref, wg_ref, bg_ref, y_ref, htr_ref, hti_ref,
                    xcr_sc, xci_sc, hr_sc, hi_sc, yi_sc, *, q, nct, carry):
    tile = pl.program_id(1)
    u = u_ref[0]
    aq_r = pr_ref[q:q + 1, :]
    aq_i = pi_ref[q:q + 1, :]
    p1r = pr_ref[1:q + 1, :]
    p1i = pi_ref[1:q + 1, :]

    if carry:
        @pl.when(tile == 0)
        def _():
            hr_sc[...] = h0r_ref[0]
            hi_sc[...] = h0i_ref[0]

        def body(c, h):
            hr, hi = h
            row0 = pl.multiple_of(c * q, q)
            xcr_sc[pl.ds(row0, q), :] = p1r * hr - p1i * hi
            xci_sc[pl.ds(row0, q), :] = p1r * hi + p1i * hr
            sr = sre_ref[pl.ds(c, 1), :]
            si = sim_ref[pl.ds(c, 1), :]
            return aq_r * hr - aq_i * hi + sr, aq_r * hi + aq_i * hr + si

        hr, hi = lax.fori_loop(0, nct, body, (hr_sc[...], hi_sc[...]))
        hr_sc[...] = hr
        hi_sc[...] = hi

        @pl.when(tile == pl.num_programs(1) - 1)
        def _():
            htr_ref[0] = hr
            hti_ref[0] = hi
    else:
        def body(c, _):
            hr = h0r_ref[pl.ds(c, 1), :]
            hi = h0i_ref[pl.ds(c, 1), :]
            row0 = pl.multiple_of(c * q, q)
            xcr_sc[pl.ds(row0, q), :] = p1r * hr - p1i * hi
            xci_sc[pl.ds(row0, q), :] = p1r * hi + p1i * hr
            return 0

        lax.fori_loop(0, nct, body, 0)
        h0r = h0r_ref[...]
        h0i = h0i_ref[...]
        htr_ref[...] = aq_r * h0r - aq_i * h0i + sre_ref[...]
        hti_ref[...] = aq_r * h0i + aq_i * h0r + sim_ref[...]

    cols = []
    for m in range(4):
        xr = xcr_sc[:, 512 * m:512 * m + 512].astype(BF16)
        xi = xci_sc[:, 512 * m:512 * m + 512].astype(BF16)
        cols.append(jnp.dot(xr, wcr_ref[m], preferred_element_type=F32)
                    + jnp.dot(xi, wci_ref[m], preferred_element_type=F32))
    for s in range(q):
        for k in range(WIDTH // LANES):
            lo_lane = s * WIDTH + LANES * k
            yi_sc[k, pl.ds(s, nct, stride=q), :] = yi_ref[:, lo_lane:lo_lane + LANES]
    yi = jnp.concatenate([yi_sc[k] for k in range(WIDTH // LANES)], axis=-1)
    y = yi + jnp.concatenate(cols, axis=-1) + d_ref[...] * u
    y = jax.nn.gelu(y)
    g = jnp.dot(y.astype(BF16), wg_ref[...], preferred_element_type=F32) + bg_ref[...]
    y_ref[0] = (y * _sigmoid(g)).astype(y_ref.dtype)


def _s5_scan(u3, yi2, s_re, s_im, h0_re, h0_im, tab, d_row, w_glu, b_glu, q, tm, carry):
    bsz, seq, _ = u3.shape
    nct = tm // q
    n_tiles = seq // tm
    tok = pl.BlockSpec((1, tm, WIDTH), lambda b, t: (b, t, 0))
    chunk = pl.BlockSpec((nct, S5_LANES), lambda b, t: (b * n_tiles + t, 0))
    chunk_y = pl.BlockSpec((nct, q * WIDTH), lambda b, t: (b * n_tiles + t, 0))
    if carry:
        h_spec = pl.BlockSpec((1, 1, S5_LANES), lambda b, t: (b, 0, 0))
        h_shape = jax.ShapeDtypeStruct((bsz, 1, S5_LANES), F32)
    else:
        h_spec = pl.BlockSpec((nct, S5_LANES), lambda b, t: (t, 0))
        h_shape = jax.ShapeDtypeStruct((seq // q, S5_LANES), F32)
    consts = [tab['wc_re'], tab['wc_im'], tab['p_re'], tab['p_im'], d_row, w_glu, b_glu]
    return pl.pallas_call(
        functools.partial(_s5_scan_kernel, q=q, nct=nct, carry=carry),
        grid=(bsz, n_tiles),
        in_specs=[tok, chunk_y, chunk, chunk, h_spec, h_spec] + [_const_spec(c.shape) for c in consts],
        out_specs=(tok, h_spec, h_spec),
        out_shape=(jax.ShapeDtypeStruct((bsz, seq, WIDTH), BF16), h_shape, h_shape),
        scratch_shapes=[pltpu.VMEM((tm, S5_LANES), F32), pltpu.VMEM((tm, S5_LANES), F32),
                        pltpu.VMEM((1, S5_LANES), F32), pltpu.VMEM((1, S5_LANES), F32),
                        pltpu.VMEM((WIDTH // LANES, tm, LANES), F32)],
        compiler_params=_params("parallel", "arbitrary"),
        name="s5_scan",
    )(u3, yi2, s_re, s_im, h0_re, h0_im, *consts)


def _ssd_kernel(z_ref, xbc_ref, sm_ref, smt_ref, conv0_ref, h0_ref, cw_ref, cb_ref, alog_row_ref,
                alog_col_ref, e_ref, d_ref, nw_ref, y_ref, convt_ref, ht_ref,
                xp_sc, st_sc, zp_sc, *, nv, q, group):
    for s in range(group):
        one = pl.ds(s, 1)
        _ssd_sequence(z_ref.at[one], xbc_ref.at[one], sm_ref.at[one], smt_ref.at[one], conv0_ref.at[one],
                      h0_ref.at[:, one], cw_ref, cb_ref, alog_row_ref, alog_col_ref, e_ref, d_ref, nw_ref,
                      y_ref.at[one], convt_ref.at[one], ht_ref.at[one],
                      xp_sc.at[s], st_sc.at[s], zp_sc.at[s], nv=nv, q=q)


def _ssd_sequence(z_ref, xbc_ref, sm_ref, smt_ref, conv0_ref, h0_ref, cw_ref, cb_ref, alog_row_ref,
                  alog_col_ref, e_ref, d_ref, nw_ref, y_ref, convt_ref, ht_ref,
                  xp_sc, st_sc, zp_sc, *, nv, q):
    c = pl.program_id(1)
    last = c == pl.num_programs(1) - 1

    @pl.when(c == 0)
    def _():
        st_sc[...] = h0_ref[0, 0]
        xp_sc[...] = jnp.zeros(xp_sc.shape, F32)
        xp_sc[8 - (SSD_CONV - 1):8, :] = conv0_ref[0]
        if nv < q:
            zp_sc[...] = jnp.zeros(zp_sc.shape, F32)

    xp_sc[8:8 + nv, :] = xbc_ref[0]
    acc = None
    for j in range(SSD_CONV):
        o = 8 - (SSD_CONV - 1) + j
        term = xp_sc[o:o + q, :] * cw_ref[j:j + 1, :]
        acc = term if acc is None else acc + term
    xc = _silu(acc + cb_ref[...])
    tail = xp_sc[8 + nv - (SSD_CONV - 1):8 + nv, :]

    @pl.when(last)
    def _():
        convt_ref[0] = tail

    xp_sc[8 - (SSD_CONV - 1):8, :] = tail
    if nv < q:
        zp_sc[0:nv, :] = z_ref[0]
        z = zp_sc[...]
    else:
        z = z_ref[0]

    xs = xc[:, :WIDTH]
    bm = xc[:, WIDTH:WIDTH + SSD_GROUPS * SSD_STATE]
    cm = xc[:, WIDTH + SSD_GROUPS * SSD_STATE:]

    sm = sm_ref[0]
    smt = smt_ref[0]
    lane = lax.broadcasted_iota(jnp.int32, (1, LANES), 1)
    a_row = jnp.where((lane >= HEADS) & (lane < 2 * HEADS), -jnp.exp(alog_row_ref[...]), 0.0)
    rowi = lax.broadcasted_iota(jnp.int32, (2 * HEADS, 1), 0)
    a_col = jnp.where(rowi >= HEADS, -jnp.exp(alog_col_ref[...]), 0.0)
    adt = sm * a_row
    adt_t = smt * a_col
    r_i = lax.broadcasted_iota(jnp.int32, (q, q), 0)
    c_i = lax.broadcasted_iota(jnp.int32, (q, q), 1)
    tri = r_i >= c_i
    tri_lo = tri.astype(BF16)
    tri_up = (r_i <= c_i).astype(BF16)
    acum = None
    acum_t = None
    for piece, piece_t in zip(_bf16_pieces(adt), _bf16_pieces(adt_t)):
        d = jnp.dot(tri_lo, piece, preferred_element_type=F32)
        dt = jnp.dot(piece_t, tri_up, preferred_element_type=F32)
        acum = d if acum is None else acum + d
        acum_t = dt if acum_t is None else acum_t + dt
    atot = acum[q - 1:q, :]
    stacked = jnp.concatenate([sm, jnp.exp(acum), jnp.exp(atot - acum)], axis=0)
    hi, mid, lo = _bf16_pieces(stacked)
    e = e_ref[...]
    expanded = (jnp.dot(hi, e, preferred_element_type=F32) + jnp.dot(mid, e, preferred_element_type=F32)
                + jnp.dot(lo, e, preferred_element_type=F32))
    dt_e = expanded[0:q]
    eac_e = expanded[q:2 * q]
    dec_e = expanded[2 * q:3 * q]
    xd = xs * dt_e
    xdd = xd * dec_e
    lane_half = lax.broadcasted_iota(jnp.int32, (q, LANES), 1) // HEAD_DIM

    y_cols = []
    for g in range(SSD_GROUPS):
        bg = bm[:, SSD_STATE * g:SSD_STATE * (g + 1)].astype(BF16)
        cg = cm[:, SSD_STATE * g:SSD_STATE * (g + 1)].astype(BF16)
        cb = lax.dot_general(cg, bg, NT_DIMS, preferred_element_type=F32)
        hpg = HEADS // SSD_GROUPS
        st_g = st_sc[hpg * HEAD_DIM * g:hpg * HEAD_DIM * (g + 1), :]
        y_off = lax.dot_general(cg, st_g.astype(BF16), NT_DIMS, preferred_element_type=F32)
        for pair in range(hpg // 2):
            ys = []
            for hh in range(2):
                h = hpg * g + 2 * pair + hh
                diff = acum[:, HEADS + h:HEADS + h + 1] - acum_t[HEADS + h:HEADS + h + 1, :]
                lmat = jnp.exp(jnp.where(tri, diff, NEG))
                w = (cb * lmat).astype(BF16)
                lo = LANES * (2 * g + pair)
                ys.append(jnp.dot(w, xd[:, lo:lo + LANES].astype(BF16), preferred_element_type=F32))
            y_cols.append(jnp.where(lane_half == 0, ys[0], ys[1]))
        y_cols[-2] = y_cols[-2] + y_off[:, :LANES] * eac_e[:, 256 * g:256 * g + LANES]
        y_cols[-1] = y_cols[-1] + y_off[:, LANES:] * eac_e[:, 256 * g + LANES:256 * (g + 1)]
        contrib = jnp.dot(xdd[:, 256 * g:256 * (g + 1)].T.astype(BF16), bg, preferred_element_type=F32)
        for hl in range(hpg):
            h = hpg * g + hl
            sl = slice(HEAD_DIM * h, HEAD_DIM * (h + 1))
            dec_h = jnp.exp(acum_t[HEADS + h:HEADS + h + 1, q - 1:q])
            st_sc[sl, :] = st_sc[sl, :] * dec_h + contrib[HEAD_DIM * hl:HEAD_DIM * (hl + 1), :]

    y = jnp.concatenate(y_cols, axis=-1) + d_ref[...] * xs
    y = y * _silu(z)
    y = y * lax.rsqrt(jnp.mean(y * y, axis=-1, keepdims=True) + RMS_EPS) * nw_ref[...]
    y_ref[0] = y[:nv].astype(y_ref.dtype)

    @pl.when(last)
    def _():
        ht_ref[0] = st_sc[...]


def _ssd(z3, xbc3, sm3, smt3, conv0, h0, layer, cw, cb, alog_row, alog_col, e_mat, d_row, nw, nv, q):
    bsz, seq, _ = z3.shape
    nchunk = seq // nv
    g = SSD_SEQS_PER_STEP if (nchunk == 1 and bsz % SSD_SEQS_PER_STEP == 0) else 1
    consts = [cw, cb, alog_row, alog_col, e_mat, d_row, nw]
    return pl.pallas_call(
        functools.partial(_ssd_kernel, nv=nv, q=q, group=g),
        grid=(bsz // g, nchunk),
        in_specs=[pl.BlockSpec((g, nv, WIDTH), lambda b, c: (b, c, 0)),
                  pl.BlockSpec((g, nv, CONV_DIM), lambda b, c: (b, c, 0)),
                  pl.BlockSpec((g, q, LANES), lambda b, c: (b, c, 0)),
                  pl.BlockSpec((g, 2 * HEADS, q), lambda b, c: (b, 0, c)),
                  pl.BlockSpec((g, SSD_CONV - 1, CONV_DIM), lambda b, c: (b, 0, 0)),
                  pl.BlockSpec((1, g, WIDTH, SSD_STATE), lambda b, c: (layer, b, 0, 0))]
                 + [_const_spec(x.shape) for x in consts],
        out_specs=(pl.BlockSpec((g, nv, WIDTH), lambda b, c: (b, c, 0)),
                   pl.BlockSpec((g, SSD_CONV - 1, CONV_DIM), lambda b, c: (b, 0, 0)),
                   pl.BlockSpec((g, WIDTH, SSD_STATE), lambda b, c: (b, 0, 0))),
        out_shape=(jax.ShapeDtypeStruct((bsz, seq, WIDTH), BF16 if nv % 16 == 0 else F32),
                   jax.ShapeDtypeStruct((bsz, SSD_CONV - 1, CONV_DIM), F32),
                   jax.ShapeDtypeStruct((bsz, WIDTH, SSD_STATE), F32)),
        scratch_shapes=[pltpu.VMEM((g, 8 + q, CONV_DIM), F32), pltpu.VMEM((g, WIDTH, SSD_STATE), F32),
                        pltpu.VMEM((g, q, WIDTH), F32)],
        compiler_params=_params("parallel", "arbitrary"),
        name="ssd",
    )(z3, xbc3, sm3, smt3, conv0, h0, *consts)


def _merge_kernel(x_ref, ys5_ref, yfox_ref, yssd_ref, wg_ref, bg_ref, ws5_ref, wfox_ref, wssd_ref,
                  wo_ref, g_ref, b_ref, o_ref):
    x = x_ref[...]
    xb = x.astype(BF16)
    merged = None
    for br, (y_ref, w_ref) in enumerate(((ys5_ref, ws5_ref), (yfox_ref, wfox_ref), (yssd_ref, wssd_ref))):
        lo = br * D_MODEL
        gate = _sigmoid(jnp.dot(xb, wg_ref[:, lo:lo + D_MODEL], preferred_element_type=F32)
                        + bg_ref[:, lo:lo + D_MODEL])
        term = gate * jnp.dot(y_ref[...].astype(BF16), w_ref[...], preferred_element_type=F32)
        merged = term if merged is None else merged + term
    out = jnp.dot(merged.astype(BF16), wo_ref[...], preferred_element_type=F32)
    o_ref[...] = _layer_norm(ALPHA * x + out, g_ref[...], b_ref[...])


def _merge(x2, ys5, yfox, yssd, wg, bg, ws5, wfox, wssd, wo, g, b, tm):
    t = x2.shape[0]
    row = lambda w: pl.BlockSpec((tm, w), lambda i: (i, 0))
    consts = [wg, bg, ws5, wfox, wssd, wo, g, b]
    return pl.pallas_call(
        _merge_kernel,
        grid=(t // tm,),
        in_specs=[row(D_MODEL), row(WIDTH), row(WIDTH), row(WIDTH)] + [_const_spec(c.shape) for c in consts],
        out_specs=row(D_MODEL),
        out_shape=jax.ShapeDtypeStruct((t, D_MODEL), F32),
        compiler_params=_params("parallel"),
        name="merge",
    )(x2, ys5, yfox, yssd, *consts)


FF_CHUNK = 256


def _ffn_kernel(x_ref, wg_ref, wu_ref, wd_ref, g_ref, b_ref, o_ref):
    x = x_ref[...]
    xb = x.astype(BF16)
    acc = None
    for c in range(0, D_FF, FF_CHUNK):
        gt = jnp.dot(xb, wg_ref[:, c:c + FF_CHUNK], preferred_element_type=F32)
        up = jnp.dot(xb, wu_ref[:, c:c + FF_CHUNK], preferred_element_type=F32)
        h = (_silu(gt) * up).astype(BF16)
        d = jnp.dot(h, wd_ref[c:c + FF_CHUNK, :], preferred_element_type=F32)
        acc = d if acc is None else acc + d
    o_ref[...] = _layer_norm(ALPHA * x + acc, g_ref[...], b_ref[...])


def _ffn(x2, wg, wu, wd, g, b, tm):
    t = x2.shape[0]
    row = pl.BlockSpec((tm, D_MODEL), lambda i: (i, 0))
    consts = [wg, wu, wd, g, b]
    return pl.pallas_call(
        _ffn_kernel,
        grid=(t // tm,),
        in_specs=[row] + [_const_spec(c.shape) for c in consts],
        out_specs=row,
        out_shape=jax.ShapeDtypeStruct((t, D_MODEL), F32),
        compiler_params=_params("parallel"),
        name="ffn",
    )(x2, *consts)


def _moe_kernel(x_ref, wr_ref, br_ref, wg_ref, wu_ref, wd_ref, g_ref, b_ref, o_ref, gate_sc, acc_sc):
    e = pl.program_id(1)
    x = x_ref[...]
    xb = x.astype(BF16)
    lane = lax.broadcasted_iota(jnp.int32, (x.shape[0], LANES), 1)

    @pl.when(e == 0)
    def _():
        logits = jnp.dot(xb, wr_ref[...], preferred_element_type=F32) + br_ref[...]
        logits = jnp.where(lane < N_EXPERTS, logits, NEG)
        m1 = jnp.max(logits, axis=-1, keepdims=True)
        i1 = jnp.min(jnp.where(logits == m1, lane, LANES), axis=-1, keepdims=True)
        rest = jnp.where(lane == i1, NEG, logits)
        m2 = jnp.max(rest, axis=-1, keepdims=True)
        i2 = jnp.min(jnp.where(rest == m2, lane, LANES), axis=-1, keepdims=True)
        e2 = jnp.exp(m2 - m1)
        w1 = 1.0 / (1.0 + e2)
        w2 = e2 / (1.0 + e2)
        gate_sc[...] = jnp.where(lane == i1, w1, jnp.where(lane == i2, w2, 0.0))
        acc_sc[...] = jnp.zeros(acc_sc.shape, F32)

    gate_e = jnp.sum(jnp.where(lane == e, gate_sc[...], 0.0), axis=-1, keepdims=True)
    out_e = None
    for c in range(0, D_MODEL, FF_CHUNK):
        gt = jnp.dot(xb, wg_ref[0, :, c:c + FF_CHUNK], preferred_element_type=F32)
        up = jnp.dot(xb, wu_ref[0, :, c:c + FF_CHUNK], preferred_element_type=F32)
        h = (_silu(gt) * up).astype(BF16)
        d = jnp.dot(h, wd_ref[0, c:c + FF_CHUNK, :], preferred_element_type=F32)
        out_e = d if out_e is None else out_e + d
    acc_sc[...] += gate_e * out_e

    @pl.when(e == N_EXPERTS - 1)
    def _():
        o_ref[...] = _layer_norm(ALPHA * x + acc_sc[...], g_ref[...], b_ref[...])


def _moe(x2, wr, br, wg, wu, wd, g, b, tm):
    t = x2.shape[0]
    row = pl.BlockSpec((tm, D_MODEL), lambda i, e: (i, 0))
    wspec = pl.BlockSpec((1, D_MODEL, D_MODEL), lambda i, e: (e, 0, 0))
    return pl.pallas_call(
        _moe_kernel,
        grid=(t // tm, N_EXPERTS),
        in_specs=[row, _const_spec(wr.shape), _const_spec(br.shape), wspec, wspec, wspec,
                  _const_spec(g.shape), _const_spec(b.shape)],
        out_specs=row,
        out_shape=jax.ShapeDtypeStruct((t, D_MODEL), F32),
        scratch_shapes=[pltpu.VMEM((tm, LANES), F32), pltpu.VMEM((tm, D_MODEL), F32)],
        compiler_params=_params("parallel", "arbitrary"),
        name="moe",
    )(x2, wr, br, wg, wu, wd, g, b)


def _row(v, width=None):
    v = v.reshape(1, -1).astype(F32)
    if width is not None and v.shape[1] < width:
        v = jnp.pad(v, ((0, 0), (0, width - v.shape[1])))
    return v


def _layer_weights(p, l):
    w_in = p['w_in'][l]
    o = np.cumsum([N_BRANCH * D_MODEL, WIDTH, WIDTH, WIDTH, WIDTH, HEADS, WIDTH, CONV_DIM, HEADS])
    gates, u, q, k, v, fg, z, xbc, dtw = (w_in[:, a:b] for a, b in zip([0] + list(o[:-1]), o))
    w = {}
    w['w_main'] = jnp.concatenate([u, q, k, v, z, xbc], axis=1).astype(BF16)
    slab = ((0, 0), (0, 0), (0, LANES - HEAD_DIM))
    k_slab = jnp.pad(k.reshape(D_MODEL, HEADS, HEAD_DIM), slab).reshape(D_MODEL, HEADS * LANES)
    q_slab = jnp.pad(q.reshape(D_MODEL, HEADS, HEAD_DIM), slab).reshape(D_MODEL, HEADS * LANES)
    w['w_rows'] = jnp.concatenate([u, k_slab, z, xbc], axis=1).astype(BF16)
    w['w_cols'] = jnp.concatenate([q_slab, k, v], axis=1).T.astype(BF16)
    e3 = np.zeros((BIAS_LANES, LANES, HEADS * LANES), np.float32)
    for piece in range(BIAS_LANES):
        for h in range(HEADS):
            e3[piece, h, LANES * h + HEAD_DIM + piece] = 1.0
    w['bias_place'] = jnp.asarray(e3.reshape(BIAS_LANES * LANES, HEADS * LANES), BF16)
    w['w_small'] = jnp.pad(jnp.concatenate([fg, dtw], axis=1), ((0, 0), (0, LANES - 2 * HEADS))).astype(BF16)
    w['b_small'] = _row(jnp.concatenate([p['b_fgate'][l], p['ssd_dt_bias'][l]]), LANES)
    w['w_gates'] = gates.astype(BF16)
    w['b_gates'] = _row(p['b_gate'][l])
    for name in ('w_branch_s5', 'w_branch_fox', 'w_branch_ssd', 'w_o'):
        w[name] = p[name][l].astype(BF16)
    for name in ('ln1_g', 'ln1_b', 'ln2_g', 'ln2_b', 's5_d', 's5_b_glu', 'ssd_conv_b', 'ssd_norm_w'):
        w[name] = _row(p[name][l])
    w['s5_w_glu'] = p['s5_w_glu'][l].astype(BF16)
    w['ssd_conv_w'] = p['ssd_conv_w'][l].astype(F32)
    w['ssd_alog_row'] = _row(jnp.concatenate([jnp.zeros((HEADS,), F32), p['ssd_a_log'][l]]), LANES)
    w['ssd_alog_col'] = jnp.concatenate([jnp.zeros((HEADS,), F32), p['ssd_a_log'][l]]).reshape(2 * HEADS, 1)
    w['ssd_d'] = _row(jnp.repeat(p['ssd_d'][l], HEAD_DIM))
    e = np.zeros((LANES, WIDTH), np.float32)
    for h in range(HEADS):
        e[HEADS + h, HEAD_DIM * h:HEAD_DIM * (h + 1)] = 1.0
    w['ssd_expand'] = jnp.asarray(e, BF16)
    if l % 2 == 0:
        for name in ('ffn_w_gate', 'ffn_w_up', 'ffn_w_down'):
            w[name] = p[name][l // 2].astype(BF16)
    else:
        w['moe_w_router'] = jnp.pad(p['moe_w_router'][l // 2], ((0, 0), (0, LANES - N_EXPERTS))).astype(BF16)
        w['moe_b_router'] = _row(p['moe_b_router'][l // 2], LANES)
        for name in ('moe_w_gate', 'moe_w_up', 'moe_w_down'):
            w[name] = p[name][l // 2].astype(BF16)
    return w


def _s5_q(seq):
    return min(S5_CHUNK, seq)


def _run_trunk(x, p, weights, s5_tabs, s5_re0, s5_im0, conv0, ssd0, paged):
    bsz, seq, _ = x.shape
    t = bsz * seq
    tm_mix = min(512, t)
    fox_tile = min(512, seq)
    x2 = x.reshape(t, D_MODEL)
    new = [[] for _ in range(7)]
    q5 = _s5_q(seq)
    ssd0 = ssd0.reshape(ssd0.shape[0], bsz, WIDTH, SSD_STATE)
    if paged is not None:
        cache_k, cache_v, cache_lf, page_table = paged
        cache_kt = cache_k.transpose(0, 1, 3, 4, 2)
        cache_vt = cache_v.transpose(0, 1, 3, 4, 2)
        cache_lft = cache_lf.transpose(0, 1, 3, 2)
    for l in range(len(weights)):
        w = weights[l]
        if paged is None:
            u, ub, z, xbc, small, smt3, kaug, qaugt, kt, vt, tot = _in_proj_prompt(
                x2.reshape(bsz, seq, D_MODEL), w['w_rows'], w['w_cols'], w['w_small'], w['b_small'],
                w['bias_place'], fox_tile)
            k_out = kt.reshape(bsz, HEADS, HEAD_DIM, seq).transpose(0, 3, 1, 2)
            v_out = vt.reshape(bsz, HEADS, HEAD_DIM, seq).transpose(0, 3, 1, 2)
            logf_out = smt3[:, :HEADS, :].transpose(0, 2, 1)
        else:
            u, ub, qb, k, v, z, xbc, small, small_t, logf = _in_proj_sample(
                x2, w['w_main'], w['w_small'], w['b_small'], min(256, t))
            k_out = k.reshape(bsz, seq, HEADS, HEAD_DIM)
            v_out = v.reshape(bsz, seq, HEADS, HEAD_DIM)
            logf_out = logf.reshape(bsz, seq, HEADS)
            smt3 = small_t.reshape(2 * HEADS, bsz, seq).transpose(1, 0, 2)

        tab = s5_tabs[q5][l]
        yi, cs_re, cs_im = _s5_chunks(ub.reshape(t // q5, q5 * WIDTH), tab, q5, min(128, t // q5))
        d_row, w_glu, b_glu = w['s5_d'], w['s5_w_glu'], w['s5_b_glu']
        if paged is None:
            y_s5, s5_re, s5_im = _s5_scan(u.reshape(bsz, seq, WIDTH), yi, cs_re, cs_im,
                                          s5_re0[l].reshape(bsz, 1, S5_LANES), s5_im0[l].reshape(bsz, 1, S5_LANES),
                                          tab, d_row, w_glu, b_glu, q5, min(256, seq), True)
        else:
            y_s5, s5_re, s5_im = _s5_scan(u.reshape(1, t, WIDTH), yi, cs_re, cs_im,
                                          s5_re0[l].reshape(bsz, S5_LANES), s5_im0[l].reshape(bsz, S5_LANES),
                                          tab, d_row, w_glu, b_glu, q5, min(256, t), False)
        y_s5 = y_s5.reshape(t, WIDTH)
        s5_re = s5_re.reshape(bsz, S5_GROUPS, S5_STATE)
        s5_im = s5_im.reshape(bsz, S5_GROUPS, S5_STATE)

        if paged is None:
            y_fox = _fox_prompt(qaugt, kaug, vt, tot, fox_tile)
        else:
            lfn_t = jnp.pad(logf.reshape(bsz, seq, HEADS).transpose(0, 2, 1),
                            ((0, 0), (0, 0), (0, LANES - seq)))
            y_fox = _fox_sample(qb.astype(F32).reshape(bsz, seq, WIDTH), k.reshape(bsz, seq, WIDTH),
                                v.reshape(bsz, seq, WIDTH), lfn_t, cache_kt, cache_vt, cache_lft, l, page_table)
        y_fox = y_fox.reshape(t, WIDTH)

        nv = math.gcd(seq, SSD_CHUNK)
        q_ssd = max(nv, SSD_MIN_ROWS)
        sm3 = small.reshape(bsz, seq, LANES)
        if nv < q_ssd:
            sm3 = jnp.pad(sm3, ((0, 0), (0, q_ssd - nv), (0, 0)))
            smt3 = jnp.pad(smt3, ((0, 0), (0, 0), (0, q_ssd - nv)))
        y_ssd, conv_new, ssd_new = _ssd(
            z.reshape(bsz, seq, WIDTH), xbc.reshape(bsz, seq, CONV_DIM), sm3, smt3, conv0[l],
            ssd0, l, w['ssd_conv_w'], w['ssd_conv_b'], w['ssd_alog_row'],
            w['ssd_alog_col'], w['ssd_expand'], w['ssd_d'], w['ssd_norm_w'], nv, q_ssd)
        y_ssd = y_ssd.reshape(t, WIDTH)
        ssd_new = ssd_new.reshape(bsz, HEADS, HEAD_DIM, SSD_STATE)

        x2 = _merge(x2, y_s5, y_fox, y_ssd, w['w_gates'], w['b_gates'], w['w_branch_s5'],
                    w['w_branch_fox'], w['w_branch_ssd'], w['w_o'], w['ln1_g'], w['ln1_b'], tm_mix)
        if l % 2 == 0:
            x2 = _ffn(x2, w['ffn_w_gate'], w['ffn_w_up'], w['ffn_w_down'], w['ln2_g'], w['ln2_b'], tm_mix)
        else:
            x2 = _moe(x2, w['moe_w_router'], w['moe_b_router'], w['moe_w_gate'], w['moe_w_up'],
                      w['moe_w_down'], w['ln2_g'], w['ln2_b'], tm_mix)

        st = (k_out, v_out, logf_out, s5_re, s5_im, conv_new, ssd_new)
        for lst, s in zip(new, st):
            lst.append(s)
    return x2.reshape(bsz, seq, D_MODEL), [jnp.stack(s) for s in new]


def kernel(x_prompt, x_sample, cache_k, cache_v, cache_logf, page_table, state_s5_re, state_s5_im,
           state_conv, state_ssd, w_in, b_gate, b_fgate, s5_lam_re, s5_lam_im, s5_log_dt, s5_b_re,
           s5_b_im, s5_c_re, s5_c_im, s5_d, s5_w_glu, s5_b_glu, ssd_conv_w, ssd_conv_b, ssd_dt_bias,
           ssd_a_log, ssd_d, ssd_norm_w, w_branch_s5, w_branch_fox, w_branch_ssd, w_o, ln1_g, ln1_b,
           ln2_g, ln2_b, ffn_w_gate, ffn_w_up, ffn_w_down, moe_w_router, moe_b_router, moe_w_gate,
           moe_w_up, moe_w_down):
    p = dict(w_in=w_in, b_gate=b_gate, b_fgate=b_fgate, s5_d=s5_d, s5_w_glu=s5_w_glu, s5_b_glu=s5_b_glu,
             ssd_conv_w=ssd_conv_w, ssd_conv_b=ssd_conv_b, ssd_dt_bias=ssd_dt_bias, ssd_a_log=ssd_a_log,
             ssd_d=ssd_d, ssd_norm_w=ssd_norm_w, w_branch_s5=w_branch_s5, w_branch_fox=w_branch_fox,
             w_branch_ssd=w_branch_ssd, w_o=w_o, ln1_g=ln1_g, ln1_b=ln1_b, ln2_g=ln2_g, ln2_b=ln2_b,
             ffn_w_gate=ffn_w_gate, ffn_w_up=ffn_w_up, ffn_w_down=ffn_w_down, moe_w_router=moe_w_router,
             moe_b_router=moe_b_router, moe_w_gate=moe_w_gate, moe_w_up=moe_w_up, moe_w_down=moe_w_down)
    depth = w_in.shape[0]
    weights = [_layer_weights(p, l) for l in range(depth)]
    qs = {_s5_q(x_prompt.shape[1]), _s5_q(x_sample.shape[1])}
    s5_tabs = _all_s5_tables(s5_lam_re, s5_lam_im, s5_log_dt, s5_b_re, s5_b_im, s5_c_re, s5_c_im, qs)
    bsz = x_prompt.shape[0]
    z_s5 = jnp.zeros((depth, bsz, S5_GROUPS, S5_STATE), F32)
    z_conv = jnp.zeros((depth, bsz, SSD_CONV - 1, CONV_DIM), F32)
    z_ssd = jnp.zeros((depth, bsz, HEADS, HEAD_DIM, SSD_STATE), F32)
    y_p, st_p = _run_trunk(x_prompt, p, weights, s5_tabs, z_s5, z_s5, z_conv, z_ssd, None)
    y_s, st_s = _run_trunk(x_sample, p, weights, s5_tabs, state_s5_re, state_s5_im, state_conv, state_ssd,
                           (cache_k, cache_v, cache_logf, page_table))
    return (y_p, y_s, *st_p, *st_s)
```

```python
import functools
import math

import jax
import jax.numpy as jnp
import numpy as np
from jax import lax
from jax.experimental import pallas as pl
from jax.experimental.pallas import tpu as pltpu

F32 = jnp.float32
BF16 = jnp.bfloat16
HIGHEST = lax.Precision.HIGHEST

D_MODEL = 1024
N_BRANCH = 3
WIDTH = 512
S5_GROUPS = 32
S5_GROUP = 16
S5_STATE = 64
S5_LANES = S5_GROUPS * S5_STATE
S5_CHUNK = 16
HEADS = 8
HEAD_DIM = 64
SSD_STATE = 128
SSD_GROUPS = 2
SSD_CONV = 4
SSD_CHUNK = 128
SSD_MIN_ROWS = 16
SSD_SEQS_PER_STEP = 8
CONV_DIM = 1024
D_FF = 2816
N_EXPERTS = 8
PAGE = 128
ALPHA = 4.0 ** 0.25
LN_EPS = 1e-5
RMS_EPS = 1e-5
NEG = -1e30
LANES = 128
VMEM_LIMIT = 56 * 1024 * 1024

NT_DIMS = (((1,), (1,)), ((), ()))


def _params(*sem):
    return pltpu.CompilerParams(dimension_semantics=sem, vmem_limit_bytes=VMEM_LIMIT)


def _const_spec(shape):
    nd = len(shape)
    return pl.BlockSpec(shape, lambda *_: (0,) * nd, pipeline_mode=pl.Buffered(1))


def _layer_norm(x, g, b):
    mu = jnp.mean(x, axis=-1, keepdims=True)
    xc = x - mu
    var = jnp.mean(xc * xc, axis=-1, keepdims=True)
    return xc * lax.rsqrt(var + LN_EPS) * g + b


def _sigmoid(x):
    return 1.0 / (1.0 + jnp.exp(-x))


def _silu(x):
    return x * _sigmoid(x)


def _bf16_pieces(x):
    hi = x.astype(BF16)
    r1 = x - hi.astype(F32)
    mid = r1.astype(BF16)
    lo = (r1 - mid.astype(F32)).astype(BF16)
    return hi, mid, lo


def _small_act(xb, ws_ref, bs_ref):
    s = jnp.dot(xb, ws_ref[...], preferred_element_type=F32) + bs_ref[...]
    t = jnp.log1p(jnp.exp(-jnp.abs(s)))
    lane = lax.broadcasted_iota(jnp.int32, s.shape, 1)
    return jnp.where(lane < HEADS, jnp.minimum(s, 0.0) - t,
                     jnp.where(lane < 2 * HEADS, jnp.maximum(s, 0.0) + t, 0.0))


SEG_U, SEG_Q, SEG_K, SEG_V, SEG_Z, SEG_X, SEG_END = 0, 512, 1024, 1536, 2048, 2560, 3584


def _in_proj_sample_kernel(x_ref, wm_ref, ws_ref, bs_ref, u_ref, ub_ref, qb_ref, k_ref, v_ref,
                           z_ref, xbc_ref, small_ref, smallt_ref, logf_ref):
    xb = x_ref[...].astype(BF16)

    def seg(a, b):
        return jnp.dot(xb, wm_ref[:, a:b], preferred_element_type=F32)

    u = seg(SEG_U, SEG_Q)
    u_ref[...] = u
    ub_ref[...] = u.astype(BF16)
    qb_ref[...] = (seg(SEG_Q, SEG_K) * (HEAD_DIM ** -0.5)).astype(BF16)
    k_ref[...] = seg(SEG_K, SEG_V)
    v_ref[...] = seg(SEG_V, SEG_Z)
    z_ref[...] = seg(SEG_Z, SEG_X)
    xbc_ref[...] = seg(SEG_X, SEG_END)
    sm = _small_act(xb, ws_ref, bs_ref)
    small_ref[...] = sm
    smallt_ref[...] = sm.T[:2 * HEADS, :]
    logf_ref[...] = sm[:, :HEADS]


def _in_proj_sample(x2, wm, ws, bs, tm):
    t = x2.shape[0]
    row = lambda w: pl.BlockSpec((tm, w), lambda i: (i, 0))
    out_shape = (
        jax.ShapeDtypeStruct((t, WIDTH), F32),
        jax.ShapeDtypeStruct((t, WIDTH), BF16),
        jax.ShapeDtypeStruct((t, WIDTH), BF16),
        jax.ShapeDtypeStruct((t, WIDTH), F32),
        jax.ShapeDtypeStruct((t, WIDTH), F32),
        jax.ShapeDtypeStruct((t, WIDTH), F32),
        jax.ShapeDtypeStruct((t, CONV_DIM), F32),
        jax.ShapeDtypeStruct((t, LANES), F32),
        jax.ShapeDtypeStruct((2 * HEADS, t), F32),
        jax.ShapeDtypeStruct((t, HEADS), F32),
    )
    out_specs = (row(WIDTH), row(WIDTH), row(WIDTH), row(WIDTH), row(WIDTH),
                 row(WIDTH), row(CONV_DIM), row(LANES),
                 pl.BlockSpec((2 * HEADS, tm), lambda i: (0, i)), row(HEADS))
    return pl.pallas_call(
        _in_proj_sample_kernel,
        grid=(t // tm,),
        in_specs=[row(D_MODEL), _const_spec(wm.shape), _const_spec(ws.shape), _const_spec(bs.shape)],
        out_specs=out_specs,
        out_shape=out_shape,
        compiler_params=_params("parallel"),
        name="in_proj_sample",
    )(x2, wm, ws, bs)


PSEG_U, PSEG_K, PSEG_Z, PSEG_X, PSEG_END = 0, 512, 1536, 2048, 3072
TSEG_Q, TSEG_K, TSEG_V, TSEG_END = 0, 1024, 1536, 2048
BIAS_LANES = 3


def _in_proj_prompt_kernel(x_ref, wr_ref, wt_ref, ws_ref, bs_ref, e_ref, u_ref, ub_ref, z_ref, xbc_ref,
                           small_ref, smallt_ref, kaug_ref, qaugt_ref, kt_ref, vt_ref, tot_ref, u_sc):
    tm = x_ref.shape[0]
    xb = x_ref[...].astype(BF16)
    u = jnp.dot(xb, wr_ref[:, PSEG_U:PSEG_K], preferred_element_type=F32)
    u_ref[...] = u
    for k in range(WIDTH // LANES):
        u_sc[k] = u[:, LANES * k:LANES * (k + 1)]
    for s in range(S5_CHUNK):
        for k in range(WIDTH // LANES):
            lo_lane = s * WIDTH + LANES * k
            ub_ref[:, lo_lane:lo_lane + LANES] = u_sc[k, pl.ds(s, tm // S5_CHUNK, stride=S5_CHUNK), :].astype(BF16)
    z_ref[...] = jnp.dot(xb, wr_ref[:, PSEG_Z:PSEG_X], preferred_element_type=F32)
    xbc_ref[...] = jnp.dot(xb, wr_ref[:, PSEG_X:PSEG_END], preferred_element_type=F32)
    sm = _small_act(xb, ws_ref, bs_ref)
    small_ref[...] = sm
    smt = sm.T[:2 * HEADS, :]
    smallt_ref[0] = smt
    r_i = lax.broadcasted_iota(jnp.int32, (tm, tm), 0)
    c_i = lax.broadcasted_iota(jnp.int32, (tm, tm), 1)
    tri = (r_i >= c_i).astype(BF16)
    ones = jnp.ones((tm, tm), BF16)
    cloc = None
    tot = None
    for piece, piece_t in zip(_bf16_pieces(sm), _bf16_pieces(smt)):
        d = jnp.dot(tri, piece, preferred_element_type=F32)
        dt = jnp.dot(piece_t, ones, preferred_element_type=F32)
        cloc = d if cloc is None else cloc + d
        tot = dt if tot is None else tot + dt
    tot_ref[0, 0] = tot
    lane = lax.broadcasted_iota(jnp.int32, cloc.shape, 1)
    hi, mid, lo = _bf16_pieces(jnp.where(lane < HEADS, -cloc, 0.0))
    pieces = (hi.astype(F32) + pltpu.roll(mid.astype(F32), HEADS, axis=1)
              + pltpu.roll(lo.astype(F32), 2 * HEADS, axis=1)).astype(BF16)
    k_slabs = (jnp.dot(xb, wr_ref[:, PSEG_K:PSEG_Z], preferred_element_type=F32)
               + jnp.dot(pieces, e_ref[...], preferred_element_type=F32)).astype(BF16)
    for h in range(HEADS):
        kaug_ref[0, h] = k_slabs[:, LANES * h:LANES * (h + 1)]
    tall = lax.dot_general(wt_ref[...], xb, NT_DIMS, preferred_element_type=F32)
    row = lax.broadcasted_iota(jnp.int32, (LANES, tm), 0)
    ones_rows = (row >= HEAD_DIM) & (row < HEAD_DIM + BIAS_LANES)
    for h in range(HEADS):
        qh = tall[TSEG_Q + LANES * h:TSEG_Q + LANES * (h + 1), :]
        qaugt_ref[0, h] = jnp.where(ones_rows, 1.0, qh * (HEAD_DIM ** -0.5)).astype(BF16)
    kt_ref[0] = tall[TSEG_K:TSEG_V, :]
    vt_ref[0] = tall[TSEG_V:TSEG_END, :]


def _in_proj_prompt(x3, wr, wt, ws, bs, e3, tm):
    bsz, seq, _ = x3.shape
    nt = seq // tm
    x2 = x3.reshape(bsz * seq, D_MODEL)
    row = lambda w: pl.BlockSpec((tm, w), lambda b, i: (b * nt + i, 0))
    colt = lambda r: pl.BlockSpec((1, r, tm), lambda b, i: (b, 0, i))
    t = bsz * seq
    out_shape = (
        jax.ShapeDtypeStruct((t, WIDTH), F32),
        jax.ShapeDtypeStruct((t // S5_CHUNK, S5_CHUNK * WIDTH), BF16),
        jax.ShapeDtypeStruct((t, WIDTH), F32),
        jax.ShapeDtypeStruct((t, CONV_DIM), F32),
        jax.ShapeDtypeStruct((t, LANES), F32),
        jax.ShapeDtypeStruct((bsz, 2 * HEADS, seq), F32),
        jax.ShapeDtypeStruct((bsz, HEADS, seq, LANES), BF16),
        jax.ShapeDtypeStruct((bsz, HEADS, LANES, seq), BF16),
        jax.ShapeDtypeStruct((bsz, WIDTH, seq), F32),
        jax.ShapeDtypeStruct((bsz, WIDTH, seq), F32),
        jax.ShapeDtypeStruct((bsz, nt, 2 * HEADS, tm), F32),
    )
    out_specs = (row(WIDTH),
                 pl.BlockSpec((tm // S5_CHUNK, S5_CHUNK * WIDTH), lambda b, i: (b * nt + i, 0)),
                 row(WIDTH), row(CONV_DIM), row(LANES), colt(2 * HEADS),
                 pl.BlockSpec((1, HEADS, tm, LANES), lambda b, i: (b, 0, i, 0)),
                 pl.BlockSpec((1, HEADS, LANES, tm), lambda b, i: (b, 0, 0, i)),
                 colt(WIDTH), colt(WIDTH),
                 pl.BlockSpec((1, 1, 2 * HEADS, tm), lambda b, i: (b, i, 0, 0)))
    consts = [wr, wt, ws, bs, e3]
    return pl.pallas_call(
        _in_proj_prompt_kernel,
        grid=(bsz, nt),
        in_specs=[row(D_MODEL)] + [_const_spec(c.shape) for c in consts],
        out_specs=out_specs,
        out_shape=out_shape,
        scratch_shapes=[pltpu.VMEM((WIDTH // LANES, tm, LANES), F32)],
        compiler_params=_params("parallel", "parallel"),
        name="in_proj_prompt",
    )(x2, *consts)


def _cumsum_lanes(x):
    n = x.shape[-1]
    lane = lax.broadcasted_iota(jnp.int32, x.shape, x.ndim - 1)
    s = 1
    while s < n:
        x = x + jnp.where(lane >= s, pltpu.roll(x, s, axis=x.ndim - 1), 0.0)
        s *= 2
    return x


FOX_HEADS_PER_STEP = 8


def _fox_prompt_kernel(it_ref, jt_ref, q_ref, k_ref, v_ref, tot_ref, o_ref, m_sc, l_sc, c_sc, acc_sc, *, tq, tk):
    g = pl.program_id(1)
    t = pl.program_id(2)
    i = it_ref[t]
    j = jt_ref[t]
    hps = FOX_HEADS_PER_STEP

    @pl.when(j == 0)
    def _():
        m_sc[...] = jnp.full(m_sc.shape, NEG, F32)
        l_sc[...] = jnp.zeros(l_sc.shape, F32)
        c_sc[...] = jnp.zeros(c_sc.shape, F32)
        acc_sc[...] = jnp.zeros(acc_sc.shape, F32)

    def step(masked):
        if masked:
            kpos = j * tk + lax.broadcasted_iota(jnp.int32, (tk, tq), 0)
            qpos = i * tq + lax.broadcasted_iota(jnp.int32, (tk, tq), 1)
            causal = kpos <= qpos
        for h in range(hps):
            s = jnp.dot(k_ref[0, h], q_ref[0, h], preferred_element_type=F32)
            if masked:
                s = jnp.where(causal, s, NEG)
            c_j = c_sc[h]
            m_prev = m_sc[h]
            m_new = jnp.maximum(m_prev, jnp.max(s, axis=0, keepdims=True) - c_j)
            alpha = jnp.exp(m_prev - m_new)
            p = jnp.exp(s - (m_new + c_j))
            l_sc[h] = alpha * l_sc[h] + jnp.sum(p, axis=0, keepdims=True)
            rows = slice(HEAD_DIM * h, HEAD_DIM * (h + 1))
            acc_sc[rows, :] = alpha * acc_sc[rows, :] + jnp.dot(
                v_ref[0, rows, :].astype(BF16), p.astype(BF16), preferred_element_type=F32)
            m_sc[h] = m_new
            c_sc[h] = c_j + tot_ref[0, 0, pl.ds(g * hps + h, 1), :]

    @pl.when(j < i)
    def _():
        step(False)

    @pl.when(j == i)
    def _():
        step(True)
        inv = jnp.concatenate([jnp.broadcast_to(1.0 / l_sc[h], (HEAD_DIM, tq)) for h in range(hps)], axis=0)
        o_ref[0] = (acc_sc[...] * inv).T.astype(o_ref.dtype)


def _fox_prompt(qaugt, kaug, vt, tot, tq):
    bsz, _, _, seq = qaugt.shape
    nq = seq // tq
    hps = FOX_HEADS_PER_STEP
    it = np.array([i for i in range(nq) for j in range(i + 1)], np.int32)
    jt = np.array([j for i in range(nq) for j in range(i + 1)], np.int32)
    grid_spec = pltpu.PrefetchScalarGridSpec(
        num_scalar_prefetch=2,
        grid=(bsz, HEADS // hps, len(it)),
        in_specs=[
            pl.BlockSpec((1, hps, LANES, tq), lambda b, g, t, it, jt: (b, g, 0, it[t])),
            pl.BlockSpec((1, hps, tq, LANES), lambda b, g, t, it, jt: (b, g, jt[t], 0)),
            pl.BlockSpec((1, hps * HEAD_DIM, tq), lambda b, g, t, it, jt: (b, g, jt[t])),
            pl.BlockSpec((1, 1, 2 * HEADS, tq), lambda b, g, t, it, jt: (b, jt[t], 0, 0)),
        ],
        out_specs=pl.BlockSpec((1, tq, hps * HEAD_DIM), lambda b, g, t, it, jt: (b, it[t], g)),
        scratch_shapes=[pltpu.VMEM((hps, 1, tq), F32), pltpu.VMEM((hps, 1, tq), F32),
                        pltpu.VMEM((hps, 1, tq), F32), pltpu.VMEM((hps * HEAD_DIM, tq), F32)],
    )
    return pl.pallas_call(
        functools.partial(_fox_prompt_kernel, tq=tq, tk=tq),
        grid_spec=grid_spec,
        out_shape=jax.ShapeDtypeStruct((bsz, seq, WIDTH), BF16),
        compiler_params=_params("parallel", "parallel", "arbitrary"),
        name="fox_prompt",
    )(jnp.asarray(it), jnp.asarray(jt), qaugt, kaug, vt, tot)


FOX_SEQS_PER_STEP = 1


def _fox_sample_kernel(pt_ref, q_ref, kn_ref, vn_ref, lfn_ref, *refs, n_pages, tq, group):
    lf_ref = refs[2 * group * n_pages]
    o_ref = refs[2 * group * n_pages + 1]
    for s in range(group):
        one = pl.ds(s, 1)
        _fox_sample_sequence(pt_ref, pl.program_id(0) * group + s, q_ref.at[one], kn_ref.at[one],
                             vn_ref.at[one], lfn_ref.at[one], refs[s * n_pages:(s + 1) * n_pages],
                             refs[(group + s) * n_pages:(group + s + 1) * n_pages], lf_ref, o_ref.at[one],
                             n_pages=n_pages, tq=tq)


def _fox_sample_sequence(pt_ref, b, q_ref, kn_ref, vn_ref, lfn_ref, k_refs, v_refs, lf_ref, o_ref, *, n_pages, tq):
    rows = tq * HEADS
    q = q_ref[0].astype(F32)
    qe = jnp.concatenate([jnp.broadcast_to(q[t:t + 1, :], (HEADS, WIDTH)) for t in range(tq)],
                         axis=0)
    r_w = lax.broadcasted_iota(jnp.int32, (rows, WIDTH), 0)
    l_w = lax.broadcasted_iota(jnp.int32, (rows, WIDTH), 1)
    head_mask = (l_w // HEAD_DIM) == (r_w % HEADS)
    qbd = jnp.where(head_mask, qe, 0.0).astype(BF16)

    scores = [None] * n_pages
    carry = jnp.zeros((HEADS, 1), F32)
    for p in range(n_pages - 1, -1, -1):
        cs = _cumsum_lanes(lf_ref[0, pt_ref[b * n_pages + p]])
        tot = cs[:, PAGE - 1:PAGE]
        dec = (carry + tot) - cs
        carry = carry + tot
        kt = k_refs[p][0, 0].reshape(WIDTH, PAGE).astype(BF16)
        s = jnp.dot(qbd, kt, preferred_element_type=F32)
        scores[p] = s + jnp.concatenate([dec] * tq, axis=0)
    pad = jnp.zeros((PAGE - tq, WIDTH), F32)
    kn = jnp.concatenate([kn_ref[0], pad], axis=0).astype(BF16)
    vn = jnp.concatenate([vn_ref[0], pad], axis=0).astype(BF16)
    cn = _cumsum_lanes(lfn_ref[0])
    r_p = lax.broadcasted_iota(jnp.int32, (rows, PAGE), 0)
    c_p = lax.broadcasted_iota(jnp.int32, (rows, PAGE), 1)
    s_new = lax.dot_general(qbd, kn, NT_DIMS, preferred_element_type=F32)
    s_new = jnp.where(c_p <= r_p // HEADS, s_new - jnp.concatenate([cn] * tq, axis=0), NEG)

    m = s_new
    for p in range(n_pages):
        m = jnp.maximum(m, scores[p])
    m = jnp.max(m, axis=-1, keepdims=True)
    pn = jnp.exp(s_new - m)
    l = jnp.sum(pn, axis=-1, keepdims=True)
    o = jnp.dot(pn.astype(BF16), vn, preferred_element_type=F32)
    for p in range(n_pages):
        pp = jnp.exp(scores[p] - m)
        l = l + jnp.sum(pp, axis=-1, keepdims=True)
        vt = v_refs[p][0, 0].reshape(WIDTH, PAGE).astype(BF16)
        o = o + lax.dot_general(pp.astype(BF16), vt, NT_DIMS, preferred_element_type=F32)
    om = jnp.where(head_mask, o / l, 0.0)
    out = jnp.concatenate([jnp.sum(om[HEADS * t:HEADS * (t + 1), :], axis=0, keepdims=True) for t in range(tq)],
                          axis=0)
    o_ref[0] = out.astype(o_ref.dtype)


def _fox_sample(qb, k_new, v_new, lfn_t, cache_kt, cache_vt, cache_lft, layer, page_table):
    dbsz, tq, _ = qb.shape
    n_pages = page_table.shape[1]
    pt = page_table.reshape(-1).astype(jnp.int32)

    g = FOX_SEQS_PER_STEP if dbsz % FOX_SEQS_PER_STEP == 0 else 1

    def page_spec(s, p):
        return pl.BlockSpec((1, 1, HEADS, HEAD_DIM, PAGE),
                            lambda b, pt: (layer, pt[(b * g + s) * n_pages + p], 0, 0, 0))

    seq_spec = lambda shape: pl.BlockSpec(shape, lambda b, pt: (b, 0, 0))
    in_specs = [seq_spec((g, tq, WIDTH)), seq_spec((g, tq, WIDTH)), seq_spec((g, tq, WIDTH)),
                seq_spec((g, HEADS, LANES))]
    in_specs += [page_spec(s, p) for s in range(g) for p in range(n_pages)]
    in_specs += [page_spec(s, p) for s in range(g) for p in range(n_pages)]
    in_specs += [pl.BlockSpec((1,) + cache_lft.shape[1:], lambda b, pt: (layer, 0, 0, 0),
                              pipeline_mode=pl.Buffered(1))]
    grid_spec = pltpu.PrefetchScalarGridSpec(
        num_scalar_prefetch=1, grid=(dbsz // g,), in_specs=in_specs,
        out_specs=pl.BlockSpec((g, tq, WIDTH), lambda b, pt: (b, 0, 0)))
    return pl.pallas_call(
        functools.partial(_fox_sample_kernel, n_pages=n_pages, tq=tq, group=g),
        grid_spec=grid_spec,
        out_shape=jax.ShapeDtypeStruct((dbsz, tq, WIDTH), F32),
        compiler_params=_params("parallel"),
        name="fox_sample",
    )(pt, qb, k_new, v_new, lfn_t, *([cache_kt] * (g * n_pages)), *([cache_vt] * (g * n_pages)), cache_lft)


def _s5_tables(lam_re, lam_im, log_dt, b_re, b_im, c_re, c_im, q):
    hp = dict(precision=HIGHEST)
    dt = jnp.exp(log_dt)[:, None]
    mag = jnp.exp(lam_re * dt)
    ab_re = mag * jnp.cos(lam_im * dt)
    ab_im = mag * jnp.sin(lam_im * dt)
    nr = ab_re - 1.0
    den = lam_re * lam_re + lam_im * lam_im
    q_re = (nr * lam_re + ab_im * lam_im) / den
    q_im = (ab_im * lam_re - nr * lam_im) / den
    bb_re = q_re[..., None] * b_re - q_im[..., None] * b_im
    bb_im = q_re[..., None] * b_im + q_im[..., None] * b_re
    jj = jnp.arange(q + 1, dtype=F32)[:, None, None]
    pmag = jnp.exp(lam_re * dt * jj)
    p_re = pmag * jnp.cos(lam_im * dt * jj)
    p_im = pmag * jnp.sin(lam_im * dt * jj)
    ab_b_re = p_re[:q, :, :, None] * bb_re - p_im[:q, :, :, None] * bb_im
    ab_b_im = p_re[:q, :, :, None] * bb_im + p_im[:q, :, :, None] * bb_re
    kern = (jnp.einsum('gcn,jgnd->jgcd', c_re, ab_b_re, **hp)
            - jnp.einsum('gcn,jgnd->jgcd', c_im, ab_b_im, **hp))
    kt = kern.transpose(0, 1, 3, 2).reshape(q, 2, 16 * S5_GROUP, S5_GROUP)
    tile_cols = jnp.tile(jnp.eye(S5_GROUP, dtype=F32), (1, 16))
    r_g = lax.broadcasted_iota(jnp.int32, (256, 256), 0) // S5_GROUP
    c_g = lax.broadcasted_iota(jnp.int32, (256, 256), 1) // S5_GROUP
    w_toep = jnp.where(r_g == c_g, jnp.einsum('jxrb,bc->jxrc', kt, tile_cols, **hp), 0.0).astype(BF16)
    eye_g = jnp.eye(S5_GROUPS, dtype=F32)

    def in_map(bb):
        full = jnp.einsum('gnc,gh->gchn', bb, eye_g).reshape(WIDTH, S5_LANES)
        return jnp.stack([full[128 * (n // 2):128 * (n // 2) + 128, 256 * n:256 * n + 256]
                          for n in range(8)]).astype(BF16)

    def out_map(c):
        full = jnp.einsum('gcn,gh->gnhc', c, eye_g).reshape(S5_LANES, WIDTH)
        return jnp.stack([full[512 * m:512 * m + 512, 128 * m:128 * m + 128]
                          for m in range(4)]).astype(BF16)

    rows = ((q + 1 + 7) // 8) * 8
    padp = lambda p: jnp.pad(p.reshape(q + 1, S5_LANES), ((0, rows - q - 1), (0, 0)))
    return dict(w_toep=w_toep, wb_re=in_map(bb_re), wb_im=in_map(bb_im), wc_re=out_map(c_re),
                wc_im=out_map(-c_im), p_re=padp(p_re), p_im=padp(p_im))


def _all_s5_tables(lam_re, lam_im, log_dt, b_re, b_im, c_re, c_im, qs):
    f32 = lambda a: a.astype(F32)
    q_max = max(qs)
    full = jax.vmap(functools.partial(_s5_tables, q=q_max))(
        f32(lam_re), f32(lam_im), f32(log_dt), f32(b_re), f32(b_im), f32(c_re), f32(c_im))
    out = {}
    for q in qs:
        rows = ((q + 1 + 7) // 8) * 8
        out[q] = []
        for l in range(lam_re.shape[0]):
            tab = {name: arr[l] for name, arr in full.items()}
            tab['w_toep'] = tab['w_toep'][:q]
            tab['p_re'] = tab['p_re'][:rows]
            tab['p_im'] = tab['p_im'][:rows]
            out[q].append(tab)
    return out


def _s5_chunk_kernel(u_ref, w_ref, wbr_ref, wbi_ref, pr_ref, pi_ref, y_ref, sr_ref, si_ref, *, q):
    for s_out in range(q):
        for half in range(2):
            acc = None
            for j in range(s_out + 1):
                a = (s_out - j) * WIDTH + half * 256
                d = jnp.dot(u_ref[:, a:a + 256], w_ref[j, half], preferred_element_type=F32)
                acc = d if acc is None else acc + d
            o = s_out * WIDTH + half * 256
            y_ref[:, o:o + 256] = acc
    for n in range(8):
        lanes = slice(256 * n, 256 * (n + 1))
        acc_r = None
        acc_i = None
        for s in range(q):
            a = s * WIDTH + 128 * (n // 2)
            br = jnp.dot(u_ref[:, a:a + 128], wbr_ref[n], preferred_element_type=F32)
            bi = jnp.dot(u_ref[:, a:a + 128], wbi_ref[n], preferred_element_type=F32)
            pr = pr_ref[q - 1 - s:q - s, lanes]
            pi = pi_ref[q - 1 - s:q - s, lanes]
            tr = pr * br - pi * bi
            ti = pr * bi + pi * br
            acc_r = tr if acc_r is None else acc_r + tr
            acc_i = ti if acc_i is None else acc_i + ti
        sr_ref[:, lanes] = acc_r
        si_ref[:, lanes] = acc_i


def _s5_chunks(u2b, tab, q, rt):
    nc = u2b.shape[0]
    consts = [tab['w_toep'], tab['wb_re'], tab['wb_im'], tab['p_re'], tab['p_im']]
    state = pl.BlockSpec((rt, S5_LANES), lambda i: (i, 0))
    return pl.pallas_call(
        functools.partial(_s5_chunk_kernel, q=q),
        grid=(nc // rt,),
        in_specs=[pl.BlockSpec((rt, q * WIDTH), lambda i: (i, 0))] + [_const_spec(c.shape) for c in consts],
        out_specs=(pl.BlockSpec((rt, q * WIDTH), lambda i: (i, 0)), state, state),
        out_shape=(jax.ShapeDtypeStruct((nc, q * WIDTH), F32),
                   jax.ShapeDtypeStruct((nc, S5_LANES), F32), jax.ShapeDtypeStruct((nc, S5_LANES), F32)),
        compiler_params=_params("parallel"),
        name="s5_chunks",
    )(u2b, *consts)


def _s5_scan_kernel(u_ref, yi_ref, sre_ref, sim_ref, h0r_ref, h0i_ref, wcr_ref, wci_ref, pr_ref, pi_ref,
                    d_ref, wg_ref, bg_ref, y_ref, htr_ref, hti_ref,
                    xcr_sc, xci_sc, hr_sc, hi_sc, yi_sc, *, q, nct, carry):
    tile = pl.program_id(1)
    u = u_ref[0]
    aq_r = pr_ref[q:q + 1, :]
    aq_i = pi_ref[q:q + 1, :]
    p1r = pr_ref[1:q + 1, :]
    p1i = pi_ref[1:q + 1, :]

    if carry:
        @pl.when(tile == 0)
        def _():
            hr_sc[...] = h0r_ref[0]
            hi_sc[...] = h0i_ref[0]

        def body(c, h):
            hr, hi = h
            row0 = pl.multiple_of(c * q, q)
            xcr_sc[pl.ds(row0, q), :] = p1r * hr - p1i * hi
            xci_sc[pl.ds(row0, q), :] = p1r * hi + p1i * hr
            sr = sre_ref[pl.ds(c, 1), :]
            si = sim_ref[pl.ds(c, 1), :]
            return aq_r * hr - aq_i * hi + sr, aq_r * hi + aq_i * hr + si

        hr, hi = lax.fori_loop(0, nct, body, (hr_sc[...], hi_sc[...]))
        hr_sc[...] = hr
        hi_sc[...] = hi

        @pl.when(tile == pl.num_programs(1) - 1)
        def _():
            htr_ref[0] = hr
            hti_ref[0] = hi
    else:
        def body(c, _):
            hr = h0r_ref[pl.ds(c, 1), :]
            hi = h0i_ref[pl.ds(c, 1), :]
            row0 = pl.multiple_of(c * q, q)
            xcr_sc[pl.ds(row0, q), :] = p1r * hr - p1i * hi
            xci_sc[pl.ds(row0, q), :] = p1r * hi + p1i * hr
            return 0

        lax.fori_loop(0, nct, body, 0)
        h0r = h0r_ref[...]
        h0i = h0i_ref[...]
        htr_ref[...] = aq_r * h0r - aq_i * h0i + sre_ref[...]
        hti_ref[...] = aq_r * h0i + aq_i * h0r + sim_ref[...]

    cols = []
    for m in range(4):
        xr = xcr_sc[:, 512 * m:512 * m + 512].astype(BF16)
        xi = xci_sc[:, 512 * m:512 * m + 512].astype(BF16)
        cols.append(jnp.dot(xr, wcr_ref[m], preferred_element_type=F32)
                    + jnp.dot(xi, wci_ref[m], preferred_element_type=F32))
    for s in range(q):
        for k in range(WIDTH // LANES):
            lo_lane = s * WIDTH + LANES * k
            yi_sc[k, pl.ds(s, nct, stride=q), :] = yi_ref[:, lo_lane:lo_lane + LANES]
    yi = jnp.concatenate([yi_sc[k] for k in range(WIDTH // LANES)], axis=-1)
    y = yi + jnp.concatenate(cols, axis=-1) + d_ref[...] * u
    y = jax.nn.gelu(y)
    g = jnp.dot(y.astype(BF16), wg_ref[...], preferred_element_type=F32) + bg_ref[...]
    y_ref[0] = (y * _sigmoid(g)).astype(y_ref.dtype)


def _s5_scan(u3, yi2, s_re, s_im, h0_re, h0_im, tab, d_row, w_glu, b_glu, q, tm, carry):
    bsz, seq, _ = u3.shape
    nct = tm // q
    n_tiles = seq // tm
    tok = pl.BlockSpec((1, tm, WIDTH), lambda b, t: (b, t, 0))
    chunk = pl.BlockSpec((nct, S5_LANES), lambda b, t: (b * n_tiles + t, 0))
    chunk_y = pl.BlockSpec((nct, q * WIDTH), lambda b, t: (b * n_tiles + t, 0))
    if carry:
        h_spec = pl.BlockSpec((1, 1, S5_LANES), lambda b, t: (b, 0, 0))
        h_shape = jax.ShapeDtypeStruct((bsz, 1, S5_LANES), F32)
    else:
        h_spec = pl.BlockSpec((nct, S5_LANES), lambda b, t: (t, 0))
        h_shape = jax.ShapeDtypeStruct((seq // q, S5_LANES), F32)
    consts = [tab['wc_re'], tab['wc_im'], tab['p_re'], tab['p_im'], d_row, w_glu, b_glu]
    return pl.pallas_call(
        functools.partial(_s5_scan_kernel, q=q, nct=nct, carry=carry),
        grid=(bsz, n_tiles),
        in_specs=[tok, chunk_y, chunk, chunk, h_spec, h_spec] + [_const_spec(c.shape) for c in consts],
        out_specs=(tok, h_spec, h_spec),
        out_shape=(jax.ShapeDtypeStruct((bsz, seq, WIDTH), BF16), h_shape, h_shape),
        scratch_shapes=[pltpu.VMEM((tm, S5_LANES), F32), pltpu.VMEM((tm, S5_LANES), F32),
                        pltpu.VMEM((1, S5_LANES), F32), pltpu.VMEM((1, S5_LANES), F32),
                        pltpu.VMEM((WIDTH // LANES, tm, LANES), F32)],
        compiler_params=_params("parallel", "arbitrary"),
        name="s5_scan",
    )(u3, yi2, s_re, s_im, h0_re, h0_im, *consts)


def _ssd_kernel(z_ref, xbc_ref, sm_ref, smt_ref, conv0_ref, h0_ref, cw_ref, cb_ref, alog_row_ref,
                alog_col_ref, e_ref, d_ref, nw_ref, y_ref, convt_ref, ht_ref,
                xp_sc, st_sc, zp_sc, *, nv, q, group):
    for s in range(group):
        one = pl.ds(s, 1)
        _ssd_sequence(z_ref.at[one], xbc_ref.at[one], sm_ref.at[one], smt_ref.at[one], conv0_ref.at[one],
                      h0_ref.at[:, one], cw_ref, cb_ref, alog_row_ref, alog_col_ref, e_ref, d_ref, nw_ref,
                      y_ref.at[one], convt_ref.at[one], ht_ref.at[one],
                      xp_sc.at[s], st_sc.at[s], zp_sc.at[s], nv=nv, q=q)


def _ssd_sequence(z_ref, xbc_ref, sm_ref, smt_ref, conv0_ref, h0_ref, cw_ref, cb_ref, alog_row_ref,
                  alog_col_ref, e_ref, d_ref, nw_ref, y_ref, convt_ref, ht_ref,
                  xp_sc, st_sc, zp_sc, *, nv, q):
    c = pl.program_id(1)
    last = c == pl.num_programs(1) - 1

    @pl.when(c == 0)
    def _():
        st_sc[...] = h0_ref[0, 0]
        xp_sc[...] = jnp.zeros(xp_sc.shape, F32)
        xp_sc[8 - (SSD_CONV - 1):8, :] = conv0_ref[0]
        if nv < q:
            zp_sc[...] = jnp.zeros(zp_sc.shape, F32)

    xp_sc[8:8 + nv, :] = xbc_ref[0]
    acc = None
    for j in range(SSD_CONV):
        o = 8 - (SSD_CONV - 1) + j
        term = xp_sc[o:o + q, :] * cw_ref[j:j + 1, :]
        acc = term if acc is None else acc + term
    xc = _silu(acc + cb_ref[...])
    tail = xp_sc[8 + nv - (SSD_CONV - 1):8 + nv, :]

    @pl.when(last)
    def _():
        convt_ref[0] = tail

    xp_sc[8 - (SSD_CONV - 1):8, :] = tail
    if nv < q:
        zp_sc[0:nv, :] = z_ref[0]
        z = zp_sc[...]
    else:
        z = z_ref[0]

    xs = xc[:, :WIDTH]
    bm = xc[:, WIDTH:WIDTH + SSD_GROUPS * SSD_STATE]
    cm = xc[:, WIDTH + SSD_GROUPS * SSD_STATE:]

    sm = sm_ref[0]
    smt = smt_ref[0]
    lane = lax.broadcasted_iota(jnp.int32, (1, LANES), 1)
    a_row = jnp.where((lane >= HEADS) & (lane < 2 * HEADS), -jnp.exp(alog_row_ref[...]), 0.0)
    rowi = lax.broadcasted_iota(jnp.int32, (2 * HEADS, 1), 0)
    a_col = jnp.where(rowi >= HEADS, -jnp.exp(alog_col_ref[...]), 0.0)
    adt = sm * a_row
    adt_t = smt * a_col
    r_i = lax.broadcasted_iota(jnp.int32, (q, q), 0)
    c_i = lax.broadcasted_iota(jnp.int32, (q, q), 1)
    tri = r_i >= c_i
    tri_lo = tri.astype(BF16)
    tri_up = (r_i <= c_i).astype(BF16)
    acum = None
    acum_t = None
    for piece, piece_t in zip(_bf16_pieces(adt), _bf16_pieces(adt_t)):
        d = jnp.dot(tri_lo, piece, preferred_element_type=F32)
        dt = jnp.dot(piece_t, tri_up, preferred_element_type=F32)
        acum = d if acum is None else acum + d
        acum_t = dt if acum_t is None else acum_t + dt
    atot = acum[q - 1:q, :]
    stacked = jnp.concatenate([sm, jnp.exp(acum), jnp.exp(atot - acum)], axis=0)
    hi, mid, lo = _bf16_pieces(stacked)
    e = e_ref[...]
    expanded = (jnp.dot(hi, e, preferred_element_type=F32) + jnp.dot(mid, e, preferred_element_type=F32)
                + jnp.dot(lo, e, preferred_element_type=F32))
    dt_e = expanded[0:q]
    eac_e = expanded[q:2 * q]
    dec_e = expanded[2 * q:3 * q]
    xd = xs * dt_e
    xdd = xd * dec_e
    lane_half = lax.broadcasted_iota(jnp.int32, (q, LANES), 1) // HEAD_DIM

    y_cols = []
    for g in range(SSD_GROUPS):
        bg = bm[:, SSD_STATE * g:SSD_STATE * (g + 1)].astype(BF16)
        cg = cm[:, SSD_STATE * g:SSD_STATE * (g + 1)].astype(BF16)
        cb = lax.dot_general(cg, bg, NT_DIMS, preferred_element_type=F32)
        hpg = HEADS // SSD_GROUPS
        st_g = st_sc[hpg * HEAD_DIM * g:hpg * HEAD_DIM * (g + 1), :]
        y_off = lax.dot_general(cg, st_g.astype(BF16), NT_DIMS, preferred_element_type=F32)
        for pair in range(hpg // 2):
            ys = []
            for hh in range(2):
                h = hpg * g + 2 * pair + hh
                diff = acum[:, HEADS + h:HEADS + h + 1] - acum_t[HEADS + h:HEADS + h + 1, :]
                lmat = jnp.exp(jnp.where(tri, diff, NEG))
                w = (cb * lmat).astype(BF16)
                lo = LANES * (2 * g + pair)
                ys.append(jnp.dot(w, xd[:, lo:lo + LANES].astype(BF16), preferred_element_type=F32))
            y_cols.append(jnp.where(lane_half == 0, ys[0], ys[1]))
        y_cols[-2] = y_cols[-2] + y_off[:, :LANES] * eac_e[:, 256 * g:256 * g + LANES]
        y_cols[-1] = y_cols[-1] + y_off[:, LANES:] * eac_e[:, 256 * g + LANES:256 * (g + 1)]
        contrib = jnp.dot(xdd[:, 256 * g:256 * (g + 1)].T.astype(BF16), bg, preferred_element_type=F32)
        for hl in range(hpg):
            h = hpg * g + hl
            sl = slice(HEAD_DIM * h, HEAD_DIM * (h + 1))
            dec_h = jnp.exp(acum_t[HEADS + h:HEADS + h + 1, q - 1:q])
            st_sc[sl, :] = st_sc[sl, :] * dec_h + contrib[HEAD_DIM * hl:HEAD_DIM * (hl + 1), :]

    y = jnp.concatenate(y_cols, axis=-1) + d_ref[...] * xs
    y = y * _silu(z)
    y = y * lax.rsqrt(jnp.mean(y * y, axis=-1, keepdims=True) + RMS_EPS) * nw_ref[...]
    y_ref[0] = y[:nv].astype(y_ref.dtype)

    @pl.when(last)
    def _():
        ht_ref[0] = st_sc[...]


def _ssd(z3, xbc3, sm3, smt3, conv0, h0, layer, cw, cb, alog_row, alog_col, e_mat, d_row, nw, nv, q):
    bsz, seq, _ = z3.shape
    nchunk = seq // nv
    g = SSD_SEQS_PER_STEP if (nchunk == 1 and bsz % SSD_SEQS_PER_STEP == 0) else 1
    consts = [cw, cb, alog_row, alog_col, e_mat, d_row, nw]
    return pl.pallas_call(
        functools.partial(_ssd_kernel, nv=nv, q=q, group=g),
        grid=(bsz // g, nchunk),
        in_specs=[pl.BlockSpec((g, nv, WIDTH), lambda b, c: (b, c, 0)),
                  pl.BlockSpec((g, nv, CONV_DIM), lambda b, c: (b, c, 0)),
                  pl.BlockSpec((g, q, LANES), lambda b, c: (b, c, 0)),
                  pl.BlockSpec((g, 2 * HEADS, q), lambda b, c: (b, 0, c)),
                  pl.BlockSpec((g, SSD_CONV - 1, CONV_DIM), lambda b, c: (b, 0, 0)),
                  pl.BlockSpec((1, g, WIDTH, SSD_STATE), lambda b, c: (layer, b, 0, 0))]
                 + [_const_spec(x.shape) for x in consts],
        out_specs=(pl.BlockSpec((g, nv, WIDTH), lambda b, c: (b, c, 0)),
                   pl.BlockSpec((g, SSD_CONV - 1, CONV_DIM), lambda b, c: (b, 0, 0)),
                   pl.BlockSpec((g, WIDTH, SSD_STATE), lambda b, c: (b, 0, 0))),
        out_shape=(jax.ShapeDtypeStruct((bsz, seq, WIDTH), BF16 if nv % 16 == 0 else F32),
                   jax.ShapeDtypeStruct((bsz, SSD_CONV - 1, CONV_DIM), F32),
                   jax.ShapeDtypeStruct((bsz, WIDTH, SSD_STATE), F32)),
        scratch_shapes=[pltpu.VMEM((g, 8 + q, CONV_DIM), F32), pltpu.VMEM((g, WIDTH, SSD_STATE), F32),
                        pltpu.VMEM((g, q, WIDTH), F32)],
        compiler_params=_params("parallel", "arbitrary"),
        name="ssd",
    )(z3, xbc3, sm3, smt3, conv0, h0, *consts)


def _merge_kernel(x_ref, ys5_ref, yfox_ref, yssd_ref, wg_ref, bg_ref, ws5_ref, wfox_ref, wssd_ref,
                  wo_ref, g_ref, b_ref, o_ref):
    x = x_ref[...]
    xb = x.astype(BF16)
    merged = None
    for br, (y_ref, w_ref) in enumerate(((ys5_ref, ws5_ref), (yfox_ref, wfox_ref), (yssd_ref, wssd_ref))):
        lo = br * D_MODEL
        gate = _sigmoid(jnp.dot(xb, wg_ref[:, lo:lo + D_MODEL], preferred_element_type=F32)
                        + bg_ref[:, lo:lo + D_MODEL])
        term = gate * jnp.dot(y_ref[...].astype(BF16), w_ref[...], preferred_element_type=F32)
        merged = term if merged is None else merged + term
    out = jnp.dot(merged.astype(BF16), wo_ref[...], preferred_element_type=F32)
    o_ref[...] = _layer_norm(ALPHA * x + out, g_ref[...], b_ref[...])


def _merge(x2, ys5, yfox, yssd, wg, bg, ws5, wfox, wssd, wo, g, b, tm):
    t = x2.shape[0]
    row = lambda w: pl.BlockSpec((tm, w), lambda i: (i, 0))
    consts = [wg, bg, ws5, wfox, wssd, wo, g, b]
    return pl.pallas_call(
        _merge_kernel,
        grid=(t // tm,),
        in_specs=[row(D_MODEL), row(WIDTH), row(WIDTH), row(WIDTH)] + [_const_spec(c.shape) for c in consts],
        out_specs=row(D_MODEL),
        out_shape=jax.ShapeDtypeStruct((t, D_MODEL), F32),
        compiler_params=_params("parallel"),
        name="merge",
    )(x2, ys5, yfox, yssd, *consts)


FF_CHUNK = 256


def _ffn_kernel(x_ref, wg_ref, wu_ref, wd_ref, g_ref, b_ref, o_ref):
    x = x_ref[...]
    xb = x.astype(BF16)
    acc = None
    for c in range(0, D_FF, FF_CHUNK):
        gt = jnp.dot(xb, wg_ref[:, c:c + FF_CHUNK], preferred_element_type=F32)
        up = jnp.dot(xb, wu_ref[:, c:c + FF_CHUNK], preferred_element_type=F32)
        h = (_silu(gt) * up).astype(BF16)
        d = jnp.dot(h, wd_ref[c:c + FF_CHUNK, :], preferred_element_type=F32)
        acc = d if acc is None else acc + d
    o_ref[...] = _layer_norm(ALPHA * x + acc, g_ref[...], b_ref[...])


def _ffn(x2, wg, wu, wd, g, b, tm):
    t = x2.shape[0]
    row = pl.BlockSpec((tm, D_MODEL), lambda i: (i, 0))
    consts = [wg, wu, wd, g, b]
    return pl.pallas_call(
        _ffn_kernel,
        grid=(t // tm,),
        in_specs=[row] + [_const_spec(c.shape) for c in consts],
        out_specs=row,
        out_shape=jax.ShapeDtypeStruct((t, D_MODEL), F32),
        compiler_params=_params("parallel"),
        name="ffn",
    )(x2, *consts)


MOE_TOKEN_TILE = 1024


def _moe_kernel(x_ref, wr_ref, br_ref, wg_ref, wu_ref, wd_ref, g_ref, b_ref, o_ref, gate_sc, acc_sc):
    e = pl.program_id(1)
    x = x_ref[...]
    xb = x.astype(BF16)
    lane = lax.broadcasted_iota(jnp.int32, (x.shape[0], LANES), 1)

    @pl.when(e == 0)
    def _():
        logits = jnp.dot(xb, wr_ref[...], preferred_element_type=F32) + br_ref[...]
        logits = jnp.where(lane < N_EXPERTS, logits, NEG)
        m1 = jnp.max(logits, axis=-1, keepdims=True)
        i1 = jnp.min(jnp.where(logits == m1, lane, LANES), axis=-1, keepdims=True)
        rest = jnp.where(lane == i1, NEG, logits)
        m2 = jnp.max(rest, axis=-1, keepdims=True)
        i2 = jnp.min(jnp.where(rest == m2, lane, LANES), axis=-1, keepdims=True)
        e2 = jnp.exp(m2 - m1)
        w1 = 1.0 / (1.0 + e2)
        w2 = e2 / (1.0 + e2)
        gate_sc[...] = jnp.where(lane == i1, w1, jnp.where(lane == i2, w2, 0.0))
        acc_sc[...] = jnp.zeros(acc_sc.shape, F32)

    gate_e = jnp.sum(jnp.where(lane == e, gate_sc[...], 0.0), axis=-1, keepdims=True)
    out_e = None
    for c in range(0, D_MODEL, FF_CHUNK):
        gt = jnp.dot(xb, wg_ref[0, :, c:c + FF_CHUNK], preferred_element_type=F32)
        up = jnp.dot(xb, wu_ref[0, :, c:c + FF_CHUNK], preferred_element_type=F32)
        h = (_silu(gt) * up).astype(BF16)
        d = jnp.dot(h, wd_ref[0, c:c + FF_CHUNK, :], preferred_element_type=F32)
        out_e = d if out_e is None else out_e + d
    acc_sc[...] += gate_e * out_e

    @pl.when(e == N_EXPERTS - 1)
    def _():
        o_ref[...] = _layer_norm(ALPHA * x + acc_sc[...], g_ref[...], b_ref[...])


def _moe(x2, wr, br, wg, wu, wd, g, b, tm):
    t = x2.shape[0]
    row = pl.BlockSpec((tm, D_MODEL), lambda i, e: (i, 0))
    wspec = pl.BlockSpec((1, D_MODEL, D_MODEL), lambda i, e: (e, 0, 0))
    return pl.pallas_call(
        _moe_kernel,
        grid=(t // tm, N_EXPERTS),
        in_specs=[row, _const_spec(wr.shape), _const_spec(br.shape), wspec, wspec, wspec,
                  _const_spec(g.shape), _const_spec(b.shape)],
        out_specs=row,
        out_shape=jax.ShapeDtypeStruct((t, D_MODEL), F32),
        scratch_shapes=[pltpu.VMEM((tm, LANES), F32), pltpu.VMEM((tm, D_MODEL), F32)],
        compiler_params=_params("parallel", "arbitrary"),
        name="moe",
    )(x2, wr, br, wg, wu, wd, g, b)


def _row(v, width=None):
    v = v.reshape(1, -1).astype(F32)
    if width is not None and v.shape[1] < width:
        v = jnp.pad(v, ((0, 0), (0, width - v.shape[1])))
    return v


def _layer_weights(p, l):
    w_in = p['w_in'][l]
    o = np.cumsum([N_BRANCH * D_MODEL, WIDTH, WIDTH, WIDTH, WIDTH, HEADS, WIDTH, CONV_DIM, HEADS])
    gates, u, q, k, v, fg, z, xbc, dtw = (w_in[:, a:b] for a, b in zip([0] + list(o[:-1]), o))
    w = {}
    w['w_main'] = jnp.concatenate([u, q, k, v, z, xbc], axis=1).astype(BF16)
    slab = ((0, 0), (0, 0), (0, LANES - HEAD_DIM))
    k_slab = jnp.pad(k.reshape(D_MODEL, HEADS, HEAD_DIM), slab).reshape(D_MODEL, HEADS * LANES)
    q_slab = jnp.pad(q.reshape(D_MODEL, HEADS, HEAD_DIM), slab).reshape(D_MODEL, HEADS * LANES)
    w['w_rows'] = jnp.concatenate([u, k_slab, z, xbc], axis=1).astype(BF16)
    w['w_cols'] = jnp.concatenate([q_slab, k, v], axis=1).T.astype(BF16)
    e3 = np.zeros((LANES, HEADS * LANES), np.float32)
    for piece in range(BIAS_LANES):
        for h in range(HEADS):
            e3[HEADS * piece + h, LANES * h + HEAD_DIM + piece] = 1.0
    w['bias_place'] = jnp.asarray(e3, BF16)
    w['w_small'] = jnp.pad(jnp.concatenate([fg, dtw], axis=1), ((0, 0), (0, LANES - 2 * HEADS))).astype(BF16)
    w['b_small'] = _row(jnp.concatenate([p['b_fgate'][l], p['ssd_dt_bias'][l]]), LANES)
    w['w_gates'] = gates.astype(BF16)
    w['b_gates'] = _row(p['b_gate'][l])
    for name in ('w_branch_s5', 'w_branch_fox', 'w_branch_ssd', 'w_o'):
        w[name] = p[name][l].astype(BF16)
    for name in ('ln1_g', 'ln1_b', 'ln2_g', 'ln2_b', 's5_d', 's5_b_glu', 'ssd_conv_b', 'ssd_norm_w'):
        w[name] = _row(p[name][l])
    w['s5_w_glu'] = p['s5_w_glu'][l].astype(BF16)
    w['ssd_conv_w'] = p['ssd_conv_w'][l].astype(F32)
    w['ssd_alog_row'] = _row(jnp.concatenate([jnp.zeros((HEADS,), F32), p['ssd_a_log'][l]]), LANES)
    w['ssd_alog_col'] = jnp.concatenate([jnp.zeros((HEADS,), F32), p['ssd_a_log'][l]]).reshape(2 * HEADS, 1)
    w['ssd_d'] = _row(jnp.repeat(p['ssd_d'][l], HEAD_DIM))
    e = np.zeros((LANES, WIDTH), np.float32)
    for h in range(HEADS):
        e[HEADS + h, HEAD_DIM * h:HEAD_DIM * (h + 1)] = 1.0
    w['ssd_expand'] = jnp.asarray(e, BF16)
    if l % 2 == 0:
        for name in ('ffn_w_gate', 'ffn_w_up', 'ffn_w_down'):
            w[name] = p[name][l // 2].astype(BF16)
    else:
        w['moe_w_router'] = jnp.pad(p['moe_w_router'][l // 2], ((0, 0), (0, LANES - N_EXPERTS))).astype(BF16)
        w['moe_b_router'] = _row(p['moe_b_router'][l // 2], LANES)
        for name in ('moe_w_gate', 'moe_w_up', 'moe_w_down'):
            w[name] = p[name][l // 2].astype(BF16)
    return w


def _s5_q(seq):
    return min(S5_CHUNK, seq)


def _run_trunk(x, p, weights, s5_tabs, s5_re0, s5_im0, conv0, ssd0, paged):
    bsz, seq, _ = x.shape
    t = bsz * seq
    tm_mix = min(512, t)
    fox_tile = min(512, seq)
    x2 = x.reshape(t, D_MODEL)
    new = [[] for _ in range(7)]
    q5 = _s5_q(seq)
    ssd0 = ssd0.reshape(ssd0.shape[0], bsz, WIDTH, SSD_STATE)
    if paged is not None:
        cache_k, cache_v, cache_lf, page_table = paged
        cache_kt = cache_k.transpose(0, 1, 3, 4, 2)
        cache_vt = cache_v.transpose(0, 1, 3, 4, 2)
        cache_lft = cache_lf.transpose(0, 1, 3, 2)
    for l in range(len(weights)):
        w = weights[l]
        if paged is None:
            u, ub, z, xbc, small, smt3, kaug, qaugt, kt, vt, tot = _in_proj_prompt(
                x2.reshape(bsz, seq, D_MODEL), w['w_rows'], w['w_cols'], w['w_small'], w['b_small'],
                w['bias_place'], fox_tile)
            k_out = kt.reshape(bsz, HEADS, HEAD_DIM, seq).transpose(0, 3, 1, 2)
            v_out = vt.reshape(bsz, HEADS, HEAD_DIM, seq).transpose(0, 3, 1, 2)
            logf_out = smt3[:, :HEADS, :].transpose(0, 2, 1)
        else:
            u, ub, qb, k, v, z, xbc, small, small_t, logf = _in_proj_sample(
                x2, w['w_main'], w['w_small'], w['b_small'], min(256, t))
            k_out = k.reshape(bsz, seq, HEADS, HEAD_DIM)
            v_out = v.reshape(bsz, seq, HEADS, HEAD_DIM)
            logf_out = logf.reshape(bsz, seq, HEADS)
            smt3 = small_t.reshape(2 * HEADS, bsz, seq).transpose(1, 0, 2)

        tab = s5_tabs[q5][l]
        yi, cs_re, cs_im = _s5_chunks(ub.reshape(t // q5, q5 * WIDTH), tab, q5, min(128, t // q5))
        d_row, w_glu, b_glu = w['s5_d'], w['s5_w_glu'], w['s5_b_glu']
        if paged is None:
            y_s5, s5_re, s5_im = _s5_scan(u.reshape(bsz, seq, WIDTH), yi, cs_re, cs_im,
                                          s5_re0[l].reshape(bsz, 1, S5_LANES), s5_im0[l].reshape(bsz, 1, S5_LANES),
                                          tab, d_row, w_glu, b_glu, q5, min(256, seq), True)
        else:
            y_s5, s5_re, s5_im = _s5_scan(u.reshape(1, t, WIDTH), yi, cs_re, cs_im,
                                          s5_re0[l].reshape(bsz, S5_LANES), s5_im0[l].reshape(bsz, S5_LANES),
                                          tab, d_row, w_glu, b_glu, q5, min(256, t), False)
        y_s5 = y_s5.reshape(t, WIDTH)
        s5_re = s5_re.reshape(bsz, S5_GROUPS, S5_STATE)
        s5_im = s5_im.reshape(bsz, S5_GROUPS, S5_STATE)

        if paged is None:
            y_fox = _fox_prompt(qaugt, kaug, vt, tot, fox_tile)
        else:
            lfn_t = jnp.pad(logf.reshape(bsz, seq, HEADS).transpose(0, 2, 1),
                            ((0, 0), (0, 0), (0, LANES - seq)))
            y_fox = _fox_sample(qb.astype(F32).reshape(bsz, seq, WIDTH), k.reshape(bsz, seq, WIDTH),
                                v.reshape(bsz, seq, WIDTH), lfn_t, cache_kt, cache_vt, cache_lft, l, page_table)
        y_fox = y_fox.reshape(t, WIDTH)

        nv = math.gcd(seq, SSD_CHUNK)
        q_ssd = max(nv, SSD_MIN_ROWS)
        sm3 = small.reshape(bsz, seq, LANES)
        if nv < q_ssd:
            sm3 = jnp.pad(sm3, ((0, 0), (0, q_ssd - nv), (0, 0)))
            smt3 = jnp.pad(smt3, ((0, 0), (0, 0), (0, q_ssd - nv)))
        y_ssd, conv_new, ssd_new = _ssd(
            z.reshape(bsz, seq, WIDTH), xbc.reshape(bsz, seq, CONV_DIM), sm3, smt3, conv0[l],
            ssd0, l, w['ssd_conv_w'], w['ssd_conv_b'], w['ssd_alog_row'],
            w['ssd_alog_col'], w['ssd_expand'], w['ssd_d'], w['ssd_norm_w'], nv, q_ssd)
        y_ssd = y_ssd.reshape(t, WIDTH)
        ssd_new = ssd_new.reshape(bsz, HEADS, HEAD_DIM, SSD_STATE)

        x2 = _merge(x2, y_s5, y_fox, y_ssd, w['w_gates'], w['b_gates'], w['w_branch_s5'],
                    w['w_branch_fox'], w['w_branch_ssd'], w['w_o'], w['ln1_g'], w['ln1_b'], tm_mix)
        if l % 2 == 0:
            x2 = _ffn(x2, w['ffn_w_gate'], w['ffn_w_up'], w['ffn_w_down'], w['ln2_g'], w['ln2_b'], tm_mix)
        else:
            x2 = _moe(x2, w['moe_w_router'], w['moe_b_router'], w['moe_w_gate'], w['moe_w_up'],
                      w['moe_w_down'], w['ln2_g'], w['ln2_b'], min(MOE_TOKEN_TILE, t))

        st = (k_out, v_out, logf_out, s5_re, s5_im, conv_new, ssd_new)
        for lst, s in zip(new, st):
            lst.append(s)
    return x2.reshape(bsz, seq, D_MODEL), [jnp.stack(s) for s in new]


def kernel(x_prompt, x_sample, cache_k, cache_v, cache_logf, page_table, state_s5_re, state_s5_im,
           state_conv, state_ssd, w_in, b_gate, b_fgate, s5_lam_re, s5_lam_im, s5_log_dt, s5_b_re,
           s5_b_im, s5_c_re, s5_c_im, s5_d, s5_w_glu, s5_b_glu, ssd_conv_w, ssd_conv_b, ssd_dt_bias,
           ssd_a_log, ssd_d, ssd_norm_w, w_branch_s5, w_branch_fox, w_branch_ssd, w_o, ln1_g, ln1_b,
           ln2_g, ln2_b, ffn_w_gate, ffn_w_up, ffn_w_down, moe_w_router, moe_b_router, moe_w_gate,
           moe_w_up, moe_w_down):
    p = dict(w_in=w_in, b_gate=b_gate, b_fgate=b_fgate, s5_d=s5_d, s5_w_glu=s5_w_glu, s5_b_glu=s5_b_glu,
             ssd_conv_w=ssd_conv_w, ssd_conv_b=ssd_conv_b, ssd_dt_bias=ssd_dt_bias, ssd_a_log=ssd_a_log,
             ssd_d=ssd_d, ssd_norm_w=ssd_norm_w, w_branch_s5=w_branch_s5, w_branch_fox=w_branch_fox,
             w_branch_ssd=w_branch_ssd, w_o=w_o, ln1_g=ln1_g, ln1_b=ln1_b, ln2_g=ln2_g, ln2_b=ln2_b,
             ffn_w_gate=ffn_w_gate, ffn_w_up=ffn_w_up, ffn_w_down=ffn_w_down, moe_w_router=moe_w_router,
             moe_b_router=moe_b_router, moe_w_gate=moe_w_gate, moe_w_up=moe_w_up, moe_w_down=moe_w_down)
    depth = w_in.shape[0]
    weights = [_layer_weights(p, l) for l in range(depth)]
    qs = {_s5_q(x_prompt.shape[1]), _s5_q(x_sample.shape[1])}
    s5_tabs = _all_s5_tables(s5_lam_re, s5_lam_im, s5_log_dt, s5_b_re, s5_b_im, s5_c_re, s5_c_im, qs)
    bsz = x_prompt.shape[0]
    z_s5 = jnp.zeros((depth, bsz, S5_GROUPS, S5_STATE), F32)
    z_conv = jnp.zeros((depth, bsz, SSD_CONV - 1, CONV_DIM), F32)
    z_ssd = jnp.zeros((depth, bsz, HEADS, HEAD_DIM, SSD_STATE), F32)
    y_p, st_p = _run_trunk(x_prompt, p, weights, s5_tabs, z_s5, z_s5, z_conv, z_ssd, None)
    y_s, st_s = _run_trunk(x_sample, p, weights, s5_tabs, state_s5_re, state_s5_im, state_conv, state_ssd,
                           (cache_k, cache_v, cache_logf, page_table))
    return (y_p, y_s, *st_p, *st_s)
```

```python
import functools
import math

import jax
import jax.numpy as jnp
import numpy as np
from jax import lax
from jax.experimental import pallas as pl
from jax.experimental.pallas import tpu as pltpu

F32 = jnp.float32
BF16 = jnp.bfloat16
HIGHEST = lax.Precision.HIGHEST

D_MODEL = 1024
N_BRANCH = 3
WIDTH = 512
S5_GROUPS = 32
S5_GROUP = 16
S5_STATE = 64
S5_LANES = S5_GROUPS * S5_STATE
S5_CHUNK = 16
HEADS = 8
HEAD_DIM = 64
SSD_STATE = 128
SSD_GROUPS = 2
SSD_CONV = 4
SSD_CHUNK = 128
SSD_MIN_ROWS = 16
SSD_SEQS_PER_STEP = 8
CONV_DIM = 1024
D_FF = 2816
N_EXPERTS = 8
PAGE = 128
ALPHA = 4.0 ** 0.25
LN_EPS = 1e-5
RMS_EPS = 1e-5
NEG = -1e30
LANES = 128
VMEM_LIMIT = 56 * 1024 * 1024

NT_DIMS = (((1,), (1,)), ((), ()))


def _params(*sem):
    return pltpu.CompilerParams(dimension_semantics=sem, vmem_limit_bytes=VMEM_LIMIT)


def _const_spec(shape):
    nd = len(shape)
    return pl.BlockSpec(shape, lambda *_: (0,) * nd, pipeline_mode=pl.Buffered(1))


def _layer_norm(x, g, b):
    mu = jnp.mean(x, axis=-1, keepdims=True)
    xc = x - mu
    var = jnp.mean(xc * xc, axis=-1, keepdims=True)
    return xc * lax.rsqrt(var + LN_EPS) * g + b


def _sigmoid(x):
    return 1.0 / (1.0 + jnp.exp(-x))


def _silu(x):
    return x * _sigmoid(x)


def _bf16_pieces(x):
    hi = x.astype(BF16)
    r1 = x - hi.astype(F32)
    mid = r1.astype(BF16)
    lo = (r1 - mid.astype(F32)).astype(BF16)
    return hi, mid, lo


def _small_act(xb, ws_ref, bs_ref):
    s = jnp.dot(xb, ws_ref[...], preferred_element_type=F32) + bs_ref[...]
    t = jnp.log1p(jnp.exp(-jnp.abs(s)))
    lane = lax.broadcasted_iota(jnp.int32, s.shape, 1)
    return jnp.where(lane < HEADS, jnp.minimum(s, 0.0) - t,
                     jnp.where(lane < 2 * HEADS, jnp.maximum(s, 0.0) + t, 0.0))


SEG_U, SEG_Q, SEG_K, SEG_V, SEG_Z, SEG_X, SEG_END = 0, 512, 1024, 1536, 2048, 2560, 3584


def _in_proj_sample_kernel(x_ref, wm_ref, ws_ref, bs_ref, u_ref, ub_ref, qb_ref, k_ref, v_ref,
                           z_ref, xbc_ref, small_ref, smallt_ref, logf_ref):
    xb = x_ref[...].astype(BF16)

    def seg(a, b):
        return jnp.dot(xb, wm_ref[:, a:b], preferred_element_type=F32)

    u = seg(SEG_U, SEG_Q)
    u_ref[...] = u
    ub_ref[...] = u.astype(BF16)
    qb_ref[...] = (seg(SEG_Q, SEG_K) * (HEAD_DIM ** -0.5)).astype(BF16)
    k_ref[...] = seg(SEG_K, SEG_V)
    v_ref[...] = seg(SEG_V, SEG_Z)
    z_ref[...] = seg(SEG_Z, SEG_X)
    xbc_ref[...] = seg(SEG_X, SEG_END)
    sm = _small_act(xb, ws_ref, bs_ref)
    small_ref[...] = sm
    smallt_ref[...] = sm.T[:2 * HEADS, :]
    logf_ref[...] = sm[:, :HEADS]


def _in_proj_sample(x2, wm, ws, bs, tm):
    t = x2.shape[0]
    row = lambda w: pl.BlockSpec((tm, w), lambda i: (i, 0))
    out_shape = (
        jax.ShapeDtypeStruct((t, WIDTH), F32),
        jax.ShapeDtypeStruct((t, WIDTH), BF16),
        jax.ShapeDtypeStruct((t, WIDTH), BF16),
        jax.ShapeDtypeStruct((t, WIDTH), F32),
        jax.ShapeDtypeStruct((t, WIDTH), F32),
        jax.ShapeDtypeStruct((t, WIDTH), F32),
        jax.ShapeDtypeStruct((t, CONV_DIM), F32),
        jax.ShapeDtypeStruct((t, LANES), F32),
        jax.ShapeDtypeStruct((2 * HEADS, t), F32),
        jax.ShapeDtypeStruct((t, HEADS), F32),
    )
    out_specs = (row(WIDTH), row(WIDTH), row(WIDTH), row(WIDTH), row(WIDTH),
                 row(WIDTH), row(CONV_DIM), row(LANES),
                 pl.BlockSpec((2 * HEADS, tm), lambda i: (0, i)), row(HEADS))
    return pl.pallas_call(
        _in_proj_sample_kernel,
        grid=(t // tm,),
        in_specs=[row(D_MODEL), _const_spec(wm.shape), _const_spec(ws.shape), _const_spec(bs.shape)],
        out_specs=out_specs,
        out_shape=out_shape,
        compiler_params=_params("parallel"),
        name="in_proj_sample",
    )(x2, wm, ws, bs)


PSEG_U, PSEG_K, PSEG_Z, PSEG_X, PSEG_END = 0, 512, 1536, 2048, 3072
TSEG_Q, TSEG_K, TSEG_V, TSEG_END = 0, 1024, 1536, 2048
BIAS_LANES = 3


def _in_proj_prompt_kernel(x_ref, wr_ref, wt_ref, ws_ref, bs_ref, e_ref, *rest):
    (u_ref, ub_ref, z_ref, xbc_ref, small_ref, smallt_ref, kaug_ref, qaugt_ref, kt_ref, vt_ref, tot_ref,
     u_sc) = rest[-12:]
    tm = x_ref.shape[0]
    xb = x_ref[...].astype(BF16)
    u = jnp.dot(xb, wr_ref[:, PSEG_U:PSEG_K], preferred_element_type=F32)
    u_ref[...] = u
    for k in range(WIDTH // LANES):
        u_sc[k] = u[:, LANES * k:LANES * (k + 1)]
    for s in range(S5_CHUNK):
        for k in range(WIDTH // LANES):
            lo_lane = s * WIDTH + LANES * k
            ub_ref[:, lo_lane:lo_lane + LANES] = u_sc[k, pl.ds(s, tm // S5_CHUNK, stride=S5_CHUNK), :].astype(BF16)
    z_ref[...] = jnp.dot(xb, wr_ref[:, PSEG_Z:PSEG_X], preferred_element_type=F32)
    xbc_ref[...] = jnp.dot(xb, wr_ref[:, PSEG_X:PSEG_END], preferred_element_type=F32)
    sm = _small_act(xb, ws_ref, bs_ref)
    small_ref[...] = sm
    smt = sm.T[:2 * HEADS, :]
    smallt_ref[0] = smt
    r_i = lax.broadcasted_iota(jnp.int32, (tm, tm), 0)
    c_i = lax.broadcasted_iota(jnp.int32, (tm, tm), 1)
    tri = (r_i >= c_i).astype(BF16)
    ones = jnp.ones((tm, tm), BF16)
    cloc = None
    tot = None
    for piece, piece_t in zip(_bf16_pieces(sm), _bf16_pieces(smt)):
        d = jnp.dot(tri, piece, preferred_element_type=F32)
        dt = jnp.dot(piece_t, ones, preferred_element_type=F32)
        cloc = d if cloc is None else cloc + d
        tot = dt if tot is None else tot + dt
    tot_ref[0, 0] = tot
    lane = lax.broadcasted_iota(jnp.int32, cloc.shape, 1)
    hi, mid, lo = _bf16_pieces(jnp.where(lane < HEADS, -cloc, 0.0))
    pieces = (hi.astype(F32) + pltpu.roll(mid.astype(F32), HEADS, axis=1)
              + pltpu.roll(lo.astype(F32), 2 * HEADS, axis=1)).astype(BF16)
    k_slabs = (jnp.dot(xb, wr_ref[:, PSEG_K:PSEG_Z], preferred_element_type=F32)
               + jnp.dot(pieces, e_ref[...], preferred_element_type=F32)).astype(BF16)
    for h in range(HEADS):
        kaug_ref[0, h] = k_slabs[:, LANES * h:LANES * (h + 1)]
    tall = lax.dot_general(wt_ref[...], xb, NT_DIMS, preferred_element_type=F32)
    row = lax.broadcasted_iota(jnp.int32, (LANES, tm), 0)
    ones_rows = (row >= HEAD_DIM) & (row < HEAD_DIM + BIAS_LANES)
    for h in range(HEADS):
        qh = tall[TSEG_Q + LANES * h:TSEG_Q + LANES * (h + 1), :]
        qaugt_ref[0, h] = jnp.where(ones_rows, 1.0, qh * (HEAD_DIM ** -0.5)).astype(BF16)
    kt_ref[0, 0] = tall[TSEG_K:TSEG_V, :]
    vt_ref[0, 0] = tall[TSEG_V:TSEG_END, :]


def _in_proj_prompt(x3, wr, wt, ws, bs, e3, tm, layer, depth, kv_prev):
    bsz, seq, _ = x3.shape
    nt = seq // tm
    x2 = x3.reshape(bsz * seq, D_MODEL)
    row = lambda w: pl.BlockSpec((tm, w), lambda b, i: (b * nt + i, 0))
    colt = lambda r: pl.BlockSpec((1, r, tm), lambda b, i: (b, 0, i))
    t = bsz * seq
    out_shape = (
        jax.ShapeDtypeStruct((t, WIDTH), F32),
        jax.ShapeDtypeStruct((t // S5_CHUNK, S5_CHUNK * WIDTH), BF16),
        jax.ShapeDtypeStruct((t, WIDTH), F32),
        jax.ShapeDtypeStruct((t, CONV_DIM), F32),
        jax.ShapeDtypeStruct((t, LANES), F32),
        jax.ShapeDtypeStruct((bsz, 2 * HEADS, seq), F32),
        jax.ShapeDtypeStruct((bsz, HEADS, seq, LANES), BF16),
        jax.ShapeDtypeStruct((bsz, HEADS, LANES, seq), BF16),
        jax.ShapeDtypeStruct((depth, bsz, WIDTH, seq), F32),
        jax.ShapeDtypeStruct((depth, bsz, WIDTH, seq), F32),
        jax.ShapeDtypeStruct((bsz, nt, 2 * HEADS, tm), F32),
    )
    layer_colt = pl.BlockSpec((1, 1, WIDTH, tm), lambda b, i: (layer, b, 0, i))
    out_specs = (row(WIDTH),
                 pl.BlockSpec((tm // S5_CHUNK, S5_CHUNK * WIDTH), lambda b, i: (b * nt + i, 0)),
                 row(WIDTH), row(CONV_DIM), row(LANES), colt(2 * HEADS),
                 pl.BlockSpec((1, HEADS, tm, LANES), lambda b, i: (b, 0, i, 0)),
                 pl.BlockSpec((1, HEADS, LANES, tm), lambda b, i: (b, 0, 0, i)),
                 layer_colt, layer_colt,
                 pl.BlockSpec((1, 1, 2 * HEADS, tm), lambda b, i: (b, i, 0, 0)))
    consts = [wr, wt, ws, bs, e3]
    n_in = 1 + len(consts)
    kv_prev = list(kv_prev)
    aliases = {n_in + a: 8 + a for a in range(len(kv_prev))}
    return pl.pallas_call(
        _in_proj_prompt_kernel,
        grid=(bsz, nt),
        in_specs=[row(D_MODEL)] + [_const_spec(c.shape) for c in consts]
                 + [pl.BlockSpec(memory_space=pl.ANY) for _ in kv_prev],
        out_specs=out_specs,
        out_shape=out_shape,
        input_output_aliases=aliases,
        scratch_shapes=[pltpu.VMEM((WIDTH // LANES, tm, LANES), F32)],
        compiler_params=_params("parallel", "parallel"),
        name="in_proj_prompt",
    )(x2, *consts, *kv_prev)


def _cumsum_lanes(x):
    n = x.shape[-1]
    lane = lax.broadcasted_iota(jnp.int32, x.shape, x.ndim - 1)
    s = 1
    while s < n:
        x = x + jnp.where(lane >= s, pltpu.roll(x, s, axis=x.ndim - 1), 0.0)
        s *= 2
    return x


FOX_HEADS_PER_STEP = 8


def _fox_prompt_kernel(it_ref, jt_ref, q_ref, k_ref, v_ref, tot_ref, o_ref, m_sc, l_sc, c_sc, acc_sc, *, tq, tk):
    g = pl.program_id(1)
    t = pl.program_id(2)
    i = it_ref[t]
    j = jt_ref[t]
    hps = FOX_HEADS_PER_STEP

    @pl.when(j == 0)
    def _():
        m_sc[...] = jnp.full(m_sc.shape, NEG, F32)
        l_sc[...] = jnp.zeros(l_sc.shape, F32)
        c_sc[...] = jnp.zeros(c_sc.shape, F32)
        acc_sc[...] = jnp.zeros(acc_sc.shape, F32)

    def step(masked):
        if masked:
            kpos = j * tk + lax.broadcasted_iota(jnp.int32, (tk, tq), 0)
            qpos = i * tq + lax.broadcasted_iota(jnp.int32, (tk, tq), 1)
            causal = kpos <= qpos
        for h in range(hps):
            s = jnp.dot(k_ref[0, h], q_ref[0, h], preferred_element_type=F32)
            if masked:
                s = jnp.where(causal, s, NEG)
            c_j = c_sc[h]
            m_prev = m_sc[h]
            m_new = jnp.maximum(m_prev, jnp.max(s, axis=0, keepdims=True) - c_j)
            alpha = jnp.exp(m_prev - m_new)
            p = jnp.exp(s - (m_new + c_j))
            l_sc[h] = alpha * l_sc[h] + jnp.sum(p, axis=0, keepdims=True)
            rows = slice(HEAD_DIM * h, HEAD_DIM * (h + 1))
            acc_sc[rows, :] = alpha * acc_sc[rows, :] + jnp.dot(
                v_ref[0, rows, :].astype(BF16), p.astype(BF16), preferred_element_type=F32)
            m_sc[h] = m_new
            c_sc[h] = c_j + tot_ref[0, 0, pl.ds(g * hps + h, 1), :]

    @pl.when(j < i)
    def _():
        step(False)

    @pl.when(j == i)
    def _():
        step(True)
        inv = jnp.concatenate([jnp.broadcast_to(1.0 / l_sc[h], (HEAD_DIM, tq)) for h in range(hps)], axis=0)
        o_ref[0] = (acc_sc[...] * inv).T.astype(o_ref.dtype)


def _fox_prompt(qaugt, kaug, vt, layer, tot, tq):
    bsz, _, _, seq = qaugt.shape
    nq = seq // tq
    hps = FOX_HEADS_PER_STEP
    it = np.array([i for i in range(nq) for j in range(i + 1)], np.int32)
    jt = np.array([j for i in range(nq) for j in range(i + 1)], np.int32)
    grid_spec = pltpu.PrefetchScalarGridSpec(
        num_scalar_prefetch=2,
        grid=(bsz, HEADS // hps, len(it)),
        in_specs=[
            pl.BlockSpec((1, hps, LANES, tq), lambda b, g, t, it, jt: (b, g, 0, it[t])),
            pl.BlockSpec((1, hps, tq, LANES), lambda b, g, t, it, jt: (b, g, jt[t], 0)),
            pl.BlockSpec((None, 1, hps * HEAD_DIM, tq), lambda b, g, t, it, jt: (layer, b, g, jt[t])),
            pl.BlockSpec((1, 1, 2 * HEADS, tq), lambda b, g, t, it, jt: (b, jt[t], 0, 0)),
        ],
        out_specs=pl.BlockSpec((1, tq, hps * HEAD_DIM), lambda b, g, t, it, jt: (b, it[t], g)),
        scratch_shapes=[pltpu.VMEM((hps, 1, tq), F32), pltpu.VMEM((hps, 1, tq), F32),
                        pltpu.VMEM((hps, 1, tq), F32), pltpu.VMEM((hps * HEAD_DIM, tq), F32)],
    )
    return pl.pallas_call(
        functools.partial(_fox_prompt_kernel, tq=tq, tk=tq),
        grid_spec=grid_spec,
        out_shape=jax.ShapeDtypeStruct((bsz, seq, WIDTH), BF16),
        compiler_params=_params("parallel", "parallel", "arbitrary"),
        name="fox_prompt",
    )(jnp.asarray(it), jnp.asarray(jt), qaugt, kaug, vt, tot)


FOX_SEQS_PER_STEP = 1


def _fox_sample_kernel(pt_ref, q_ref, kn_ref, vn_ref, lfn_ref, *refs, n_pages, tq, group):
    lf_ref = refs[2 * group * n_pages]
    o_ref = refs[2 * group * n_pages + 1]
    for s in range(group):
        one = pl.ds(s, 1)
        _fox_sample_sequence(pt_ref, pl.program_id(0) * group + s, q_ref.at[one], kn_ref.at[one],
                             vn_ref.at[one], lfn_ref.at[one], refs[s * n_pages:(s + 1) * n_pages],
                             refs[(group + s) * n_pages:(group + s + 1) * n_pages], lf_ref, o_ref.at[one],
                             n_pages=n_pages, tq=tq)


def _fox_sample_sequence(pt_ref, b, q_ref, kn_ref, vn_ref, lfn_ref, k_refs, v_refs, lf_ref, o_ref, *, n_pages, tq):
    rows = tq * HEADS
    q = q_ref[0].astype(F32)
    qe = jnp.concatenate([jnp.broadcast_to(q[t:t + 1, :], (HEADS, WIDTH)) for t in range(tq)],
                         axis=0)
    r_w = lax.broadcasted_iota(jnp.int32, (rows, WIDTH), 0)
    l_w = lax.broadcasted_iota(jnp.int32, (rows, WIDTH), 1)
    head_mask = (l_w // HEAD_DIM) == (r_w % HEADS)
    qbd = jnp.where(head_mask, qe, 0.0).astype(BF16)

    scores = [None] * n_pages
    carry = jnp.zeros((HEADS, 1), F32)
    for p in range(n_pages - 1, -1, -1):
        cs = _cumsum_lanes(lf_ref[0, pt_ref[b * n_pages + p]])
        tot = cs[:, PAGE - 1:PAGE]
        dec = (carry + tot) - cs
        carry = carry + tot
        kt = k_refs[p][0, 0].reshape(WIDTH, PAGE).astype(BF16)
        s = jnp.dot(qbd, kt, preferred_element_type=F32)
        scores[p] = s + jnp.concatenate([dec] * tq, axis=0)
    pad = jnp.zeros((PAGE - tq, WIDTH), F32)
    kn = jnp.concatenate([kn_ref[0], pad], axis=0).astype(BF16)
    vn = jnp.concatenate([vn_ref[0], pad], axis=0).astype(BF16)
    cn = _cumsum_lanes(lfn_ref[0])
    r_p = lax.broadcasted_iota(jnp.int32, (rows, PAGE), 0)
    c_p = lax.broadcasted_iota(jnp.int32, (rows, PAGE), 1)
    s_new = lax.dot_general(qbd, kn, NT_DIMS, preferred_element_type=F32)
    s_new = jnp.where(c_p <= r_p // HEADS, s_new - jnp.concatenate([cn] * tq, axis=0), NEG)

    m = s_new
    for p in range(n_pages):
        m = jnp.maximum(m, scores[p])
    m = jnp.max(m, axis=-1, keepdims=True)
    pn = jnp.exp(s_new - m)
    l = jnp.sum(pn, axis=-1, keepdims=True)
    o = jnp.dot(pn.astype(BF16), vn, preferred_element_type=F32)
    for p in range(n_pages):
        pp = jnp.exp(scores[p] - m)
        l = l + jnp.sum(pp, axis=-1, keepdims=True)
        vt = v_refs[p][0, 0].reshape(WIDTH, PAGE).astype(BF16)
        o = o + lax.dot_general(pp.astype(BF16), vt, NT_DIMS, preferred_element_type=F32)
    om = jnp.where(head_mask, o / l, 0.0)
    out = jnp.concatenate([jnp.sum(om[HEADS * t:HEADS * (t + 1), :], axis=0, keepdims=True) for t in range(tq)],
                          axis=0)
    o_ref[0] = out.astype(o_ref.dtype)


def _fox_sample(qb, k_new, v_new, lfn_t, cache_kt, cache_vt, cache_lft, layer, page_table):
    dbsz, tq, _ = qb.shape
    n_pages = page_table.shape[1]
    pt = page_table.reshape(-1).astype(jnp.int32)

    g = FOX_SEQS_PER_STEP if dbsz % FOX_SEQS_PER_STEP == 0 else 1

    def page_spec(s, p):
        return pl.BlockSpec((1, 1, HEADS, HEAD_DIM, PAGE),
                            lambda b, pt: (layer, pt[(b * g + s) * n_pages + p], 0, 0, 0))

    seq_spec = lambda shape: pl.BlockSpec(shape, lambda b, pt: (b, 0, 0))
    in_specs = [seq_spec((g, tq, WIDTH)), seq_spec((g, tq, WIDTH)), seq_spec((g, tq, WIDTH)),
                seq_spec((g, HEADS, LANES))]
    in_specs += [page_spec(s, p) for s in range(g) for p in range(n_pages)]
    in_specs += [page_spec(s, p) for s in range(g) for p in range(n_pages)]
    in_specs += [pl.BlockSpec((1,) + cache_lft.shape[1:], lambda b, pt: (layer, 0, 0, 0),
                              pipeline_mode=pl.Buffered(1))]
    grid_spec = pltpu.PrefetchScalarGridSpec(
        num_scalar_prefetch=1, grid=(dbsz // g,), in_specs=in_specs,
        out_specs=pl.BlockSpec((g, tq, WIDTH), lambda b, pt: (b, 0, 0)))
    return pl.pallas_call(
        functools.partial(_fox_sample_kernel, n_pages=n_pages, tq=tq, group=g),
        grid_spec=grid_spec,
        out_shape=jax.ShapeDtypeStruct((dbsz, tq, WIDTH), F32),
        compiler_params=_params("parallel"),
        name="fox_sample",
    )(pt, qb, k_new, v_new, lfn_t, *([cache_kt] * (g * n_pages)), *([cache_vt] * (g * n_pages)), cache_lft)


def _s5_tables(lam_re, lam_im, log_dt, b_re, b_im, c_re, c_im, q):
    hp = dict(precision=HIGHEST)
    dt = jnp.exp(log_dt)[:, None]
    mag = jnp.exp(lam_re * dt)
    ab_re = mag * jnp.cos(lam_im * dt)
    ab_im = mag * jnp.sin(lam_im * dt)
    nr = ab_re - 1.0
    den = lam_re * lam_re + lam_im * lam_im
    q_re = (nr * lam_re + ab_im * lam_im) / den
    q_im = (ab_im * lam_re - nr * lam_im) / den
    bb_re = q_re[..., None] * b_re - q_im[..., None] * b_im
    bb_im = q_re[..., None] * b_im + q_im[..., None] * b_re
    jj = jnp.arange(q + 1, dtype=F32)[:, None, None]
    pmag = jnp.exp(lam_re * dt * jj)
    p_re = pmag * jnp.cos(lam_im * dt * jj)
    p_im = pmag * jnp.sin(lam_im * dt * jj)
    ab_b_re = p_re[:q, :, :, None] * bb_re - p_im[:q, :, :, None] * bb_im
    ab_b_im = p_re[:q, :, :, None] * bb_im + p_im[:q, :, :, None] * bb_re
    kern = (jnp.einsum('gcn,jgnd->jgcd', c_re, ab_b_re, **hp)
            - jnp.einsum('gcn,jgnd->jgcd', c_im, ab_b_im, **hp))
    kt = kern.transpose(0, 1, 3, 2).reshape(q, 2, 16 * S5_GROUP, S5_GROUP)
    tile_cols = jnp.tile(jnp.eye(S5_GROUP, dtype=F32), (1, 16))
    r_g = lax.broadcasted_iota(jnp.int32, (256, 256), 0) // S5_GROUP
    c_g = lax.broadcasted_iota(jnp.int32, (256, 256), 1) // S5_GROUP
    w_toep = jnp.where(r_g == c_g, jnp.einsum('jxrb,bc->jxrc', kt, tile_cols, **hp), 0.0).astype(BF16)
    eye_g = jnp.eye(S5_GROUPS, dtype=F32)

    def in_map(bb):
        full = jnp.einsum('gnc,gh->gchn', bb, eye_g).reshape(WIDTH, S5_LANES)
        return jnp.stack([full[128 * (n // 2):128 * (n // 2) + 128, 256 * n:256 * n + 256]
                          for n in range(8)]).astype(BF16)

    def out_map(c):
        full = jnp.einsum('gcn,gh->gnhc', c, eye_g).reshape(S5_LANES, WIDTH)
        return jnp.stack([full[512 * m:512 * m + 512, 128 * m:128 * m + 128]
                          for m in range(4)]).astype(BF16)

    rows = ((q + 1 + 7) // 8) * 8
    padp = lambda p: jnp.pad(p.reshape(q + 1, S5_LANES), ((0, rows - q - 1), (0, 0)))
    return dict(w_toep=w_toep, wb_re=in_map(bb_re), wb_im=in_map(bb_im), wc_re=out_map(c_re),
                wc_im=out_map(-c_im), p_re=padp(p_re), p_im=padp(p_im))


def _all_s5_tables(lam_re, lam_im, log_dt, b_re, b_im, c_re, c_im, qs):
    f32 = lambda a: a.astype(F32)
    q_max = max(qs)
    full = jax.vmap(functools.partial(_s5_tables, q=q_max))(
        f32(lam_re), f32(lam_im), f32(log_dt), f32(b_re), f32(b_im), f32(c_re), f32(c_im))
    out = {}
    for q in qs:
        rows = ((q + 1 + 7) // 8) * 8
        out[q] = []
        for l in range(lam_re.shape[0]):
            tab = {name: arr[l] for name, arr in full.items()}
            tab['w_toep'] = tab['w_toep'][:q]
            tab['p_re'] = tab['p_re'][:rows]
            tab['p_im'] = tab['p_im'][:rows]
            out[q].append(tab)
    return out


def _s5_chunk_kernel(u_ref, w_ref, wbr_ref, wbi_ref, pr_ref, pi_ref, y_ref, sr_ref, si_ref, *, q):
    for s_out in range(q):
        for half in range(2):
            acc = None
            for j in range(s_out + 1):
                a = (s_out - j) * WIDTH + half * 256
                d = jnp.dot(u_ref[:, a:a + 256], w_ref[j, half], preferred_element_type=F32)
                acc = d if acc is None else acc + d
            o = s_out * WIDTH + half * 256
            y_ref[:, o:o + 256] = acc
    for n in range(8):
        lanes = slice(256 * n, 256 * (n + 1))
        acc_r = None
        acc_i = None
        for s in range(q):
            a = s * WIDTH + 128 * (n // 2)
            br = jnp.dot(u_ref[:, a:a + 128], wbr_ref[n], preferred_element_type=F32)
            bi = jnp.dot(u_ref[:, a:a + 128], wbi_ref[n], preferred_element_type=F32)
            pr = pr_ref[q - 1 - s:q - s, lanes]
            pi = pi_ref[q - 1 - s:q - s, lanes]
            tr = pr * br - pi * bi
            ti = pr * bi + pi * br
            acc_r = tr if acc_r is None else acc_r + tr
            acc_i = ti if acc_i is None else acc_i + ti
        sr_ref[:, lanes] = acc_r
        si_ref[:, lanes] = acc_i


def _s5_chunks(u2b, tab, q, rt):
    nc = u2b.shape[0]
    consts = [tab['w_toep'], tab['wb_re'], tab['wb_im'], tab['p_re'], tab['p_im']]
    state = pl.BlockSpec((rt, S5_LANES), lambda i: (i, 0))
    return pl.pallas_call(
        functools.partial(_s5_chunk_kernel, q=q),
        grid=(nc // rt,),
        in_specs=[pl.BlockSpec((rt, q * WIDTH), lambda i: (i, 0))] + [_const_spec(c.shape) for c in consts],
        out_specs=(pl.BlockSpec((rt, q * WIDTH), lambda i: (i, 0)), state, state),
        out_shape=(jax.ShapeDtypeStruct((nc, q * WIDTH), F32),
                   jax.ShapeDtypeStruct((nc, S5_LANES), F32), jax.ShapeDtypeStruct((nc, S5_LANES), F32)),
        compiler_params=_params("parallel"),
        name="s5_chunks",
    )(u2b, *consts)


def _s5_scan_kernel(u_ref, yi_ref, sre_ref, sim_ref, h0r_ref, h0i_ref, wcr_ref, wci_ref, pr_ref, pi_ref,
                    d_ref, wg_ref, bg_ref, y_ref, htr_ref, hti_ref,
                    xcr_sc, xci_sc, hr_sc, hi_sc, yi_sc, *, q, nct, carry):
    tile = pl.program_id(1)
    u = u_ref[0]
    aq_r = pr_ref[q:q + 1, :]
    aq_i = pi_ref[q:q + 1, :]
    p1r = pr_ref[1:q + 1, :]
    p1i = pi_ref[1:q + 1, :]

    if carry:
        @pl.when(tile == 0)
        def _():
            hr_sc[...] = h0r_ref[0]
            hi_sc[...] = h0i_ref[0]

        def body(c, h):
            hr, hi = h
            row0 = pl.multiple_of(c * q, q)
            xcr_sc[pl.ds(row0, q), :] = p1r * hr - p1i * hi
            xci_sc[pl.ds(row0, q), :] = p1r * hi + p1i * hr
            sr = sre_ref[pl.ds(c, 1), :]
            si = sim_ref[pl.ds(c, 1), :]
            return aq_r * hr - aq_i * hi + sr, aq_r * hi + aq_i * hr + si

        hr, hi = lax.fori_loop(0, nct, body, (hr_sc[...], hi_sc[...]))
        hr_sc[...] = hr
        hi_sc[...] = hi

        @pl.when(tile == pl.num_programs(1) - 1)
        def _():
            htr_ref[0] = hr
            hti_ref[0] = hi
    else:
        def body(c, _):
            hr = h0r_ref[pl.ds(c, 1), :]
            hi = h0i_ref[pl.ds(c, 1), :]
            row0 = pl.multiple_of(c * q, q)
            xcr_sc[pl.ds(row0, q), :] = p1r * hr - p1i * hi
            xci_sc[pl.ds(row0, q), :] = p1r * hi + p1i * hr
            return 0

        lax.fori_loop(0, nct, body, 0)
        h0r = h0r_ref[...]
        h0i = h0i_ref[...]
        htr_ref[...] = aq_r * h0r - aq_i * h0i + sre_ref[...]
        hti_ref[...] = aq_r * h0i + aq_i * h0r + sim_ref[...]

    cols = []
    for m in range(4):
        xr = xcr_sc[:, 512 * m:512 * m + 512].astype(BF16)
        xi = xci_sc[:, 512 * m:512 * m + 512].astype(BF16)
        cols.append(jnp.dot(xr, wcr_ref[m], preferred_element_type=F32)
                    + jnp.dot(xi, wci_ref[m], preferred_element_type=F32))
    for s in range(q):
        for k in range(WIDTH // LANES):
            lo_lane = s * WIDTH + LANES * k
            yi_sc[k, pl.ds(s, nct, stride=q), :] = yi_ref[:, lo_lane:lo_lane + LANES]
    yi = jnp.concatenate([yi_sc[k] for k in range(WIDTH // LANES)], axis=-1)
    y = yi + jnp.concatenate(cols, axis=-1) + d_ref[...] * u
    y = jax.nn.gelu(y)
    g = jnp.dot(y.astype(BF16), wg_ref[...], preferred_element_type=F32) + bg_ref[...]
    y_ref[0] = (y * _sigmoid(g)).astype(y_ref.dtype)


def _s5_scan(u3, yi2, s_re, s_im, h0_re, h0_im, tab, d_row, w_glu, b_glu, q, tm, carry):
    bsz, seq, _ = u3.shape
    nct = tm // q
    n_tiles = seq // tm
    tok = pl.BlockSpec((1, tm, WIDTH), lambda b, t: (b, t, 0))
    chunk = pl.BlockSpec((nct, S5_LANES), lambda b, t: (b * n_tiles + t, 0))
    chunk_y = pl.BlockSpec((nct, q * WIDTH), lambda b, t: (b * n_tiles + t, 0))
    if carry:
        h_spec = pl.BlockSpec((1, 1, S5_LANES), lambda b, t: (b, 0, 0))
        h_shape = jax.ShapeDtypeStruct((bsz, 1, S5_LANES), F32)
    else:
        h_spec = pl.BlockSpec((nct, S5_LANES), lambda b, t: (t, 0))
        h_shape = jax.ShapeDtypeStruct((seq // q, S5_LANES), F32)
    consts = [tab['wc_re'], tab['wc_im'], tab['p_re'], tab['p_im'], d_row, w_glu, b_glu]
    return pl.pallas_call(
        functools.partial(_s5_scan_kernel, q=q, nct=nct, carry=carry),
        grid=(bsz, n_tiles),
        in_specs=[tok, chunk_y, chunk, chunk, h_spec, h_spec] + [_const_spec(c.shape) for c in consts],
        out_specs=(tok, h_spec, h_spec),
        out_shape=(jax.ShapeDtypeStruct((bsz, seq, WIDTH), BF16), h_shape, h_shape),
        scratch_shapes=[pltpu.VMEM((tm, S5_LANES), F32), pltpu.VMEM((tm, S5_LANES), F32),
                        pltpu.VMEM((1, S5_LANES), F32), pltpu.VMEM((1, S5_LANES), F32),
                        pltpu.VMEM((WIDTH // LANES, tm, LANES), F32)],
        compiler_params=_params("parallel", "arbitrary"),
        name="s5_scan",
    )(u3, yi2, s_re, s_im, h0_re, h0_im, *consts)


def _ssd_kernel(z_ref, xbc_ref, sm_ref, smt_ref, conv0_ref, h0_ref, cw_ref, cb_ref, alog_row_ref,
                alog_col_ref, e_ref, d_ref, nw_ref, y_ref, convt_ref, ht_ref,
                xp_sc, st_sc, zp_sc, *, nv, q, group):
    for s in range(group):
        one = pl.ds(s, 1)
        _ssd_sequence(z_ref.at[one], xbc_ref.at[one], sm_ref.at[one], smt_ref.at[one], conv0_ref.at[one],
                      h0_ref.at[:, one], cw_ref, cb_ref, alog_row_ref, alog_col_ref, e_ref, d_ref, nw_ref,
                      y_ref.at[one], convt_ref.at[one], ht_ref.at[one],
                      xp_sc.at[s], st_sc.at[s], zp_sc.at[s], nv=nv, q=q)


def _ssd_sequence(z_ref, xbc_ref, sm_ref, smt_ref, conv0_ref, h0_ref, cw_ref, cb_ref, alog_row_ref,
                  alog_col_ref, e_ref, d_ref, nw_ref, y_ref, convt_ref, ht_ref,
                  xp_sc, st_sc, zp_sc, *, nv, q):
    c = pl.program_id(1)
    last = c == pl.num_programs(1) - 1

    @pl.when(c == 0)
    def _():
        st_sc[...] = h0_ref[0, 0]
        xp_sc[...] = jnp.zeros(xp_sc.shape, F32)
        xp_sc[8 - (SSD_CONV - 1):8, :] = conv0_ref[0]
        if nv < q:
            zp_sc[...] = jnp.zeros(zp_sc.shape, F32)

    xp_sc[8:8 + nv, :] = xbc_ref[0]
    acc = None
    for j in range(SSD_CONV):
        o = 8 - (SSD_CONV - 1) + j
        term = xp_sc[o:o + q, :] * cw_ref[j:j + 1, :]
        acc = term if acc is None else acc + term
    xc = _silu(acc + cb_ref[...])
    tail = xp_sc[8 + nv - (SSD_CONV - 1):8 + nv, :]

    @pl.when(last)
    def _():
        convt_ref[0] = tail

    xp_sc[8 - (SSD_CONV - 1):8, :] = tail
    if nv < q:
        zp_sc[0:nv, :] = z_ref[0]
        z = zp_sc[...]
    else:
        z = z_ref[0]

    xs = xc[:, :WIDTH]
    bm = xc[:, WIDTH:WIDTH + SSD_GROUPS * SSD_STATE]
    cm = xc[:, WIDTH + SSD_GROUPS * SSD_STATE:]

    sm = sm_ref[0]
    smt = smt_ref[0]
    lane = lax.broadcasted_iota(jnp.int32, (1, LANES), 1)
    a_row = jnp.where((lane >= HEADS) & (lane < 2 * HEADS), -jnp.exp(alog_row_ref[...]), 0.0)
    rowi = lax.broadcasted_iota(jnp.int32, (2 * HEADS, 1), 0)
    a_col = jnp.where(rowi >= HEADS, -jnp.exp(alog_col_ref[...]), 0.0)
    adt = sm * a_row
    adt_t = smt * a_col
    r_i = lax.broadcasted_iota(jnp.int32, (q, q), 0)
    c_i = lax.broadcasted_iota(jnp.int32, (q, q), 1)
    tri = r_i >= c_i
    tri_lo = tri.astype(BF16)
    tri_up = (r_i <= c_i).astype(BF16)
    acum = None
    acum_t = None
    for piece, piece_t in zip(_bf16_pieces(adt), _bf16_pieces(adt_t)):
        d = jnp.dot(tri_lo, piece, preferred_element_type=F32)
        dt = jnp.dot(piece_t, tri_up, preferred_element_type=F32)
        acum = d if acum is None else acum + d
        acum_t = dt if acum_t is None else acum_t + dt
    atot = acum[q - 1:q, :]
    stacked = jnp.concatenate([sm, jnp.exp(acum), jnp.exp(atot - acum)], axis=0)
    hi, mid, lo = _bf16_pieces(stacked)
    e = e_ref[...]
    expanded = (jnp.dot(hi, e, preferred_element_type=F32) + jnp.dot(mid, e, preferred_element_type=F32)
                + jnp.dot(lo, e, preferred_element_type=F32))
    dt_e = expanded[0:q]
    eac_e = expanded[q:2 * q]
    dec_e = expanded[2 * q:3 * q]
    xd = xs * dt_e
    xdd = xd * dec_e
    lane_half = lax.broadcasted_iota(jnp.int32, (q, LANES), 1) // HEAD_DIM

    y_cols = []
    for g in range(SSD_GROUPS):
        bg = bm[:, SSD_STATE * g:SSD_STATE * (g + 1)].astype(BF16)
        cg = cm[:, SSD_STATE * g:SSD_STATE * (g + 1)].astype(BF16)
        cb = lax.dot_general(cg, bg, NT_DIMS, preferred_element_type=F32)
        hpg = HEADS // SSD_GROUPS
        st_g = st_sc[hpg * HEAD_DIM * g:hpg * HEAD_DIM * (g + 1), :]
        y_off = lax.dot_general(cg, st_g.astype(BF16), NT_DIMS, preferred_element_type=F32)
        for pair in range(hpg // 2):
            ys = []
            for hh in range(2):
                h = hpg * g + 2 * pair + hh
                diff = acum[:, HEADS + h:HEADS + h + 1] - acum_t[HEADS + h:HEADS + h + 1, :]
                lmat = jnp.exp(jnp.where(tri, diff, NEG))
                w = (cb * lmat).astype(BF16)
                lo = LANES * (2 * g + pair)
                ys.append(jnp.dot(w, xd[:, lo:lo + LANES].astype(BF16), preferred_element_type=F32))
            y_cols.append(jnp.where(lane_half == 0, ys[0], ys[1]))
        y_cols[-2] = y_cols[-2] + y_off[:, :LANES] * eac_e[:, 256 * g:256 * g + LANES]
        y_cols[-1] = y_cols[-1] + y_off[:, LANES:] * eac_e[:, 256 * g + LANES:256 * (g + 1)]
        contrib = jnp.dot(xdd[:, 256 * g:256 * (g + 1)].T.astype(BF16), bg, preferred_element_type=F32)
        for hl in range(hpg):
            h = hpg * g + hl
            sl = slice(HEAD_DIM * h, HEAD_DIM * (h + 1))
            dec_h = jnp.exp(acum_t[HEADS + h:HEADS + h + 1, q - 1:q])
            st_sc[sl, :] = st_sc[sl, :] * dec_h + contrib[HEAD_DIM * hl:HEAD_DIM * (hl + 1), :]

    y = jnp.concatenate(y_cols, axis=-1) + d_ref[...] * xs
    y = y * _silu(z)
    y = y * lax.rsqrt(jnp.mean(y * y, axis=-1, keepdims=True) + RMS_EPS) * nw_ref[...]
    y_ref[0] = y[:nv].astype(y_ref.dtype)

    @pl.when(last)
    def _():
        ht_ref[0] = st_sc[...]


def _ssd(z3, xbc3, sm3, smt3, conv0, h0, layer, cw, cb, alog_row, alog_col, e_mat, d_row, nw, nv, q):
    bsz, seq, _ = z3.shape
    nchunk = seq // nv
    g = SSD_SEQS_PER_STEP if (nchunk == 1 and bsz % SSD_SEQS_PER_STEP == 0) else 1
    consts = [cw, cb, alog_row, alog_col, e_mat, d_row, nw]
    return pl.pallas_call(
        functools.partial(_ssd_kernel, nv=nv, q=q, group=g),
        grid=(bsz // g, nchunk),
        in_specs=[pl.BlockSpec((g, nv, WIDTH), lambda b, c: (b, c, 0)),
                  pl.BlockSpec((g, nv, CONV_DIM), lambda b, c: (b, c, 0)),
                  pl.BlockSpec((g, q, LANES), lambda b, c: (b, c, 0)),
                  pl.BlockSpec((g, 2 * HEADS, q), lambda b, c: (b, 0, c)),
                  pl.BlockSpec((g, SSD_CONV - 1, CONV_DIM), lambda b, c: (b, 0, 0)),
                  pl.BlockSpec((1, g, WIDTH, SSD_STATE), lambda b, c: (layer, b, 0, 0))]
                 + [_const_spec(x.shape) for x in consts],
        out_specs=(pl.BlockSpec((g, nv, WIDTH), lambda b, c: (b, c, 0)),
                   pl.BlockSpec((g, SSD_CONV - 1, CONV_DIM), lambda b, c: (b, 0, 0)),
                   pl.BlockSpec((g, WIDTH, SSD_STATE), lambda b, c: (b, 0, 0))),
        out_shape=(jax.ShapeDtypeStruct((bsz, seq, WIDTH), BF16 if nv % 16 == 0 else F32),
                   jax.ShapeDtypeStruct((bsz, SSD_CONV - 1, CONV_DIM), F32),
                   jax.ShapeDtypeStruct((bsz, WIDTH, SSD_STATE), F32)),
        scratch_shapes=[pltpu.VMEM((g, 8 + q, CONV_DIM), F32), pltpu.VMEM((g, WIDTH, SSD_STATE), F32),
                        pltpu.VMEM((g, q, WIDTH), F32)],
        compiler_params=_params("parallel", "arbitrary"),
        name="ssd",
    )(z3, xbc3, sm3, smt3, conv0, h0, *consts)


def _merge_kernel(x_ref, ys5_ref, yfox_ref, yssd_ref, wg_ref, bg_ref, ws5_ref, wfox_ref, wssd_ref,
                  wo_ref, g_ref, b_ref, o_ref):
    x = x_ref[...]
    xb = x.astype(BF16)
    merged = None
    for br, (y_ref, w_ref) in enumerate(((ys5_ref, ws5_ref), (yfox_ref, wfox_ref), (yssd_ref, wssd_ref))):
        lo = br * D_MODEL
        gate = _sigmoid(jnp.dot(xb, wg_ref[:, lo:lo + D_MODEL], preferred_element_type=F32)
                        + bg_ref[:, lo:lo + D_MODEL])
        term = gate * jnp.dot(y_ref[...].astype(BF16), w_ref[...], preferred_element_type=F32)
        merged = term if merged is None else merged + term
    out = jnp.dot(merged.astype(BF16), wo_ref[...], preferred_element_type=F32)
    o_ref[...] = _layer_norm(ALPHA * x + out, g_ref[...], b_ref[...])


def _merge(x2, ys5, yfox, yssd, wg, bg, ws5, wfox, wssd, wo, g, b, tm):
    t = x2.shape[0]
    row = lambda w: pl.BlockSpec((tm, w), lambda i: (i, 0))
    consts = [wg, bg, ws5, wfox, wssd, wo, g, b]
    return pl.pallas_call(
        _merge_kernel,
        grid=(t // tm,),
        in_specs=[row(D_MODEL), row(WIDTH), row(WIDTH), row(WIDTH)] + [_const_spec(c.shape) for c in consts],
        out_specs=row(D_MODEL),
        out_shape=jax.ShapeDtypeStruct((t, D_MODEL), F32),
        compiler_params=_params("parallel"),
        name="merge",
    )(x2, ys5, yfox, yssd, *consts)


FF_CHUNK = 256


def _ffn_kernel(x_ref, wg_ref, wu_ref, wd_ref, g_ref, b_ref, o_ref):
    x = x_ref[...]
    xb = x.astype(BF16)
    acc = None
    for c in range(0, D_FF, FF_CHUNK):
        gt = jnp.dot(xb, wg_ref[:, c:c + FF_CHUNK], preferred_element_type=F32)
        up = jnp.dot(xb, wu_ref[:, c:c + FF_CHUNK], preferred_element_type=F32)
        h = (_silu(gt) * up).astype(BF16)
        d = jnp.dot(h, wd_ref[c:c + FF_CHUNK, :], preferred_element_type=F32)
        acc = d if acc is None else acc + d
    o_ref[...] = _layer_norm(ALPHA * x + acc, g_ref[...], b_ref[...])


def _ffn(x2, wg, wu, wd, g, b, tm):
    t = x2.shape[0]
    row = pl.BlockSpec((tm, D_MODEL), lambda i: (i, 0))
    consts = [wg, wu, wd, g, b]
    return pl.pallas_call(
        _ffn_kernel,
        grid=(t // tm,),
        in_specs=[row] + [_const_spec(c.shape) for c in consts],
        out_specs=row,
        out_shape=jax.ShapeDtypeStruct((t, D_MODEL), F32),
        compiler_params=_params("parallel"),
        name="ffn",
    )(x2, *consts)


MOE_TOKEN_TILE = 1024


def _moe_kernel(x_ref, wr_ref, br_ref, wg_ref, wu_ref, wd_ref, g_ref, b_ref, o_ref, gate_sc, acc_sc):
    e = pl.program_id(1)
    x = x_ref[...]
    xb = x.astype(BF16)
    lane = lax.broadcasted_iota(jnp.int32, (x.shape[0], LANES), 1)

    @pl.when(e == 0)
    def _():
        logits = jnp.dot(xb, wr_ref[...], preferred_element_type=F32) + br_ref[...]
        logits = jnp.where(lane < N_EXPERTS, logits, NEG)
        m1 = jnp.max(logits, axis=-1, keepdims=True)
        i1 = jnp.min(jnp.where(logits == m1, lane, LANES), axis=-1, keepdims=True)
        rest = jnp.where(lane == i1, NEG, logits)
        m2 = jnp.max(rest, axis=-1, keepdims=True)
        i2 = jnp.min(jnp.where(rest == m2, lane, LANES), axis=-1, keepdims=True)
        e2 = jnp.exp(m2 - m1)
        w1 = 1.0 / (1.0 + e2)
        w2 = e2 / (1.0 + e2)
        gate_sc[...] = jnp.where(lane == i1, w1, jnp.where(lane == i2, w2, 0.0))
        acc_sc[...] = jnp.zeros(acc_sc.shape, F32)

    gate_e = jnp.sum(jnp.where(lane == e, gate_sc[...], 0.0), axis=-1, keepdims=True)
    out_e = None
    for c in range(0, D_MODEL, FF_CHUNK):
        gt = jnp.dot(xb, wg_ref[0, :, c:c + FF_CHUNK], preferred_element_type=F32)
        up = jnp.dot(xb, wu_ref[0, :, c:c + FF_CHUNK], preferred_element_type=F32)
        h = (_silu(gt) * up).astype(BF16)
        d = jnp.dot(h, wd_ref[0, c:c + FF_CHUNK, :], preferred_element_type=F32)
        out_e = d if out_e is None else out_e + d
    acc_sc[...] += gate_e * out_e

    @pl.when(e == N_EXPERTS - 1)
    def _():
        o_ref[...] = _layer_norm(ALPHA * x + acc_sc[...], g_ref[...], b_ref[...])


def _moe(x2, wr, br, wg, wu, wd, g, b, tm):
    t = x2.shape[0]
    row = pl.BlockSpec((tm, D_MODEL), lambda i, e: (i, 0))
    wspec = pl.BlockSpec((1, D_MODEL, D_MODEL), lambda i, e: (e, 0, 0))
    return pl.pallas_call(
        _moe_kernel,
        grid=(t // tm, N_EXPERTS),
        in_specs=[row, _const_spec(wr.shape), _const_spec(br.shape), wspec, wspec, wspec,
                  _const_spec(g.shape), _const_spec(b.shape)],
        out_specs=row,
        out_shape=jax.ShapeDtypeStruct((t, D_MODEL), F32),
        scratch_shapes=[pltpu.VMEM((tm, LANES), F32), pltpu.VMEM((tm, D_MODEL), F32)],
        compiler_params=_params("parallel", "arbitrary"),
        name="moe",
    )(x2, wr, br, wg, wu, wd, g, b)


def _row(v, width=None):
    v = v.reshape(1, -1).astype(F32)
    if width is not None and v.shape[1] < width:
        v = jnp.pad(v, ((0, 0), (0, width - v.shape[1])))
    return v


def _layer_weights(p, l):
    w_in = p['w_in'][l]
    o = np.cumsum([N_BRANCH * D_MODEL, WIDTH, WIDTH, WIDTH, WIDTH, HEADS, WIDTH, CONV_DIM, HEADS])
    gates, u, q, k, v, fg, z, xbc, dtw = (w_in[:, a:b] for a, b in zip([0] + list(o[:-1]), o))
    w = {}
    w['w_main'] = jnp.concatenate([u, q, k, v, z, xbc], axis=1).astype(BF16)
    slab = ((0, 0), (0, 0), (0, LANES - HEAD_DIM))
    k_slab = jnp.pad(k.reshape(D_MODEL, HEADS, HEAD_DIM), slab).reshape(D_MODEL, HEADS * LANES)
    q_slab = jnp.pad(q.reshape(D_MODEL, HEADS, HEAD_DIM), slab).reshape(D_MODEL, HEADS * LANES)
    w['w_rows'] = jnp.concatenate([u, k_slab, z, xbc], axis=1).astype(BF16)
    w['w_cols'] = jnp.concatenate([q_slab, k, v], axis=1).T.astype(BF16)
    e3 = np.zeros((LANES, HEADS * LANES), np.float32)
    for piece in range(BIAS_LANES):
        for h in range(HEADS):
            e3[HEADS * piece + h, LANES * h + HEAD_DIM + piece] = 1.0
    w['bias_place'] = jnp.asarray(e3, BF16)
    w['w_small'] = jnp.pad(jnp.concatenate([fg, dtw], axis=1), ((0, 0), (0, LANES - 2 * HEADS))).astype(BF16)
    w['b_small'] = _row(jnp.concatenate([p['b_fgate'][l], p['ssd_dt_bias'][l]]), LANES)
    w['w_gates'] = gates.astype(BF16)
    w['b_gates'] = _row(p['b_gate'][l])
    for name in ('w_branch_s5', 'w_branch_fox', 'w_branch_ssd', 'w_o'):
        w[name] = p[name][l].astype(BF16)
    for name in ('ln1_g', 'ln1_b', 'ln2_g', 'ln2_b', 's5_d', 's5_b_glu', 'ssd_conv_b', 'ssd_norm_w'):
        w[name] = _row(p[name][l])
    w['s5_w_glu'] = p['s5_w_glu'][l].astype(BF16)
    w['ssd_conv_w'] = p['ssd_conv_w'][l].astype(F32)
    w['ssd_alog_row'] = _row(jnp.concatenate([jnp.zeros((HEADS,), F32), p['ssd_a_log'][l]]), LANES)
    w['ssd_alog_col'] = jnp.concatenate([jnp.zeros((HEADS,), F32), p['ssd_a_log'][l]]).reshape(2 * HEADS, 1)
    w['ssd_d'] = _row(jnp.repeat(p['ssd_d'][l], HEAD_DIM))
    e = np.zeros((LANES, WIDTH), np.float32)
    for h in range(HEADS):
        e[HEADS + h, HEAD_DIM * h:HEAD_DIM * (h + 1)] = 1.0
    w['ssd_expand'] = jnp.asarray(e, BF16)
    if l % 2 == 0:
        for name in ('ffn_w_gate', 'ffn_w_up', 'ffn_w_down'):
            w[name] = p[name][l // 2].astype(BF16)
    else:
        w['moe_w_router'] = jnp.pad(p['moe_w_router'][l // 2], ((0, 0), (0, LANES - N_EXPERTS))).astype(BF16)
        w['moe_b_router'] = _row(p['moe_b_router'][l // 2], LANES)
        for name in ('moe_w_gate', 'moe_w_up', 'moe_w_down'):
            w[name] = p[name][l // 2].astype(BF16)
    return w


def _s5_q(seq):
    return min(S5_CHUNK, seq)


def _run_trunk(x, p, weights, s5_tabs, s5_re0, s5_im0, conv0, ssd0, paged):
    bsz, seq, _ = x.shape
    t = bsz * seq
    tm_mix = min(512, t)
    fox_tile = min(512, seq)
    x2 = x.reshape(t, D_MODEL)
    new = [[] for _ in range(7)]
    kv_all = tuple(jnp.zeros((len(weights), bsz, WIDTH, seq), F32) for _ in range(2)) if paged is None else ()
    q5 = _s5_q(seq)
    ssd0 = ssd0.reshape(ssd0.shape[0], bsz, WIDTH, SSD_STATE)
    if paged is not None:
        cache_k, cache_v, cache_lf, page_table = paged
        cache_kt = cache_k.transpose(0, 1, 3, 4, 2)
        cache_vt = cache_v.transpose(0, 1, 3, 4, 2)
        cache_lft = cache_lf.transpose(0, 1, 3, 2)
    for l in range(len(weights)):
        w = weights[l]
        if paged is None:
            u, ub, z, xbc, small, smt3, kaug, qaugt, kt_all, vt_all, tot = _in_proj_prompt(
                x2.reshape(bsz, seq, D_MODEL), w['w_rows'], w['w_cols'], w['w_small'], w['b_small'],
                w['bias_place'], fox_tile, l, len(weights), kv_all)
            kv_all = (kt_all, vt_all)
            k_out = v_out = None
            logf_out = smt3[:, :HEADS, :].transpose(0, 2, 1)
        else:
            u, ub, qb, k, v, z, xbc, small, small_t, logf = _in_proj_sample(
                x2, w['w_main'], w['w_small'], w['b_small'], min(256, t))
            k_out = k.reshape(bsz, seq, HEADS, HEAD_DIM)
            v_out = v.reshape(bsz, seq, HEADS, HEAD_DIM)
            logf_out = logf.reshape(bsz, seq, HEADS)
            smt3 = small_t.reshape(2 * HEADS, bsz, seq).transpose(1, 0, 2)

        tab = s5_tabs[q5][l]
        yi, cs_re, cs_im = _s5_chunks(ub.reshape(t // q5, q5 * WIDTH), tab, q5, min(128, t // q5))
        d_row, w_glu, b_glu = w['s5_d'], w['s5_w_glu'], w['s5_b_glu']
        if paged is None:
            y_s5, s5_re, s5_im = _s5_scan(u.reshape(bsz, seq, WIDTH), yi, cs_re, cs_im,
                                          s5_re0[l].reshape(bsz, 1, S5_LANES), s5_im0[l].reshape(bsz, 1, S5_LANES),
                                          tab, d_row, w_glu, b_glu, q5, min(256, seq), True)
        else:
            y_s5, s5_re, s5_im = _s5_scan(u.reshape(1, t, WIDTH), yi, cs_re, cs_im,
                                          s5_re0[l].reshape(bsz, S5_LANES), s5_im0[l].reshape(bsz, S5_LANES),
                                          tab, d_row, w_glu, b_glu, q5, min(256, t), False)
        y_s5 = y_s5.reshape(t, WIDTH)
        s5_re = s5_re.reshape(bsz, S5_GROUPS, S5_STATE)
        s5_im = s5_im.reshape(bsz, S5_GROUPS, S5_STATE)

        if paged is None:
            y_fox = _fox_prompt(qaugt, kaug, vt_all, l, tot, fox_tile)
        else:
            lfn_t = jnp.pad(logf.reshape(bsz, seq, HEADS).transpose(0, 2, 1),
                            ((0, 0), (0, 0), (0, LANES - seq)))
            y_fox = _fox_sample(qb.astype(F32).reshape(bsz, seq, WIDTH), k.reshape(bsz, seq, WIDTH),
                                v.reshape(bsz, seq, WIDTH), lfn_t, cache_kt, cache_vt, cache_lft, l, page_table)
        y_fox = y_fox.reshape(t, WIDTH)

        nv = math.gcd(seq, SSD_CHUNK)
        q_ssd = max(nv, SSD_MIN_ROWS)
        sm3 = small.reshape(bsz, seq, LANES)
        if nv < q_ssd:
            sm3 = jnp.pad(sm3, ((0, 0), (0, q_ssd - nv), (0, 0)))
            smt3 = jnp.pad(smt3, ((0, 0), (0, 0), (0, q_ssd - nv)))
        y_ssd, conv_new, ssd_new = _ssd(
            z.reshape(bsz, seq, WIDTH), xbc.reshape(bsz, seq, CONV_DIM), sm3, smt3, conv0[l],
            ssd0, l, w['ssd_conv_w'], w['ssd_conv_b'], w['ssd_alog_row'],
            w['ssd_alog_col'], w['ssd_expand'], w['ssd_d'], w['ssd_norm_w'], nv, q_ssd)
        y_ssd = y_ssd.reshape(t, WIDTH)
        ssd_new = ssd_new.reshape(bsz, HEADS, HEAD_DIM, SSD_STATE)

        x2 = _merge(x2, y_s5, y_fox, y_ssd, w['w_gates'], w['b_gates'], w['w_branch_s5'],
                    w['w_branch_fox'], w['w_branch_ssd'], w['w_o'], w['ln1_g'], w['ln1_b'], tm_mix)
        if l % 2 == 0:
            x2 = _ffn(x2, w['ffn_w_gate'], w['ffn_w_up'], w['ffn_w_down'], w['ln2_g'], w['ln2_b'], tm_mix)
        else:
            x2 = _moe(x2, w['moe_w_router'], w['moe_b_router'], w['moe_w_gate'], w['moe_w_up'],
                      w['moe_w_down'], w['ln2_g'], w['ln2_b'], min(MOE_TOKEN_TILE, t))

        st = (k_out, v_out, logf_out, s5_re, s5_im, conv_new, ssd_new)
        for lst, s in zip(new, st):
            lst.append(s)
    if paged is None:
        new[0], new[1] = ([a.reshape(len(weights), bsz, HEADS, HEAD_DIM, seq).transpose(0, 1, 4, 2, 3)]
                          for a in kv_all)
        stacked = [new[0][0], new[1][0]] + [jnp.stack(s) for s in new[2:]]
    else:
        stacked = [jnp.stack(s) for s in new]
    return x2.reshape(bsz, seq, D_MODEL), stacked


def kernel(x_prompt, x_sample, cache_k, cache_v, cache_logf, page_table, state_s5_re, state_s5_im,
           state_conv, state_ssd, w_in, b_gate, b_fgate, s5_lam_re, s5_lam_im, s5_log_dt, s5_b_re,
           s5_b_im, s5_c_re, s5_c_im, s5_d, s5_w_glu, s5_b_glu, ssd_conv_w, ssd_conv_b, ssd_dt_bias,
           ssd_a_log, ssd_d, ssd_norm_w, w_branch_s5, w_branch_fox, w_branch_ssd, w_o, ln1_g, ln1_b,
           ln2_g, ln2_b, ffn_w_gate, ffn_w_up, ffn_w_down, moe_w_router, moe_b_router, moe_w_gate,
           moe_w_up, moe_w_down):
    p = dict(w_in=w_in, b_gate=b_gate, b_fgate=b_fgate, s5_d=s5_d, s5_w_glu=s5_w_glu, s5_b_glu=s5_b_glu,
             ssd_conv_w=ssd_conv_w, ssd_conv_b=ssd_conv_b, ssd_dt_bias=ssd_dt_bias, ssd_a_log=ssd_a_log,
             ssd_d=ssd_d, ssd_norm_w=ssd_norm_w, w_branch_s5=w_branch_s5, w_branch_fox=w_branch_fox,
             w_branch_ssd=w_branch_ssd, w_o=w_o, ln1_g=ln1_g, ln1_b=ln1_b, ln2_g=ln2_g, ln2_b=ln2_b,
             ffn_w_gate=ffn_w_gate, ffn_w_up=ffn_w_up, ffn_w_down=ffn_w_down, moe_w_router=moe_w_router,
             moe_b_router=moe_b_router, moe_w_gate=moe_w_gate, moe_w_up=moe_w_up, moe_w_down=moe_w_down)
    depth = w_in.shape[0]
    weights = [_layer_weights(p, l) for l in range(depth)]
    qs = {_s5_q(x_prompt.shape[1]), _s5_q(x_sample.shape[1])}
    s5_tabs = _all_s5_tables(s5_lam_re, s5_lam_im, s5_log_dt, s5_b_re, s5_b_im, s5_c_re, s5_c_im, qs)
    bsz = x_prompt.shape[0]
    z_s5 = jnp.zeros((depth, bsz, S5_GROUPS, S5_STATE), F32)
    z_conv = jnp.zeros((depth, bsz, SSD_CONV - 1, CONV_DIM), F32)
    z_ssd = jnp.zeros((depth, bsz, HEADS, HEAD_DIM, SSD_STATE), F32)
    y_p, st_p = _run_trunk(x_prompt, p, weights, s5_tabs, z_s5, z_s5, z_conv, z_ssd, None)
    y_s, st_s = _run_trunk(x_sample, p, weights, s5_tabs, state_s5_re, state_s5_im, state_conv, state_ssd,
                           (cache_k, cache_v, cache_logf, page_table))
    return (y_p, y_s, *st_p, *st_s)
```

```python
import functools
import math

import jax
import jax.numpy as jnp
import numpy as np
from jax import lax
from jax.experimental import pallas as pl
from jax.experimental.pallas import tpu as pltpu

F32 = jnp.float32
BF16 = jnp.bfloat16
HIGHEST = lax.Precision.HIGHEST

D_MODEL = 1024
N_BRANCH = 3
WIDTH = 512
S5_GROUPS = 32
S5_GROUP = 16
S5_STATE = 64
S5_LANES = S5_GROUPS * S5_STATE
S5_CHUNK = 16
HEADS = 8
HEAD_DIM = 64
SSD_STATE = 128
SSD_GROUPS = 2
SSD_CONV = 4
SSD_CHUNK = 128
SSD_MIN_ROWS = 16
SSD_SEQS_PER_STEP = 8
CONV_DIM = 1024
D_FF = 2816
N_EXPERTS = 8
PAGE = 128
ALPHA = 4.0 ** 0.25
LN_EPS = 1e-5
RMS_EPS = 1e-5
NEG = -1e30
LANES = 128
VMEM_LIMIT = 56 * 1024 * 1024

NT_DIMS = (((1,), (1,)), ((), ()))


def _params(*sem):
    return pltpu.CompilerParams(dimension_semantics=sem, vmem_limit_bytes=VMEM_LIMIT)


def _const_spec(shape):
    nd = len(shape)
    return pl.BlockSpec(shape, lambda *_: (0,) * nd, pipeline_mode=pl.Buffered(1))


def _layer_norm(x, g, b):
    mu = jnp.mean(x, axis=-1, keepdims=True)
    xc = x - mu
    var = jnp.mean(xc * xc, axis=-1, keepdims=True)
    return xc * lax.rsqrt(var + LN_EPS) * g + b


def _sigmoid(x):
    return 1.0 / (1.0 + jnp.exp(-x))


def _silu(x):
    return x * _sigmoid(x)


def _bf16_pieces(x):
    hi = x.astype(BF16)
    r1 = x - hi.astype(F32)
    mid = r1.astype(BF16)
    lo = (r1 - mid.astype(F32)).astype(BF16)
    return hi, mid, lo


def _small_act(xb, ws_ref, bs_ref):
    s = jnp.dot(xb, ws_ref[...], preferred_element_type=F32) + bs_ref[...]
    t = jnp.log1p(jnp.exp(-jnp.abs(s)))
    lane = lax.broadcasted_iota(jnp.int32, s.shape, 1)
    return jnp.where(lane < HEADS, jnp.minimum(s, 0.0) - t,
                     jnp.where(lane < 2 * HEADS, jnp.maximum(s, 0.0) + t, 0.0))


SEG_U, SEG_Q, SEG_K, SEG_V, SEG_Z, SEG_X, SEG_END = 0, 512, 1024, 1536, 2048, 2560, 3584


def _in_proj_sample_kernel(x_ref, wm_ref, ws_ref, bs_ref, u_ref, ub_ref, qb_ref, k_ref, v_ref,
                           z_ref, xbc_ref, small_ref, smallt_ref, logf_ref):
    xb = x_ref[...].astype(BF16)

    def seg(a, b):
        return jnp.dot(xb, wm_ref[:, a:b], preferred_element_type=F32)

    u = seg(SEG_U, SEG_Q)
    u_ref[...] = u
    ub_ref[...] = u.astype(BF16)
    qb_ref[...] = (seg(SEG_Q, SEG_K) * (HEAD_DIM ** -0.5)).astype(BF16)
    k_ref[...] = seg(SEG_K, SEG_V)
    v_ref[...] = seg(SEG_V, SEG_Z)
    z_ref[...] = seg(SEG_Z, SEG_X)
    xbc_ref[...] = seg(SEG_X, SEG_END)
    sm = _small_act(xb, ws_ref, bs_ref)
    small_ref[...] = sm
    smallt_ref[...] = sm.T[:2 * HEADS, :]
    logf_ref[...] = sm[:, :HEADS]


def _in_proj_sample(x2, wm, ws, bs, tm):
    t = x2.shape[0]
    row = lambda w: pl.BlockSpec((tm, w), lambda i: (i, 0))
    out_shape = (
        jax.ShapeDtypeStruct((t, WIDTH), F32),
        jax.ShapeDtypeStruct((t, WIDTH), BF16),
        jax.ShapeDtypeStruct((t, WIDTH), BF16),
        jax.ShapeDtypeStruct((t, WIDTH), F32),
        jax.ShapeDtypeStruct((t, WIDTH), F32),
        jax.ShapeDtypeStruct((t, WIDTH), F32),
        jax.ShapeDtypeStruct((t, CONV_DIM), F32),
        jax.ShapeDtypeStruct((t, LANES), F32),
        jax.ShapeDtypeStruct((2 * HEADS, t), F32),
        jax.ShapeDtypeStruct((t, HEADS), F32),
    )
    out_specs = (row(WIDTH), row(WIDTH), row(WIDTH), row(WIDTH), row(WIDTH),
                 row(WIDTH), row(CONV_DIM), row(LANES),
                 pl.BlockSpec((2 * HEADS, tm), lambda i: (0, i)), row(HEADS))
    return pl.pallas_call(
        _in_proj_sample_kernel,
        grid=(t // tm,),
        in_specs=[row(D_MODEL), _const_spec(wm.shape), _const_spec(ws.shape), _const_spec(bs.shape)],
        out_specs=out_specs,
        out_shape=out_shape,
        compiler_params=_params("parallel"),
        name="in_proj_sample",
    )(x2, wm, ws, bs)


PSEG_U, PSEG_K, PSEG_Z, PSEG_X, PSEG_END = 0, 512, 1536, 2048, 3072
TSEG_Q, TSEG_K, TSEG_V, TSEG_END = 0, 1024, 1536, 2048
BIAS_LANES = 3


def _in_proj_prompt_kernel(x_ref, wr_ref, wt_ref, ws_ref, bs_ref, e_ref, *rest):
    (u_ref, ub_ref, z_ref, xbc_ref, small_ref, smallt_ref, kaug_ref, qaugt_ref, kt_ref, vt_ref, tot_ref,
     u_sc) = rest[-12:]
    tm = x_ref.shape[0]
    xb = x_ref[...].astype(BF16)
    u = jnp.dot(xb, wr_ref[:, PSEG_U:PSEG_K], preferred_element_type=F32)
    u_ref[...] = u
    for k in range(WIDTH // LANES):
        u_sc[k] = u[:, LANES * k:LANES * (k + 1)]
    for s in range(S5_CHUNK):
        for k in range(WIDTH // LANES):
            lo_lane = s * WIDTH + LANES * k
            ub_ref[:, lo_lane:lo_lane + LANES] = u_sc[k, pl.ds(s, tm // S5_CHUNK, stride=S5_CHUNK), :].astype(BF16)
    z_ref[...] = jnp.dot(xb, wr_ref[:, PSEG_Z:PSEG_X], preferred_element_type=F32)
    xbc_ref[...] = jnp.dot(xb, wr_ref[:, PSEG_X:PSEG_END], preferred_element_type=F32)
    sm = _small_act(xb, ws_ref, bs_ref)
    small_ref[...] = sm
    smt = sm.T[:2 * HEADS, :]
    smallt_ref[0] = smt
    r_i = lax.broadcasted_iota(jnp.int32, (tm, tm), 0)
    c_i = lax.broadcasted_iota(jnp.int32, (tm, tm), 1)
    tri = (r_i >= c_i).astype(BF16)
    ones = jnp.ones((tm, tm), BF16)
    cloc = None
    tot = None
    for piece, piece_t in zip(_bf16_pieces(sm), _bf16_pieces(smt)):
        d = jnp.dot(tri, piece, preferred_element_type=F32)
        dt = jnp.dot(piece_t, ones, preferred_element_type=F32)
        cloc = d if cloc is None else cloc + d
        tot = dt if tot is None else tot + dt
    tot_ref[0, 0] = tot
    lane = lax.broadcasted_iota(jnp.int32, cloc.shape, 1)
    hi, mid, lo = _bf16_pieces(jnp.where(lane < HEADS, -cloc, 0.0))
    pieces = (hi.astype(F32) + pltpu.roll(mid.astype(F32), HEADS, axis=1)
              + pltpu.roll(lo.astype(F32), 2 * HEADS, axis=1)).astype(BF16)
    k_slabs = (jnp.dot(xb, wr_ref[:, PSEG_K:PSEG_Z], preferred_element_type=F32)
               + jnp.dot(pieces, e_ref[...], preferred_element_type=F32)).astype(BF16)
    for h in range(HEADS):
        kaug_ref[0, h] = k_slabs[:, LANES * h:LANES * (h + 1)]
    tall = lax.dot_general(wt_ref[...], xb, NT_DIMS, preferred_element_type=F32)
    row = lax.broadcasted_iota(jnp.int32, (LANES, tm), 0)
    ones_rows = (row >= HEAD_DIM) & (row < HEAD_DIM + BIAS_LANES)
    for h in range(HEADS):
        qh = tall[TSEG_Q + LANES * h:TSEG_Q + LANES * (h + 1), :]
        qaugt_ref[0, h] = jnp.where(ones_rows, 1.0, qh * (HEAD_DIM ** -0.5)).astype(BF16)
    kt_ref[0, 0] = tall[TSEG_K:TSEG_V, :]
    vt_ref[0, 0] = tall[TSEG_V:TSEG_END, :]


def _in_proj_prompt(x3, wr, wt, ws, bs, e3, tm, layer, depth, kv_prev):
    bsz, seq, _ = x3.shape
    nt = seq // tm
    x2 = x3.reshape(bsz * seq, D_MODEL)
    row = lambda w: pl.BlockSpec((tm, w), lambda b, i: (b * nt + i, 0))
    colt = lambda r: pl.BlockSpec((1, r, tm), lambda b, i: (b, 0, i))
    t = bsz * seq
    out_shape = (
        jax.ShapeDtypeStruct((t, WIDTH), F32),
        jax.ShapeDtypeStruct((t // S5_CHUNK, S5_CHUNK * WIDTH), BF16),
        jax.ShapeDtypeStruct((t, WIDTH), F32),
        jax.ShapeDtypeStruct((t, CONV_DIM), F32),
        jax.ShapeDtypeStruct((t, LANES), F32),
        jax.ShapeDtypeStruct((bsz, 2 * HEADS, seq), F32),
        jax.ShapeDtypeStruct((bsz, HEADS, seq, LANES), BF16),
        jax.ShapeDtypeStruct((bsz, HEADS, LANES, seq), BF16),
        jax.ShapeDtypeStruct((depth, bsz, WIDTH, seq), F32),
        jax.ShapeDtypeStruct((depth, bsz, WIDTH, seq), F32),
        jax.ShapeDtypeStruct((bsz, nt, 2 * HEADS, tm), F32),
    )
    layer_colt = pl.BlockSpec((1, 1, WIDTH, tm), lambda b, i: (layer, b, 0, i))
    out_specs = (row(WIDTH),
                 pl.BlockSpec((tm // S5_CHUNK, S5_CHUNK * WIDTH), lambda b, i: (b * nt + i, 0)),
                 row(WIDTH), row(CONV_DIM), row(LANES), colt(2 * HEADS),
                 pl.BlockSpec((1, HEADS, tm, LANES), lambda b, i: (b, 0, i, 0)),
                 pl.BlockSpec((1, HEADS, LANES, tm), lambda b, i: (b, 0, 0, i)),
                 layer_colt, layer_colt,
                 pl.BlockSpec((1, 1, 2 * HEADS, tm), lambda b, i: (b, i, 0, 0)))
    consts = [wr, wt, ws, bs, e3]
    n_in = 1 + len(consts)
    kv_prev = list(kv_prev)
    aliases = {n_in + a: 8 + a for a in range(len(kv_prev))}
    return pl.pallas_call(
        _in_proj_prompt_kernel,
        grid=(bsz, nt),
        in_specs=[row(D_MODEL)] + [_const_spec(c.shape) for c in consts]
                 + [pl.BlockSpec(memory_space=pl.ANY) for _ in kv_prev],
        out_specs=out_specs,
        out_shape=out_shape,
        input_output_aliases=aliases,
        scratch_shapes=[pltpu.VMEM((WIDTH // LANES, tm, LANES), F32)],
        compiler_params=_params("parallel", "parallel"),
        name="in_proj_prompt",
    )(x2, *consts, *kv_prev)


def _cumsum_lanes(x):
    n = x.shape[-1]
    lane = lax.broadcasted_iota(jnp.int32, x.shape, x.ndim - 1)
    s = 1
    while s < n:
        x = x + jnp.where(lane >= s, pltpu.roll(x, s, axis=x.ndim - 1), 0.0)
        s *= 2
    return x


FOX_HEADS_PER_STEP = 8


def _fox_prompt_kernel(it_ref, jt_ref, q_ref, k_ref, v_ref, tot_ref, o_ref, m_sc, l_sc, c_sc, acc_sc, *, tq, tk):
    g = pl.program_id(1)
    t = pl.program_id(2)
    i = it_ref[t]
    j = jt_ref[t]
    hps = FOX_HEADS_PER_STEP

    @pl.when(j == 0)
    def _():
        m_sc[...] = jnp.full(m_sc.shape, NEG, F32)
        l_sc[...] = jnp.zeros(l_sc.shape, F32)
        c_sc[...] = jnp.zeros(c_sc.shape, F32)
        acc_sc[...] = jnp.zeros(acc_sc.shape, F32)

    def step(masked):
        if masked:
            kpos = j * tk + lax.broadcasted_iota(jnp.int32, (tk, tq), 0)
            qpos = i * tq + lax.broadcasted_iota(jnp.int32, (tk, tq), 1)
            causal = kpos <= qpos
        for h in range(hps):
            s = jnp.dot(k_ref[0, h], q_ref[0, h], preferred_element_type=F32)
            if masked:
                s = jnp.where(causal, s, NEG)
            c_j = c_sc[h]
            m_prev = m_sc[h]
            m_new = jnp.maximum(m_prev, jnp.max(s, axis=0, keepdims=True) - c_j)
            alpha = jnp.exp(m_prev - m_new)
            p = jnp.exp(s - (m_new + c_j))
            l_sc[h] = alpha * l_sc[h] + jnp.sum(p, axis=0, keepdims=True)
            rows = slice(HEAD_DIM * h, HEAD_DIM * (h + 1))
            acc_sc[rows, :] = alpha * acc_sc[rows, :] + jnp.dot(
                v_ref[0, rows, :].astype(BF16), p.astype(BF16), preferred_element_type=F32)
            m_sc[h] = m_new
            c_sc[h] = c_j + tot_ref[0, 0, pl.ds(g * hps + h, 1), :]

    @pl.when(j < i)
    def _():
        step(False)

    @pl.when(j == i)
    def _():
        step(True)
        inv = jnp.concatenate([jnp.broadcast_to(1.0 / l_sc[h], (HEAD_DIM, tq)) for h in range(hps)], axis=0)
        o_ref[0] = (acc_sc[...] * inv).T.astype(o_ref.dtype)


def _fox_prompt(qaugt, kaug, vt, layer, tot, tq):
    bsz, _, _, seq = qaugt.shape
    nq = seq // tq
    hps = FOX_HEADS_PER_STEP
    it = np.array([i for i in range(nq) for j in range(i + 1)], np.int32)
    jt = np.array([j for i in range(nq) for j in range(i + 1)], np.int32)
    grid_spec = pltpu.PrefetchScalarGridSpec(
        num_scalar_prefetch=2,
        grid=(bsz, HEADS // hps, len(it)),
        in_specs=[
            pl.BlockSpec((1, hps, LANES, tq), lambda b, g, t, it, jt: (b, g, 0, it[t])),
            pl.BlockSpec((1, hps, tq, LANES), lambda b, g, t, it, jt: (b, g, jt[t], 0)),
            pl.BlockSpec((None, 1, hps * HEAD_DIM, tq), lambda b, g, t, it, jt: (layer, b, g, jt[t])),
            pl.BlockSpec((1, 1, 2 * HEADS, tq), lambda b, g, t, it, jt: (b, jt[t], 0, 0)),
        ],
        out_specs=pl.BlockSpec((1, tq, hps * HEAD_DIM), lambda b, g, t, it, jt: (b, it[t], g)),
        scratch_shapes=[pltpu.VMEM((hps, 1, tq), F32), pltpu.VMEM((hps, 1, tq), F32),
                        pltpu.VMEM((hps, 1, tq), F32), pltpu.VMEM((hps * HEAD_DIM, tq), F32)],
    )
    return pl.pallas_call(
        functools.partial(_fox_prompt_kernel, tq=tq, tk=tq),
        grid_spec=grid_spec,
        out_shape=jax.ShapeDtypeStruct((bsz, seq, WIDTH), BF16),
        compiler_params=_params("parallel", "parallel", "arbitrary"),
        name="fox_prompt",
    )(jnp.asarray(it), jnp.asarray(jt), qaugt, kaug, vt, tot)


FOX_SEQS_PER_STEP = 1


def _fox_sample_kernel(pt_ref, q_ref, kn_ref, vn_ref, lfn_ref, *refs, n_pages, tq, group):
    lf_ref = refs[2 * group * n_pages]
    o_ref = refs[2 * group * n_pages + 1]
    for s in range(group):
        one = pl.ds(s, 1)
        _fox_sample_sequence(pt_ref, pl.program_id(0) * group + s, q_ref.at[one], kn_ref.at[one],
                             vn_ref.at[one], lfn_ref.at[one], refs[s * n_pages:(s + 1) * n_pages],
                             refs[(group + s) * n_pages:(group + s + 1) * n_pages], lf_ref, o_ref.at[one],
                             n_pages=n_pages, tq=tq)


def _fox_sample_sequence(pt_ref, b, q_ref, kn_ref, vn_ref, lfn_ref, k_refs, v_refs, lf_ref, o_ref, *, n_pages, tq):
    rows = tq * HEADS
    q = q_ref[0].astype(F32)
    qe = jnp.concatenate([jnp.broadcast_to(q[t:t + 1, :], (HEADS, WIDTH)) for t in range(tq)],
                         axis=0)
    r_w = lax.broadcasted_iota(jnp.int32, (rows, WIDTH), 0)
    l_w = lax.broadcasted_iota(jnp.int32, (rows, WIDTH), 1)
    head_mask = (l_w // HEAD_DIM) == (r_w % HEADS)
    qbd = jnp.where(head_mask, qe, 0.0).astype(BF16)

    scores = [None] * n_pages
    carry = jnp.zeros((HEADS, 1), F32)
    for p in range(n_pages - 1, -1, -1):
        cs = _cumsum_lanes(lf_ref[0, pt_ref[b * n_pages + p]])
        tot = cs[:, PAGE - 1:PAGE]
        dec = (carry + tot) - cs
        carry = carry + tot
        kt = k_refs[p][0, 0].reshape(WIDTH, PAGE).astype(BF16)
        s = jnp.dot(qbd, kt, preferred_element_type=F32)
        scores[p] = s + jnp.concatenate([dec] * tq, axis=0)
    pad = jnp.zeros((PAGE - tq, WIDTH), F32)
    kn = jnp.concatenate([kn_ref[0], pad], axis=0).astype(BF16)
    vn = jnp.concatenate([vn_ref[0], pad], axis=0).astype(BF16)
    cn = _cumsum_lanes(lfn_ref[0])
    r_p = lax.broadcasted_iota(jnp.int32, (rows, PAGE), 0)
    c_p = lax.broadcasted_iota(jnp.int32, (rows, PAGE), 1)
    s_new = lax.dot_general(qbd, kn, NT_DIMS, preferred_element_type=F32)
    s_new = jnp.where(c_p <= r_p // HEADS, s_new - jnp.concatenate([cn] * tq, axis=0), NEG)

    m = s_new
    for p in range(n_pages):
        m = jnp.maximum(m, scores[p])
    m = jnp.max(m, axis=-1, keepdims=True)
    pn = jnp.exp(s_new - m)
    l = jnp.sum(pn, axis=-1, keepdims=True)
    o = jnp.dot(pn.astype(BF16), vn, preferred_element_type=F32)
    for p in range(n_pages):
        pp = jnp.exp(scores[p] - m)
        l = l + jnp.sum(pp, axis=-1, keepdims=True)
        vt = v_refs[p][0, 0].reshape(WIDTH, PAGE).astype(BF16)
        o = o + lax.dot_general(pp.astype(BF16), vt, NT_DIMS, preferred_element_type=F32)
    om = jnp.where(head_mask, o / l, 0.0)
    out = jnp.concatenate([jnp.sum(om[HEADS * t:HEADS * (t + 1), :], axis=0, keepdims=True) for t in range(tq)],
                          axis=0)
    o_ref[0] = out.astype(o_ref.dtype)


def _fox_sample(qb, k_new, v_new, lfn_t, cache_kt, cache_vt, cache_lft, layer, page_table):
    dbsz, tq, _ = qb.shape
    n_pages = page_table.shape[1]
    pt = page_table.reshape(-1).astype(jnp.int32)

    g = FOX_SEQS_PER_STEP if dbsz % FOX_SEQS_PER_STEP == 0 else 1

    def page_spec(s, p):
        return pl.BlockSpec((1, 1, HEADS, HEAD_DIM, PAGE),
                            lambda b, pt: (layer, pt[(b * g + s) * n_pages + p], 0, 0, 0))

    seq_spec = lambda shape: pl.BlockSpec(shape, lambda b, pt: (b, 0, 0))
    in_specs = [seq_spec((g, tq, WIDTH)), seq_spec((g, tq, WIDTH)), seq_spec((g, tq, WIDTH)),
                seq_spec((g, HEADS, LANES))]
    in_specs += [page_spec(s, p) for s in range(g) for p in range(n_pages)]
    in_specs += [page_spec(s, p) for s in range(g) for p in range(n_pages)]
    in_specs += [pl.BlockSpec((1,) + cache_lft.shape[1:], lambda b, pt: (layer, 0, 0, 0),
                              pipeline_mode=pl.Buffered(1))]
    grid_spec = pltpu.PrefetchScalarGridSpec(
        num_scalar_prefetch=1, grid=(dbsz // g,), in_specs=in_specs,
        out_specs=pl.BlockSpec((g, tq, WIDTH), lambda b, pt: (b, 0, 0)))
    return pl.pallas_call(
        functools.partial(_fox_sample_kernel, n_pages=n_pages, tq=tq, group=g),
        grid_spec=grid_spec,
        out_shape=jax.ShapeDtypeStruct((dbsz, tq, WIDTH), F32),
        compiler_params=_params("parallel"),
        name="fox_sample",
    )(pt, qb, k_new, v_new, lfn_t, *([cache_kt] * (g * n_pages)), *([cache_vt] * (g * n_pages)), cache_lft)


FOX_RING = 3


def _fox_sample_ring_kernel(pt_ref, q_ref, kn_ref, vn_ref, lfn_ref, kc_ref, vc_ref, lf_ref, o_ref,
                            kbuf, vbuf, sem, *, n_pages, tq, layer, n_seq):
    b = pl.program_id(0)

    def copies(seq, slot):
        out = []
        for p in range(n_pages):
            page = pt_ref[seq * n_pages + p]
            out.append(pltpu.make_async_copy(kc_ref.at[layer, page], kbuf.at[slot, p], sem.at[0, slot]))
            out.append(pltpu.make_async_copy(vc_ref.at[layer, page], vbuf.at[slot, p], sem.at[1, slot]))
        return out

    @pl.when(b == 0)
    def _():
        for s in range(min(FOX_RING - 1, n_seq)):
            for c in copies(s, s):
                c.start()

    ahead = b + (FOX_RING - 1)

    @pl.when(ahead < n_seq)
    def _():
        for c in copies(ahead, ahead % FOX_RING):
            c.start()

    slot = b % FOX_RING
    for c in copies(b, slot):
        c.wait()
    one = pl.ds(slot, 1)
    k_views = [kbuf.at[one, pl.ds(p, 1)] for p in range(n_pages)]
    v_views = [vbuf.at[one, pl.ds(p, 1)] for p in range(n_pages)]
    _fox_sample_sequence(pt_ref, b, q_ref, kn_ref, vn_ref, lfn_ref, k_views, v_views, lf_ref, o_ref,
                         n_pages=n_pages, tq=tq)


def _fox_sample_ring(qb, k_new, v_new, lfn_t, cache_kt, cache_vt, cache_lft, layer, page_table):
    dbsz, tq, _ = qb.shape
    n_pages = page_table.shape[1]
    pt = page_table.reshape(-1).astype(jnp.int32)
    seq_spec = lambda shape: pl.BlockSpec(shape, lambda b, pt: (b, 0, 0))
    in_specs = [seq_spec((1, tq, WIDTH)), seq_spec((1, tq, WIDTH)), seq_spec((1, tq, WIDTH)),
                seq_spec((1, HEADS, LANES)),
                pl.BlockSpec(memory_space=pl.ANY), pl.BlockSpec(memory_space=pl.ANY),
                pl.BlockSpec((1,) + cache_lft.shape[1:], lambda b, pt: (layer, 0, 0, 0),
                             pipeline_mode=pl.Buffered(1))]
    page_buf = pltpu.VMEM((FOX_RING, n_pages, HEADS, HEAD_DIM, PAGE), F32)
    grid_spec = pltpu.PrefetchScalarGridSpec(
        num_scalar_prefetch=1, grid=(dbsz,), in_specs=in_specs,
        out_specs=pl.BlockSpec((1, tq, WIDTH), lambda b, pt: (b, 0, 0)),
        scratch_shapes=[page_buf, page_buf, pltpu.SemaphoreType.DMA((2, FOX_RING))])
    return pl.pallas_call(
        functools.partial(_fox_sample_ring_kernel, n_pages=n_pages, tq=tq, layer=layer, n_seq=dbsz),
        grid_spec=grid_spec,
        out_shape=jax.ShapeDtypeStruct((dbsz, tq, WIDTH), F32),
        compiler_params=_params("arbitrary"),
        name="fox_sample_ring",
    )(pt, qb, k_new, v_new, lfn_t, cache_kt, cache_vt, cache_lft)


def _s5_tables(lam_re, lam_im, log_dt, b_re, b_im, c_re, c_im, q):
    hp = dict(precision=HIGHEST)
    dt = jnp.exp(log_dt)[:, None]
    mag = jnp.exp(lam_re * dt)
    ab_re = mag * jnp.cos(lam_im * dt)
    ab_im = mag * jnp.sin(lam_im * dt)
    nr = ab_re - 1.0
    den = lam_re * lam_re + lam_im * lam_im
    q_re = (nr * lam_re + ab_im * lam_im) / den
    q_im = (ab_im * lam_re - nr * lam_im) / den
    bb_re = q_re[..., None] * b_re - q_im[..., None] * b_im
    bb_im = q_re[..., None] * b_im + q_im[..., None] * b_re
    jj = jnp.arange(q + 1, dtype=F32)[:, None, None]
    pmag = jnp.exp(lam_re * dt * jj)
    p_re = pmag * jnp.cos(lam_im * dt * jj)
    p_im = pmag * jnp.sin(lam_im * dt * jj)
    ab_b_re = p_re[:q, :, :, None] * bb_re - p_im[:q, :, :, None] * bb_im
    ab_b_im = p_re[:q, :, :, None] * bb_im + p_im[:q, :, :, None] * bb_re
    kern = (jnp.einsum('gcn,jgnd->jgcd', c_re, ab_b_re, **hp)
            - jnp.einsum('gcn,jgnd->jgcd', c_im, ab_b_im, **hp))
    kt = kern.transpose(0, 1, 3, 2).reshape(q, 2, 16 * S5_GROUP, S5_GROUP)
    tile_cols = jnp.tile(jnp.eye(S5_GROUP, dtype=F32), (1, 16))
    r_g = lax.broadcasted_iota(jnp.int32, (256, 256), 0) // S5_GROUP
    c_g = lax.broadcasted_iota(jnp.int32, (256, 256), 1) // S5_GROUP
    w_toep = jnp.where(r_g == c_g, jnp.einsum('jxrb,bc->jxrc', kt, tile_cols, **hp), 0.0).astype(BF16)
    eye_g = jnp.eye(S5_GROUPS, dtype=F32)

    def in_map(bb):
        full = jnp.einsum('gnc,gh->gchn', bb, eye_g).reshape(WIDTH, S5_LANES)
        return jnp.stack([full[128 * (n // 2):128 * (n // 2) + 128, 256 * n:256 * n + 256]
                          for n in range(8)]).astype(BF16)

    def out_map(c):
        full = jnp.einsum('gcn,gh->gnhc', c, eye_g).reshape(S5_LANES, WIDTH)
        return jnp.stack([full[512 * m:512 * m + 512, 128 * m:128 * m + 128]
                          for m in range(4)]).astype(BF16)

    rows = ((q + 1 + 7) // 8) * 8
    padp = lambda p: jnp.pad(p.reshape(q + 1, S5_LANES), ((0, rows - q - 1), (0, 0)))
    return dict(w_toep=w_toep, wb_re=in_map(bb_re), wb_im=in_map(bb_im), wc_re=out_map(c_re),
                wc_im=out_map(-c_im), p_re=padp(p_re), p_im=padp(p_im))


def _all_s5_tables(lam_re, lam_im, log_dt, b_re, b_im, c_re, c_im, qs):
    f32 = lambda a: a.astype(F32)
    q_max = max(qs)
    full = jax.vmap(functools.partial(_s5_tables, q=q_max))(
        f32(lam_re), f32(lam_im), f32(log_dt), f32(b_re), f32(b_im), f32(c_re), f32(c_im))
    out = {}
    for q in qs:
        rows = ((q + 1 + 7) // 8) * 8
        out[q] = []
        for l in range(lam_re.shape[0]):
            tab = {name: arr[l] for name, arr in full.items()}
            tab['w_toep'] = tab['w_toep'][:q]
            tab['p_re'] = tab['p_re'][:rows]
            tab['p_im'] = tab['p_im'][:rows]
            out[q].append(tab)
    return out


def _s5_chunk_kernel(u_ref, w_ref, wbr_ref, wbi_ref, pr_ref, pi_ref, y_ref, sr_ref, si_ref, *, q):
    for s_out in range(q):
        for half in range(2):
            acc = None
            for j in range(s_out + 1):
                a = (s_out - j) * WIDTH + half * 256
                d = jnp.dot(u_ref[:, a:a + 256], w_ref[j, half], preferred_element_type=F32)
                acc = d if acc is None else acc + d
            o = s_out * WIDTH + half * 256
            y_ref[:, o:o + 256] = acc
    for n in range(8):
        lanes = slice(256 * n, 256 * (n + 1))
        acc_r = None
        acc_i = None
        for s in range(q):
            a = s * WIDTH + 128 * (n // 2)
            br = jnp.dot(u_ref[:, a:a + 128], wbr_ref[n], preferred_element_type=F32)
            bi = jnp.dot(u_ref[:, a:a + 128], wbi_ref[n], preferred_element_type=F32)
            pr = pr_ref[q - 1 - s:q - s, lanes]
            pi = pi_ref[q - 1 - s:q - s, lanes]
            tr = pr * br - pi * bi
            ti = pr * bi + pi * br
            acc_r = tr if acc_r is None else acc_r + tr
            acc_i = ti if acc_i is None else acc_i + ti
        sr_ref[:, lanes] = acc_r
        si_ref[:, lanes] = acc_i


def _s5_chunks(u2b, tab, q, rt):
    nc = u2b.shape[0]
    consts = [tab['w_toep'], tab['wb_re'], tab['wb_im'], tab['p_re'], tab['p_im']]
    state = pl.BlockSpec((rt, S5_LANES), lambda i: (i, 0))
    return pl.pallas_call(
        functools.partial(_s5_chunk_kernel, q=q),
        grid=(nc // rt,),
        in_specs=[pl.BlockSpec((rt, q * WIDTH), lambda i: (i, 0))] + [_const_spec(c.shape) for c in consts],
        out_specs=(pl.BlockSpec((rt, q * WIDTH), lambda i: (i, 0)), state, state),
        out_shape=(jax.ShapeDtypeStruct((nc, q * WIDTH), F32),
                   jax.ShapeDtypeStruct((nc, S5_LANES), F32), jax.ShapeDtypeStruct((nc, S5_LANES), F32)),
        compiler_params=_params("parallel"),
        name="s5_chunks",
    )(u2b, *consts)


def _s5_scan_kernel(u_ref, yi_ref, sre_ref, sim_ref, h0r_ref, h0i_ref, wcr_ref, wci_ref, pr_ref, pi_ref,
                    d_ref, wg_ref, bg_ref, y_ref, htr_ref, hti_ref,
                    xcr_sc, xci_sc, hr_sc, hi_sc, yi_sc, *, q, nct, carry):
    tile = pl.program_id(1)
    u = u_ref[0]
    aq_r = pr_ref[q:q + 1, :]
    aq_i = pi_ref[q:q + 1, :]
    p1r = pr_ref[1:q + 1, :]
    p1i = pi_ref[1:q + 1, :]

    if carry:
        @pl.when(tile == 0)
        def _():
            hr_sc[...] = h0r_ref[0]
            hi_sc[...] = h0i_ref[0]

        def body(c, h):
            hr, hi = h
            row0 = pl.multiple_of(c * q, q)
            xcr_sc[pl.ds(row0, q), :] = p1r * hr - p1i * hi
            xci_sc[pl.ds(row0, q), :] = p1r * hi + p1i * hr
            sr = sre_ref[pl.ds(c, 1), :]
            si = sim_ref[pl.ds(c, 1), :]
            return aq_r * hr - aq_i * hi + sr, aq_r * hi + aq_i * hr + si

        hr, hi = lax.fori_loop(0, nct, body, (hr_sc[...], hi_sc[...]))
        hr_sc[...] = hr
        hi_sc[...] = hi

        @pl.when(tile == pl.num_programs(1) - 1)
        def _():
            htr_ref[0] = hr
            hti_ref[0] = hi
    else:
        def body(c, _):
            hr = h0r_ref[pl.ds(c, 1), :]
            hi = h0i_ref[pl.ds(c, 1), :]
            row0 = pl.multiple_of(c * q, q)
            xcr_sc[pl.ds(row0, q), :] = p1r * hr - p1i * hi
            xci_sc[pl.ds(row0, q), :] = p1r * hi + p1i * hr
            return 0

        lax.fori_loop(0, nct, body, 0)
        h0r = h0r_ref[...]
        h0i = h0i_ref[...]
        htr_ref[...] = aq_r * h0r - aq_i * h0i + sre_ref[...]
        hti_ref[...] = aq_r * h0i + aq_i * h0r + sim_ref[...]

    cols = []
    for m in range(4):
        xr = xcr_sc[:, 512 * m:512 * m + 512].astype(BF16)
        xi = xci_sc[:, 512 * m:512 * m + 512].astype(BF16)
        cols.append(jnp.dot(xr, wcr_ref[m], preferred_element_type=F32)
                    + jnp.dot(xi, wci_ref[m], preferred_element_type=F32))
    for s in range(q):
        for k in range(WIDTH // LANES):
            lo_lane = s * WIDTH + LANES * k
            yi_sc[k, pl.ds(s, nct, stride=q), :] = yi_ref[:, lo_lane:lo_lane + LANES]
    yi = jnp.concatenate([yi_sc[k] for k in range(WIDTH // LANES)], axis=-1)
    y = yi + jnp.concatenate(cols, axis=-1) + d_ref[...] * u
    y = jax.nn.gelu(y)
    g = jnp.dot(y.astype(BF16), wg_ref[...], preferred_element_type=F32) + bg_ref[...]
    y_ref[0] = (y * _sigmoid(g)).astype(y_ref.dtype)


def _s5_scan(u3, yi2, s_re, s_im, h0_re, h0_im, tab, d_row, w_glu, b_glu, q, tm, carry):
    bsz, seq, _ = u3.shape
    nct = tm // q
    n_tiles = seq // tm
    tok = pl.BlockSpec((1, tm, WIDTH), lambda b, t: (b, t, 0))
    chunk = pl.BlockSpec((nct, S5_LANES), lambda b, t: (b * n_tiles + t, 0))
    chunk_y = pl.BlockSpec((nct, q * WIDTH), lambda b, t: (b * n_tiles + t, 0))
    if carry:
        h_spec = pl.BlockSpec((1, 1, S5_LANES), lambda b, t: (b, 0, 0))
        h_shape = jax.ShapeDtypeStruct((bsz, 1, S5_LANES), F32)
    else:
        h_spec = pl.BlockSpec((nct, S5_LANES), lambda b, t: (t, 0))
        h_shape = jax.ShapeDtypeStruct((seq // q, S5_LANES), F32)
    consts = [tab['wc_re'], tab['wc_im'], tab['p_re'], tab['p_im'], d_row, w_glu, b_glu]
    return pl.pallas_call(
        functools.partial(_s5_scan_kernel, q=q, nct=nct, carry=carry),
        grid=(bsz, n_tiles),
        in_specs=[tok, chunk_y, chunk, chunk, h_spec, h_spec] + [_const_spec(c.shape) for c in consts],
        out_specs=(tok, h_spec, h_spec),
        out_shape=(jax.ShapeDtypeStruct((bsz, seq, WIDTH), BF16), h_shape, h_shape),
        scratch_shapes=[pltpu.VMEM((tm, S5_LANES), F32), pltpu.VMEM((tm, S5_LANES), F32),
                        pltpu.VMEM((1, S5_LANES), F32), pltpu.VMEM((1, S5_LANES), F32),
                        pltpu.VMEM((WIDTH // LANES, tm, LANES), F32)],
        compiler_params=_params("parallel", "arbitrary"),
        name="s5_scan",
    )(u3, yi2, s_re, s_im, h0_re, h0_im, *consts)


def _ssd_kernel(z_ref, xbc_ref, sm_ref, smt_ref, conv0_ref, h0_ref, cw_ref, cb_ref, alog_row_ref,
                alog_col_ref, e_ref, d_ref, nw_ref, y_ref, convt_ref, ht_ref,
                xp_sc, st_sc, zp_sc, *, nv, q, group):
    for s in range(group):
        one = pl.ds(s, 1)
        _ssd_sequence(z_ref.at[one], xbc_ref.at[one], sm_ref.at[one], smt_ref.at[one], conv0_ref.at[one],
                      h0_ref.at[:, one], cw_ref, cb_ref, alog_row_ref, alog_col_ref, e_ref, d_ref, nw_ref,
                      y_ref.at[one], convt_ref.at[one], ht_ref.at[one],
                      xp_sc.at[s], st_sc.at[s], zp_sc.at[s], nv=nv, q=q)


def _ssd_sequence(z_ref, xbc_ref, sm_ref, smt_ref, conv0_ref, h0_ref, cw_ref, cb_ref, alog_row_ref,
                  alog_col_ref, e_ref, d_ref, nw_ref, y_ref, convt_ref, ht_ref,
                  xp_sc, st_sc, zp_sc, *, nv, q):
    c = pl.program_id(1)
    last = c == pl.num_programs(1) - 1

    @pl.when(c == 0)
    def _():
        st_sc[...] = h0_ref[0, 0]
        xp_sc[...] = jnp.zeros(xp_sc.shape, F32)
        xp_sc[8 - (SSD_CONV - 1):8, :] = conv0_ref[0]
        if nv < q:
            zp_sc[...] = jnp.zeros(zp_sc.shape, F32)

    xp_sc[8:8 + nv, :] = xbc_ref[0]
    acc = None
    for j in range(SSD_CONV):
        o = 8 - (SSD_CONV - 1) + j
        term = xp_sc[o:o + q, :] * cw_ref[j:j + 1, :]
        acc = term if acc is None else acc + term
    xc = _silu(acc + cb_ref[...])
    tail = xp_sc[8 + nv - (SSD_CONV - 1):8 + nv, :]

    @pl.when(last)
    def _():
        convt_ref[0] = tail

    xp_sc[8 - (SSD_CONV - 1):8, :] = tail
    if nv < q:
        zp_sc[0:nv, :] = z_ref[0]
        z = zp_sc[...]
    else:
        z = z_ref[0]

    xs = xc[:, :WIDTH]
    bm = xc[:, WIDTH:WIDTH + SSD_GROUPS * SSD_STATE]
    cm = xc[:, WIDTH + SSD_GROUPS * SSD_STATE:]

    sm = sm_ref[0]
    smt = smt_ref[0]
    lane = lax.broadcasted_iota(jnp.int32, (1, LANES), 1)
    a_row = jnp.where((lane >= HEADS) & (lane < 2 * HEADS), -jnp.exp(alog_row_ref[...]), 0.0)
    rowi = lax.broadcasted_iota(jnp.int32, (2 * HEADS, 1), 0)
    a_col = jnp.where(rowi >= HEADS, -jnp.exp(alog_col_ref[...]), 0.0)
    adt = sm * a_row
    adt_t = smt * a_col
    r_i = lax.broadcasted_iota(jnp.int32, (q, q), 0)
    c_i = lax.broadcasted_iota(jnp.int32, (q, q), 1)
    tri = r_i >= c_i
    tri_lo = tri.astype(BF16)
    tri_up = (r_i <= c_i).astype(BF16)
    acum = None
    acum_t = None
    for piece, piece_t in zip(_bf16_pieces(adt), _bf16_pieces(adt_t)):
        d = jnp.dot(tri_lo, piece, preferred_element_type=F32)
        dt = jnp.dot(piece_t, tri_up, preferred_element_type=F32)
        acum = d if acum is None else acum + d
        acum_t = dt if acum_t is None else acum_t + dt
    atot = acum[q - 1:q, :]
    stacked = jnp.concatenate([sm, jnp.exp(acum), jnp.exp(atot - acum)], axis=0)
    hi, mid, lo = _bf16_pieces(stacked)
    e = e_ref[...]
    expanded = (jnp.dot(hi, e, preferred_element_type=F32) + jnp.dot(mid, e, preferred_element_type=F32)
                + jnp.dot(lo, e, preferred_element_type=F32))
    dt_e = expanded[0:q]
    eac_e = expanded[q:2 * q]
    dec_e = expanded[2 * q:3 * q]
    xd = xs * dt_e
    xdd = xd * dec_e
    lane_half = lax.broadcasted_iota(jnp.int32, (q, LANES), 1) // HEAD_DIM

    y_cols = []
    for g in range(SSD_GROUPS):
        bg = bm[:, SSD_STATE * g:SSD_STATE * (g + 1)].astype(BF16)
        cg = cm[:, SSD_STATE * g:SSD_STATE * (g + 1)].astype(BF16)
        cb = lax.dot_general(cg, bg, NT_DIMS, preferred_element_type=F32)
        hpg = HEADS // SSD_GROUPS
        st_g = st_sc[hpg * HEAD_DIM * g:hpg * HEAD_DIM * (g + 1), :]
        y_off = lax.dot_general(cg, st_g.astype(BF16), NT_DIMS, preferred_element_type=F32)
        for pair in range(hpg // 2):
            ys = []
            for hh in range(2):
                h = hpg * g + 2 * pair + hh
                diff = acum[:, HEADS + h:HEADS + h + 1] - acum_t[HEADS + h:HEADS + h + 1, :]
                lmat = jnp.exp(jnp.where(tri, diff, NEG))
                w = (cb * lmat).astype(BF16)
                lo = LANES * (2 * g + pair)
                ys.append(jnp.dot(w, xd[:, lo:lo + LANES].astype(BF16), preferred_element_type=F32))
            y_cols.append(jnp.where(lane_half == 0, ys[0], ys[1]))
        y_cols[-2] = y_cols[-2] + y_off[:, :LANES] * eac_e[:, 256 * g:256 * g + LANES]
        y_cols[-1] = y_cols[-1] + y_off[:, LANES:] * eac_e[:, 256 * g + LANES:256 * (g + 1)]
        contrib = jnp.dot(xdd[:, 256 * g:256 * (g + 1)].T.astype(BF16), bg, preferred_element_type=F32)
        for hl in range(hpg):
            h = hpg * g + hl
            sl = slice(HEAD_DIM * h, HEAD_DIM * (h + 1))
            dec_h = jnp.exp(acum_t[HEADS + h:HEADS + h + 1, q - 1:q])
            st_sc[sl, :] = st_sc[sl, :] * dec_h + contrib[HEAD_DIM * hl:HEAD_DIM * (hl + 1), :]

    y = jnp.concatenate(y_cols, axis=-1) + d_ref[...] * xs
    y = y * _silu(z)
    y = y * lax.rsqrt(jnp.mean(y * y, axis=-1, keepdims=True) + RMS_EPS) * nw_ref[...]
    y_ref[0] = y[:nv].astype(y_ref.dtype)

    @pl.when(last)
    def _():
        ht_ref[0] = st_sc[...]


def _ssd(z3, xbc3, sm3, smt3, conv0, h0, layer, cw, cb, alog_row, alog_col, e_mat, d_row, nw, nv, q):
    bsz, seq, _ = z3.shape
    nchunk = seq // nv
    g = SSD_SEQS_PER_STEP if (nchunk == 1 and bsz % SSD_SEQS_PER_STEP == 0) else 1
    consts = [cw, cb, alog_row, alog_col, e_mat, d_row, nw]
    return pl.pallas_call(
        functools.partial(_ssd_kernel, nv=nv, q=q, group=g),
        grid=(bsz // g, nchunk),
        in_specs=[pl.BlockSpec((g, nv, WIDTH), lambda b, c: (b, c, 0)),
                  pl.BlockSpec((g, nv, CONV_DIM), lambda b, c: (b, c, 0)),
                  pl.BlockSpec((g, q, LANES), lambda b, c: (b, c, 0)),
                  pl.BlockSpec((g, 2 * HEADS, q), lambda b, c: (b, 0, c)),
                  pl.BlockSpec((g, SSD_CONV - 1, CONV_DIM), lambda b, c: (b, 0, 0)),
                  pl.BlockSpec((1, g, WIDTH, SSD_STATE), lambda b, c: (layer, b, 0, 0))]
                 + [_const_spec(x.shape) for x in consts],
        out_specs=(pl.BlockSpec((g, nv, WIDTH), lambda b, c: (b, c, 0)),
                   pl.BlockSpec((g, SSD_CONV - 1, CONV_DIM), lambda b, c: (b, 0, 0)),
                   pl.BlockSpec((g, WIDTH, SSD_STATE), lambda b, c: (b, 0, 0))),
        out_shape=(jax.ShapeDtypeStruct((bsz, seq, WIDTH), BF16 if nv % 16 == 0 else F32),
                   jax.ShapeDtypeStruct((bsz, SSD_CONV - 1, CONV_DIM), F32),
                   jax.ShapeDtypeStruct((bsz, WIDTH, SSD_STATE), F32)),
        scratch_shapes=[pltpu.VMEM((g, 8 + q, CONV_DIM), F32), pltpu.VMEM((g, WIDTH, SSD_STATE), F32),
                        pltpu.VMEM((g, q, WIDTH), F32)],
        compiler_params=_params("parallel", "arbitrary"),
        name="ssd",
    )(z3, xbc3, sm3, smt3, conv0, h0, *consts)


def _merge_kernel(x_ref, ys5_ref, yfox_ref, yssd_ref, wg_ref, bg_ref, ws5_ref, wfox_ref, wssd_ref,
                  wo_ref, g_ref, b_ref, o_ref):
    x = x_ref[...]
    xb = x.astype(BF16)
    merged = None
    for br, (y_ref, w_ref) in enumerate(((ys5_ref, ws5_ref), (yfox_ref, wfox_ref), (yssd_ref, wssd_ref))):
        lo = br * D_MODEL
        gate = _sigmoid(jnp.dot(xb, wg_ref[:, lo:lo + D_MODEL], preferred_element_type=F32)
                        + bg_ref[:, lo:lo + D_MODEL])
        term = gate * jnp.dot(y_ref[...].astype(BF16), w_ref[...], preferred_element_type=F32)
        merged = term if merged is None else merged + term
    out = jnp.dot(merged.astype(BF16), wo_ref[...], preferred_element_type=F32)
    o_ref[...] = _layer_norm(ALPHA * x + out, g_ref[...], b_ref[...])


def _merge(x2, ys5, yfox, yssd, wg, bg, ws5, wfox, wssd, wo, g, b, tm):
    t = x2.shape[0]
    row = lambda w: pl.BlockSpec((tm, w), lambda i: (i, 0))
    consts = [wg, bg, ws5, wfox, wssd, wo, g, b]
    return pl.pallas_call(
        _merge_kernel,
        grid=(t // tm,),
        in_specs=[row(D_MODEL), row(WIDTH), row(WIDTH), row(WIDTH)] + [_const_spec(c.shape) for c in consts],
        out_specs=row(D_MODEL),
        out_shape=jax.ShapeDtypeStruct((t, D_MODEL), F32),
        compiler_params=_params("parallel"),
        name="merge",
    )(x2, ys5, yfox, yssd, *consts)


FF_CHUNK = 256


def _ffn_kernel(x_ref, wg_ref, wu_ref, wd_ref, g_ref, b_ref, o_ref):
    x = x_ref[...]
    xb = x.astype(BF16)
    acc = None
    for c in range(0, D_FF, FF_CHUNK):
        gt = jnp.dot(xb, wg_ref[:, c:c + FF_CHUNK], preferred_element_type=F32)
        up = jnp.dot(xb, wu_ref[:, c:c + FF_CHUNK], preferred_element_type=F32)
        h = (_silu(gt) * up).astype(BF16)
        d = jnp.dot(h, wd_ref[c:c + FF_CHUNK, :], preferred_element_type=F32)
        acc = d if acc is None else acc + d
    o_ref[...] = _layer_norm(ALPHA * x + acc, g_ref[...], b_ref[...])


def _ffn(x2, wg, wu, wd, g, b, tm):
    t = x2.shape[0]
    row = pl.BlockSpec((tm, D_MODEL), lambda i: (i, 0))
    consts = [wg, wu, wd, g, b]
    return pl.pallas_call(
        _ffn_kernel,
        grid=(t // tm,),
        in_specs=[row] + [_const_spec(c.shape) for c in consts],
        out_specs=row,
        out_shape=jax.ShapeDtypeStruct((t, D_MODEL), F32),
        compiler_params=_params("parallel"),
        name="ffn",
    )(x2, *consts)


MOE_TOKEN_TILE = 1024


def _moe_kernel(x_ref, wr_ref, br_ref, wg_ref, wu_ref, wd_ref, g_ref, b_ref, o_ref, gate_sc, acc_sc):
    e = pl.program_id(1)
    x = x_ref[...]
    xb = x.astype(BF16)
    lane = lax.broadcasted_iota(jnp.int32, (x.shape[0], LANES), 1)

    @pl.when(e == 0)
    def _():
        logits = jnp.dot(xb, wr_ref[...], preferred_element_type=F32) + br_ref[...]
        logits = jnp.where(lane < N_EXPERTS, logits, NEG)
        m1 = jnp.max(logits, axis=-1, keepdims=True)
        i1 = jnp.min(jnp.where(logits == m1, lane, LANES), axis=-1, keepdims=True)
        rest = jnp.where(lane == i1, NEG, logits)
        m2 = jnp.max(rest, axis=-1, keepdims=True)
        i2 = jnp.min(jnp.where(rest == m2, lane, LANES), axis=-1, keepdims=True)
        e2 = jnp.exp(m2 - m1)
        w1 = 1.0 / (1.0 + e2)
        w2 = e2 / (1.0 + e2)
        gate_sc[...] = jnp.where(lane == i1, w1, jnp.where(lane == i2, w2, 0.0))
        acc_sc[...] = jnp.zeros(acc_sc.shape, F32)

    gate_e = jnp.sum(jnp.where(lane == e, gate_sc[...], 0.0), axis=-1, keepdims=True)
    out_e = None
    for c in range(0, D_MODEL, FF_CHUNK):
        gt = jnp.dot(xb, wg_ref[0, :, c:c + FF_CHUNK], preferred_element_type=F32)
        up = jnp.dot(xb, wu_ref[0, :, c:c + FF_CHUNK], preferred_element_type=F32)
        h = (_silu(gt) * up).astype(BF16)
        d = jnp.dot(h, wd_ref[0, c:c + FF_CHUNK, :], preferred_element_type=F32)
        out_e = d if out_e is None else out_e + d
    acc_sc[...] += gate_e * out_e

    @pl.when(e == N_EXPERTS - 1)
    def _():
        o_ref[...] = _layer_norm(ALPHA * x + acc_sc[...], g_ref[...], b_ref[...])


def _moe(x2, wr, br, wg, wu, wd, g, b, tm):
    t = x2.shape[0]
    row = pl.BlockSpec((tm, D_MODEL), lambda i, e: (i, 0))
    wspec = pl.BlockSpec((1, D_MODEL, D_MODEL), lambda i, e: (e, 0, 0))
    return pl.pallas_call(
        _moe_kernel,
        grid=(t // tm, N_EXPERTS),
        in_specs=[row, _const_spec(wr.shape), _const_spec(br.shape), wspec, wspec, wspec,
                  _const_spec(g.shape), _const_spec(b.shape)],
        out_specs=row,
        out_shape=jax.ShapeDtypeStruct((t, D_MODEL), F32),
        scratch_shapes=[pltpu.VMEM((tm, LANES), F32), pltpu.VMEM((tm, D_MODEL), F32)],
        compiler_params=_params("parallel", "arbitrary"),
        name="moe",
    )(x2, wr, br, wg, wu, wd, g, b)


def _row(v, width=None):
    v = v.reshape(1, -1).astype(F32)
    if width is not None and v.shape[1] < width:
        v = jnp.pad(v, ((0, 0), (0, width - v.shape[1])))
    return v


def _layer_weights(p, l):
    w_in = p['w_in'][l]
    o = np.cumsum([N_BRANCH * D_MODEL, WIDTH, WIDTH, WIDTH, WIDTH, HEADS, WIDTH, CONV_DIM, HEADS])
    gates, u, q, k, v, fg, z, xbc, dtw = (w_in[:, a:b] for a, b in zip([0] + list(o[:-1]), o))
    w = {}
    w['w_main'] = jnp.concatenate([u, q, k, v, z, xbc], axis=1).astype(BF16)
    slab = ((0, 0), (0, 0), (0, LANES - HEAD_DIM))
    k_slab = jnp.pad(k.reshape(D_MODEL, HEADS, HEAD_DIM), slab).reshape(D_MODEL, HEADS * LANES)
    q_slab = jnp.pad(q.reshape(D_MODEL, HEADS, HEAD_DIM), slab).reshape(D_MODEL, HEADS * LANES)
    w['w_rows'] = jnp.concatenate([u, k_slab, z, xbc], axis=1).astype(BF16)
    w['w_cols'] = jnp.concatenate([q_slab, k, v], axis=1).T.astype(BF16)
    e3 = np.zeros((LANES, HEADS * LANES), np.float32)
    for piece in range(BIAS_LANES):
        for h in range(HEADS):
            e3[HEADS * piece + h, LANES * h + HEAD_DIM + piece] = 1.0
    w['bias_place'] = jnp.asarray(e3, BF16)
    w['w_small'] = jnp.pad(jnp.concatenate([fg, dtw], axis=1), ((0, 0), (0, LANES - 2 * HEADS))).astype(BF16)
    w['b_small'] = _row(jnp.concatenate([p['b_fgate'][l], p['ssd_dt_bias'][l]]), LANES)
    w['w_gates'] = gates.astype(BF16)
    w['b_gates'] = _row(p['b_gate'][l])
    for name in ('w_branch_s5', 'w_branch_fox', 'w_branch_ssd', 'w_o'):
        w[name] = p[name][l].astype(BF16)
    for name in ('ln1_g', 'ln1_b', 'ln2_g', 'ln2_b', 's5_d', 's5_b_glu', 'ssd_conv_b', 'ssd_norm_w'):
        w[name] = _row(p[name][l])
    w['s5_w_glu'] = p['s5_w_glu'][l].astype(BF16)
    w['ssd_conv_w'] = p['ssd_conv_w'][l].astype(F32)
    w['ssd_alog_row'] = _row(jnp.concatenate([jnp.zeros((HEADS,), F32), p['ssd_a_log'][l]]), LANES)
    w['ssd_alog_col'] = jnp.concatenate([jnp.zeros((HEADS,), F32), p['ssd_a_log'][l]]).reshape(2 * HEADS, 1)
    w['ssd_d'] = _row(jnp.repeat(p['ssd_d'][l], HEAD_DIM))
    e = np.zeros((LANES, WIDTH), np.float32)
    for h in range(HEADS):
        e[HEADS + h, HEAD_DIM * h:HEAD_DIM * (h + 1)] = 1.0
    w['ssd_expand'] = jnp.asarray(e, BF16)
    if l % 2 == 0:
        for name in ('ffn_w_gate', 'ffn_w_up', 'ffn_w_down'):
            w[name] = p[name][l // 2].astype(BF16)
    else:
        w['moe_w_router'] = jnp.pad(p['moe_w_router'][l // 2], ((0, 0), (0, LANES - N_EXPERTS))).astype(BF16)
        w['moe_b_router'] = _row(p['moe_b_router'][l // 2], LANES)
        for name in ('moe_w_gate', 'moe_w_up', 'moe_w_down'):
            w[name] = p[name][l // 2].astype(BF16)
    return w


def _s5_q(seq):
    return min(S5_CHUNK, seq)


def _run_trunk(x, p, weights, s5_tabs, s5_re0, s5_im0, conv0, ssd0, paged):
    bsz, seq, _ = x.shape
    t = bsz * seq
    tm_mix = min(512, t)
    fox_tile = min(512, seq)
    x2 = x.reshape(t, D_MODEL)
    new = [[] for _ in range(7)]
    kv_all = tuple(jnp.zeros((len(weights), bsz, WIDTH, seq), F32) for _ in range(2)) if paged is None else ()
    q5 = _s5_q(seq)
    ssd0 = ssd0.reshape(ssd0.shape[0], bsz, WIDTH, SSD_STATE)
    if paged is not None:
        cache_k, cache_v, cache_lf, page_table = paged
        cache_kt = cache_k.transpose(0, 1, 3, 4, 2)
        cache_vt = cache_v.transpose(0, 1, 3, 4, 2)
        cache_lft = cache_lf.transpose(0, 1, 3, 2)
    for l in range(len(weights)):
        w = weights[l]
        if paged is None:
            u, ub, z, xbc, small, smt3, kaug, qaugt, kt_all, vt_all, tot = _in_proj_prompt(
                x2.reshape(bsz, seq, D_MODEL), w['w_rows'], w['w_cols'], w['w_small'], w['b_small'],
                w['bias_place'], fox_tile, l, len(weights), kv_all)
            kv_all = (kt_all, vt_all)
            k_out = v_out = None
            logf_out = smt3[:, :HEADS, :].transpose(0, 2, 1)
        else:
            u, ub, qb, k, v, z, xbc, small, small_t, logf = _in_proj_sample(
                x2, w['w_main'], w['w_small'], w['b_small'], min(256, t))
            k_out = k.reshape(bsz, seq, HEADS, HEAD_DIM)
            v_out = v.reshape(bsz, seq, HEADS, HEAD_DIM)
            logf_out = logf.reshape(bsz, seq, HEADS)
            smt3 = small_t.reshape(2 * HEADS, bsz, seq).transpose(1, 0, 2)

        tab = s5_tabs[q5][l]
        yi, cs_re, cs_im = _s5_chunks(ub.reshape(t // q5, q5 * WIDTH), tab, q5, min(128, t // q5))
        d_row, w_glu, b_glu = w['s5_d'], w['s5_w_glu'], w['s5_b_glu']
        if paged is None:
            y_s5, s5_re, s5_im = _s5_scan(u.reshape(bsz, seq, WIDTH), yi, cs_re, cs_im,
                                          s5_re0[l].reshape(bsz, 1, S5_LANES), s5_im0[l].reshape(bsz, 1, S5_LANES),
                                          tab, d_row, w_glu, b_glu, q5, min(256, seq), True)
        else:
            y_s5, s5_re, s5_im = _s5_scan(u.reshape(1, t, WIDTH), yi, cs_re, cs_im,
                                          s5_re0[l].reshape(bsz, S5_LANES), s5_im0[l].reshape(bsz, S5_LANES),
                                          tab, d_row, w_glu, b_glu, q5, min(256, t), False)
        y_s5 = y_s5.reshape(t, WIDTH)
        s5_re = s5_re.reshape(bsz, S5_GROUPS, S5_STATE)
        s5_im = s5_im.reshape(bsz, S5_GROUPS, S5_STATE)

        if paged is None:
            y_fox = _fox_prompt(qaugt, kaug, vt_all, l, tot, fox_tile)
        else:
            lfn_t = jnp.pad(logf.reshape(bsz, seq, HEADS).transpose(0, 2, 1),
                            ((0, 0), (0, 0), (0, LANES - seq)))
            y_fox = _fox_sample_ring(qb.astype(F32).reshape(bsz, seq, WIDTH), k.reshape(bsz, seq, WIDTH),
                                v.reshape(bsz, seq, WIDTH), lfn_t, cache_kt, cache_vt, cache_lft, l, page_table)
        y_fox = y_fox.reshape(t, WIDTH)

        nv = math.gcd(seq, SSD_CHUNK)
        q_ssd = max(nv, SSD_MIN_ROWS)
        sm3 = small.reshape(bsz, seq, LANES)
        if nv < q_ssd:
            sm3 = jnp.pad(sm3, ((0, 0), (0, q_ssd - nv), (0, 0)))
            smt3 = jnp.pad(smt3, ((0, 0), (0, 0), (0, q_ssd - nv)))
        y_ssd, conv_new, ssd_new = _ssd(
            z.reshape(bsz, seq, WIDTH), xbc.reshape(bsz, seq, CONV_DIM), sm3, smt3, conv0[l],
            ssd0, l, w['ssd_conv_w'], w['ssd_conv_b'], w['ssd_alog_row'],
            w['ssd_alog_col'], w['ssd_expand'], w['ssd_d'], w['ssd_norm_w'], nv, q_ssd)
        y_ssd = y_ssd.reshape(t, WIDTH)
        ssd_new = ssd_new.reshape(bsz, HEADS, HEAD_DIM, SSD_STATE)

        x2 = _merge(x2, y_s5, y_fox, y_ssd, w['w_gates'], w['b_gates'], w['w_branch_s5'],
                    w['w_branch_fox'], w['w_branch_ssd'], w['w_o'], w['ln1_g'], w['ln1_b'], tm_mix)
        if l % 2 == 0:
            x2 = _ffn(x2, w['ffn_w_gate'], w['ffn_w_up'], w['ffn_w_down'], w['ln2_g'], w['ln2_b'], tm_mix)
        else:
            x2 = _moe(x2, w['moe_w_router'], w['moe_b_router'], w['moe_w_gate'], w['moe_w_up'],
                      w['moe_w_down'], w['ln2_g'], w['ln2_b'], min(MOE_TOKEN_TILE, t))

        st = (k_out, v_out, logf_out, s5_re, s5_im, conv_new, ssd_new)
        for lst, s in zip(new, st):
            lst.append(s)
    if paged is None:
        new[0], new[1] = ([a.reshape(len(weights), bsz, HEADS, HEAD_DIM, seq).transpose(0, 1, 4, 2, 3)]
                          for a in kv_all)
        stacked = [new[0][0], new[1][0]] + [jnp.stack(s) for s in new[2:]]
    else:
        stacked = [jnp.stack(s) for s in new]
    return x2.reshape(bsz, seq, D_MODEL), stacked


def kernel(x_prompt, x_sample, cache_k, cache_v, cache_logf, page_table, state_s5_re, state_s5_im,
           state_conv, state_ssd, w_in, b_gate, b_fgate, s5_lam_re, s5_lam_im, s5_log_dt, s5_b_re,
           s5_b_im, s5_c_re, s5_c_im, s5_d, s5_w_glu, s5_b_glu, ssd_conv_w, ssd_conv_b, ssd_dt_bias,
           ssd_a_log, ssd_d, ssd_norm_w, w_branch_s5, w_branch_fox, w_branch_ssd, w_o, ln1_g, ln1_b,
           ln2_g, ln2_b, ffn_w_gate, ffn_w_up, ffn_w_down, moe_w_router, moe_b_router, moe_w_gate,
           moe_w_up, moe_w_down):
    p = dict(w_in=w_in, b_gate=b_gate, b_fgate=b_fgate, s5_d=s5_d, s5_w_glu=s5_w_glu, s5_b_glu=s5_b_glu,
             ssd_conv_w=ssd_conv_w, ssd_conv_b=ssd_conv_b, ssd_dt_bias=ssd_dt_bias, ssd_a_log=ssd_a_log,
             ssd_d=ssd_d, ssd_norm_w=ssd_norm_w, w_branch_s5=w_branch_s5, w_branch_fox=w_branch_fox,
             w_branch_ssd=w_branch_ssd, w_o=w_o, ln1_g=ln1_g, ln1_b=ln1_b, ln2_g=ln2_g, ln2_b=ln2_b,
             ffn_w_gate=ffn_w_gate, ffn_w_up=ffn_w_up, ffn_w_down=ffn_w_down, moe_w_router=moe_w_router,
             moe_b_router=moe_b_router, moe_w_gate=moe_w_gate, moe_w_up=moe_w_up, moe_w_down=moe_w_down)
    depth = w_in.shape[0]
    weights = [_layer_weights(p, l) for l in range(depth)]
    qs = {_s5_q(x_prompt.shape[1]), _s5_q(x_sample.shape[1])}
    s5_tabs = _all_s5_tables(s5_lam_re, s5_lam_im, s5_log_dt, s5_b_re, s5_b_im, s5_c_re, s5_c_im, qs)
    bsz = x_prompt.shape[0]
    z_s5 = jnp.zeros((depth, bsz, S5_GROUPS, S5_STATE), F32)
    z_conv = jnp.zeros((depth, bsz, SSD_CONV - 1, CONV_DIM), F32)
    z_ssd = jnp.zeros((depth, bsz, HEADS, HEAD_DIM, SSD_STATE), F32)
    y_p, st_p = _run_trunk(x_prompt, p, weights, s5_tabs, z_s5, z_s5, z_conv, z_ssd, None)
    y_s, st_s = _run_trunk(x_sample, p, weights, s5_tabs, state_s5_re, state_s5_im, state_conv, state_ssd,
                           (cache_k, cache_v, cache_logf, page_table))
    return (y_p, y_s, *st_p, *st_s)
```
